```python
import math
import jax
import jax.numpy as jnp
from jax import lax
import numpy as np

D_MODEL = 2048
BATCH = 2
SEQ = 8192
DEPTH = 1

GRID_W = 64
CTX_LEN = 256
HEAD_DIM = 128
ATTN_WIDTH = D_MODEL // 2
HYENA_WIDTH = D_MODEL - ATTN_WIDTH
MIX_WIDTH = ATTN_WIDTH + HYENA_WIDTH
N_Q_HEADS = ATTN_WIDTH // HEAD_DIM
N_KV_HEADS = 2
KV_GROUP = N_Q_HEADS // N_KV_HEADS
KV_WIDTH = N_KV_HEADS * HEAD_DIM
Q_END = ATTN_WIDTH
K_END = Q_END + KV_WIDTH
V_END = K_END + KV_WIDTH
IN_WIDTH = V_END + 3 * HYENA_WIDTH
Q_BLOCK = 128
ROPE_THETA = 10000.0
ROPE_AXIS_DIM = HEAD_DIM // 2
SHORT_CONV = 3
FILTER_EMB = 33
FILTER_BANDS = (FILTER_EMB - 1) // 2
FILTER_HIDDEN = 64
DECAY_TARGET = 1e-2
FAST_DECAY_PCT = 0.3
SLOW_DECAY_PCT = 1.5
MIN_DECAY = -math.log(DECAY_TARGET) / SLOW_DECAY_PCT
MAX_DECAY = -math.log(DECAY_TARGET) / FAST_DECAY_PCT
D_FF = 5632
N_MOD = 9
EPS = 1e-6
F32 = jnp.float32

kernel_name = "hymba_hyena_gqa_macaron_dit_layer"


def rms_norm(x, g):
    xf = x.astype(F32)
    y = xf * lax.rsqrt(jnp.mean(xf * xf, axis=-1, keepdims=True) + EPS)
    return (y * g.astype(F32)).astype(x.dtype)


def chunk(m, i):
    return m[..., i * D_MODEL:(i + 1) * D_MODEL]


def modulate(h, shift, scale):
    return h * (1 + scale) + shift


def swiglu(h, w_up, w_down):
    gate, up = jnp.split(h @ w_up, 2, axis=-1)
    return (jax.nn.silu(gate) * up) @ w_down


def half_ffn(s, m, slot, g, w_up, w_down):
    h = modulate(rms_norm(s, g), chunk(m, 3 * slot), chunk(m, 3 * slot + 1))
    return s + 0.5 * chunk(m, 3 * slot + 2) * swiglu(h, w_up, w_down)


def axial_rope_angles(L):
    rows = L // GRID_W
    row = jnp.repeat(jnp.arange(rows, dtype=jnp.int32), GRID_W)
    col = jnp.tile(jnp.arange(GRID_W, dtype=jnp.int32), rows)
    inv = ROPE_THETA ** (-jnp.arange(0, ROPE_AXIS_DIM, 2, dtype=F32) / ROPE_AXIS_DIM)
    ang = jnp.concatenate([row.astype(F32)[:, None] * inv, col.astype(F32)[:, None] * inv], axis=-1)
    return jnp.cos(ang), jnp.sin(ang)


def apply_rope(x, cos, sin):
    xf = x.astype(F32).reshape(*x.shape[:-1], HEAD_DIM // 2, 2)
    x0, x1 = xf[..., 0], xf[..., 1]
    c = cos[None, :, None, :]
    s = sin[None, :, None, :]
    out = jnp.stack([x0 * c - x1 * s, x0 * s + x1 * c], axis=-1)
    return out.reshape(x.shape).astype(x.dtype)


def q_heads(p_q, q_norm):
    B, L = p_q.shape[:2]
    return rms_norm(p_q.reshape(B, L, N_Q_HEADS, HEAD_DIM), q_norm)


def kv_heads(p_kv, k_norm):
    B, L = p_kv.shape[:2]
    k = rms_norm(p_kv[..., :KV_WIDTH].reshape(B, L, N_KV_HEADS, HEAD_DIM), k_norm)
    v = p_kv[..., KV_WIDTH:].reshape(B, L, N_KV_HEADS, HEAD_DIM)
    return k, v


def split_proj(p, q_norm, k_norm):
    q = q_heads(p[..., :Q_END], q_norm)
    k, v = kv_heads(p[..., Q_END:V_END], k_norm)
    return q, k, v, p[..., V_END:]


def block_attention(q, k, v):
    B, Lq = q.shape[:2]
    nblk = Lq // Q_BLOCK
    qb = q.reshape(B, nblk, Q_BLOCK, N_KV_HEADS, KV_GROUP, HEAD_DIM).transpose(1, 0, 2, 3, 4, 5)
    kf = k.astype(F32)
    vf = v.astype(F32)
    scale = HEAD_DIM ** -0.5

    def one_block(qblk):
        s = jnp.einsum('bqkgd,bskd->bkgqs', qblk.astype(F32), kf) * scale
        p = jax.nn.softmax(s, axis=-1)
        return jnp.einsum('bkgqs,bskd->bqkgd', p, vf).astype(q.dtype)

    ob = lax.map(one_block, qb)
    return ob.transpose(1, 0, 2, 3, 4, 5).reshape(B, Lq, N_Q_HEADS * HEAD_DIM)


def hyena_filters(L, w1, b1, w2, b2, w3, b3, w4, freq, decay):
    t = jnp.linspace(0.0, 1.0, L, dtype=F32)[:, None]
    w = 2.0 * math.pi * jnp.arange(L, dtype=F32)[:, None] / L
    f = jnp.linspace(1e-4, FILTER_BANDS - 1, FILTER_BANDS, dtype=F32)[None, :]
    z = jnp.concatenate([t, jnp.cos(f * w), -jnp.sin(f * w)], axis=-1)
    fr = freq.astype(F32)
    h = jnp.sin(fr * (z @ w1.astype(F32) + b1.astype(F32)))
    h = jnp.sin(fr * (h @ w2.astype(F32) + b2.astype(F32)))
    h = jnp.sin(fr * (h @ w3.astype(F32) + b3.astype(F32)))
    h = (h @ w4.astype(F32)).reshape(L, 2, HYENA_WIDTH)
    h = h * jnp.exp(-t[:, :, None] * jnp.abs(decay.astype(F32))[None])
    k_fwd, k_bwd = h[:, 0], h[:, 1]
    kk = jnp.concatenate([k_fwd, jnp.zeros((1, HYENA_WIDTH), F32), k_bwd[:0:-1]], axis=0)
    return kk / jnp.sum(jnp.abs(kk), axis=0, keepdims=True)


def long_conv(v, kk):
    L = v.shape[1]
    vf = jnp.fft.rfft(v.astype(F32), n=2 * L, axis=1)
    kf = jnp.fft.rfft(kk, n=2 * L, axis=0)
    return jnp.fft.irfft(vf * kf[None], n=2 * L, axis=1)[:, :L].astype(v.dtype)


def short_conv(u, w, b):
    L = u.shape[1]
    up = jnp.pad(u, ((0, 0), (1, 1), (0, 0)))
    return up[:, :L] * w[0] + up[:, 1:L + 1] * w[1] + up[:, 2:] * w[2] + b


def hyena_mixer(u, kk, conv_w, conv_b, hy_bias):
    uc = short_conv(u, conv_w, conv_b)
    x0, x1, v = jnp.split(uc, 3, axis=-1)
    v = v * x1
    v = long_conv(v, kk) + hy_bias * v
    return v * x0


def merge_groups(attn, hyo, g_out, w_out):
    y = jnp.concatenate([rms_norm(attn, g_out[:ATTN_WIDTH]), rms_norm(hyo, g_out[ATTN_WIDTH:])], axis=-1)
    return y @ w_out


def setup_inputs(seed: int = 0) -> dict:
    key = jax.random.key(seed)
    ks = jax.random.split(key, 32)

    def nrm(k, shape, fan_in, mult=1.0):
        return jax.random.normal(k, shape, F32) * (mult * fan_in ** -0.5)

    def gain(k, shape):
        return 1.0 + 0.05 * jax.random.normal(k, shape, F32)

    def small(k, shape):
        return 0.02 * jax.random.normal(k, shape, F32)

    return {
        "x": jax.random.normal(ks[0], (BATCH, SEQ, D_MODEL), F32),
        "c": jax.random.normal(ks[1], (BATCH, D_MODEL), F32),
        "ctx": jax.random.normal(ks[2], (BATCH, CTX_LEN, D_MODEL), F32),
        "c_ctx": jax.random.normal(ks[3], (D_MODEL,), F32),
        "w_ada": nrm(ks[4], (DEPTH, D_MODEL, N_MOD * D_MODEL), D_MODEL, 0.5),
        "b_ada": small(ks[5], (DEPTH, N_MOD * D_MODEL)),
        "g_norm": gain(ks[6], (DEPTH, 3, D_MODEL)),
        "w_ffn1_up": nrm(ks[7], (DEPTH, D_MODEL, 2 * D_FF), D_MODEL),
        "w_ffn1_down": nrm(ks[8], (DEPTH, D_FF, D_MODEL), D_FF),
        "w_ffn2_up": nrm(ks[9], (DEPTH, D_MODEL, 2 * D_FF), D_MODEL),
        "w_ffn2_down": nrm(ks[10], (DEPTH, D_FF, D_MODEL), D_FF),
        "w_in": nrm(ks[11], (DEPTH, D_MODEL, IN_WIDTH), D_MODEL),
        "q_norm": gain(ks[12], (DEPTH, HEAD_DIM)),
        "k_norm": gain(ks[13], (DEPTH, HEAD_DIM)),
        "conv_w": nrm(ks[14], (DEPTH, SHORT_CONV, 3 * HYENA_WIDTH), SHORT_CONV),
        "conv_b": small(ks[15], (DEPTH, 3 * HYENA_WIDTH)),
        "flt_w1": nrm(ks[16], (DEPTH, FILTER_EMB, FILTER_HIDDEN), FILTER_EMB, 2.0),
        "flt_b1": small(ks[17], (DEPTH, FILTER_HIDDEN)),
        "flt_w2": nrm(ks[18], (DEPTH, FILTER_HIDDEN, FILTER_HIDDEN), FILTER_HIDDEN, 2.0),
        "flt_b2": small(ks[19], (DEPTH, FILTER_HIDDEN)),
        "flt_w3": nrm(ks[20], (DEPTH, FILTER_HIDDEN, FILTER_HIDDEN), FILTER_HIDDEN, 2.0),
        "flt_b3": small(ks[21], (DEPTH, FILTER_HIDDEN)),
        "flt_w4": nrm(ks[22], (DEPTH, FILTER_HIDDEN, 2 * HYENA_WIDTH), FILTER_HIDDEN),
        "flt_freq": gain(ks[23], (DEPTH, FILTER_HIDDEN)),
        "flt_decay": jax.random.uniform(ks[24], (DEPTH, 2, HYENA_WIDTH), F32, MIN_DECAY, MAX_DECAY),
        "hy_bias": jax.random.normal(ks[25], (DEPTH, HYENA_WIDTH), F32),
        "g_out": gain(ks[26], (DEPTH, MIX_WIDTH)),
        "w_out": nrm(ks[27], (DEPTH, MIX_WIDTH, D_MODEL), MIX_WIDTH),
    }


def reference(x, c, ctx, c_ctx, w_ada, b_ada, g_norm, w_ffn1_up, w_ffn1_down, w_ffn2_up, w_ffn2_down,
              w_in, q_norm, k_norm, conv_w, conv_b, flt_w1, flt_b1, flt_w2, flt_b2, flt_w3, flt_b3,
              flt_w4, flt_freq, flt_decay, hy_bias, g_out, w_out):
    L = x.shape[1]
    Lc = ctx.shape[1]
    cos, sin = axial_rope_angles(L)
    s_lat = jax.nn.silu(c)
    s_ctx = jax.nn.silu(c_ctx)[None]
    for l in range(DEPTH):
        update_ctx = l < DEPTH - 1
        mod = (s_lat @ w_ada[l] + b_ada[l])[:, None, :]
        mod_c = (s_ctx @ w_ada[l] + b_ada[l])[:, None, :]
        flt = (flt_w1[l], flt_b1[l], flt_w2[l], flt_b2[l], flt_w3[l], flt_b3[l], flt_w4[l], flt_freq[l], flt_decay[l])

        x = half_ffn(x, mod, 0, g_norm[l, 0], w_ffn1_up[l], w_ffn1_down[l])
        ctx = half_ffn(ctx, mod_c, 0, g_norm[l, 0], w_ffn1_up[l], w_ffn1_down[l])

        h = modulate(rms_norm(x, g_norm[l, 1]), chunk(mod, 3), chunk(mod, 4))
        hc = modulate(rms_norm(ctx, g_norm[l, 1]), chunk(mod_c, 3), chunk(mod_c, 4))
        q, k, v, hy = split_proj(h @ w_in[l], q_norm[l], k_norm[l])
        if update_ctx:
            qc, kc, vc, hyc = split_proj(hc @ w_in[l], q_norm[l], k_norm[l])
        else:
            kc, vc = kv_heads(hc @ w_in[l, :, Q_END:V_END], k_norm[l])
        q = apply_rope(q, cos, sin)
        k = apply_rope(k, cos, sin)
        k_all = jnp.concatenate([kc, k], axis=1)
        v_all = jnp.concatenate([vc, v], axis=1)
        attn = block_attention(q, k_all, v_all)
        hyo = hyena_mixer(hy, hyena_filters(L, *flt), conv_w[l], conv_b[l], hy_bias[l])
        x = x + chunk(mod, 5) * merge_groups(attn, hyo, g_out[l], w_out[l])
        if update_ctx:
            attn_c = block_attention(qc, kc, vc)
            hyo_c = hyena_mixer(hyc, hyena_filters(Lc, *flt), conv_w[l], conv_b[l], hy_bias[l])
            ctx = ctx + chunk(mod_c, 5) * merge_groups(attn_c, hyo_c, g_out[l], w_out[l])
            ctx = half_ffn(ctx, mod_c, 2, g_norm[l, 2], w_ffn2_up[l], w_ffn2_down[l])

        x = half_ffn(x, mod, 2, g_norm[l, 2], w_ffn2_up[l], w_ffn2_down[l])
    return x
```

```python
import functools
import math

import jax
import jax.numpy as jnp
import numpy as np
from jax import lax
from jax.experimental import pallas as pl
from jax.experimental.pallas import tpu as pltpu

F32 = jnp.float32
BF16 = jnp.bfloat16

HEAD_DIM = 128
N_KV_HEADS = 2
GRID_W = 64
ROPE_THETA = 10000.0
FILTER_BANDS = 16
DECAY_EPS = 1e-6
N_MOD = 9

V7X_VMEM_LIMIT_BYTES = 56 * 1024 * 1024
DFT_N2 = 128

_HI = lax.Precision.HIGHEST


def _params(*sem):
    return pltpu.CompilerParams(dimension_semantics=sem, vmem_limit_bytes=V7X_VMEM_LIMIT_BYTES)


def _pick_tile(n, cap, mult):
    best = None
    t = mult
    while t <= min(n, cap):
        if n % t == 0:
            best = t
        t += mult
    assert best is not None, (n, cap, mult)
    return best


def _rms(x, eps=DECAY_EPS):
    return x * lax.rsqrt(jnp.mean(x * x, axis=-1, keepdims=True) + eps)


def _ada_kernel(c_ref, w_ref, b_ref, o_ref):
    c = c_ref[...]
    s = c * jax.nn.sigmoid(c)
    o_ref[...] = jnp.dot(s, w_ref[...], preferred_element_type=F32, precision=_HI) + b_ref[...]


def _ada_mod(c_rows, w_ada, b_ada):
    rows, d = c_rows.shape
    n = w_ada.shape[1]
    tn = _pick_tile(n, 1024, 128)
    return pl.pallas_call(
        _ada_kernel,
        out_shape=jax.ShapeDtypeStruct((rows, n), F32),
        grid=(n // tn,),
        in_specs=[pl.BlockSpec((rows, d), lambda j: (0, 0)),
                  pl.BlockSpec((d, tn), lambda j: (0, j)),
                  pl.BlockSpec((1, tn), lambda j: (0, j))],
        out_specs=pl.BlockSpec((rows, tn), lambda j: (0, j)),
        compiler_params=_params("arbitrary"),
        name="ada_mod",
    )(c_rows, w_ada, b_ada.reshape(1, n))


def _ffn_kernel(x_ref, shift_ref, scale_ref, gate_ref, g_ref, wg_ref, wu_ref, wd_ref, o_ref,
                h_ref, acc_ref):
    k = pl.program_id(1)

    @pl.when(k == 0)
    def _():
        y = _rms(x_ref[...]) * g_ref[...]
        h_ref[...] = (y * (1.0 + scale_ref[0]) + shift_ref[0]).astype(BF16)
        acc_ref[...] = jnp.zeros_like(acc_ref)

    h = h_ref[...]
    g = jnp.dot(h, wg_ref[...], preferred_element_type=F32)
    u = jnp.dot(h, wu_ref[...], preferred_element_type=F32)
    a = (g * jax.nn.sigmoid(g) * u).astype(BF16)
    acc_ref[...] += jnp.dot(a, wd_ref[...], preferred_element_type=F32)

    @pl.when(k == pl.num_programs(1) - 1)
    def _():
        o_ref[...] = x_ref[...] + 0.5 * gate_ref[0] * acc_ref[...]


def _ffn(x, mod3, row_of_tile, slot, g, w_up, w_down, tm):
    t, d = x.shape
    f = w_down.shape[0]
    tf = _pick_tile(f, 512, 128)
    nf = f // tf
    mod_spec = lambda c: pl.BlockSpec((1, 1, d), lambda i, k: (row_of_tile(i), 0, 3 * slot + c))
    return pl.pallas_call(
        _ffn_kernel,
        out_shape=jax.ShapeDtypeStruct((t, d), F32),
        grid=(t // tm, nf),
        in_specs=[pl.BlockSpec((tm, d), lambda i, k: (i, 0)),
                  mod_spec(0), mod_spec(1), mod_spec(2),
                  pl.BlockSpec((1, d), lambda i, k: (0, 0)),
                  pl.BlockSpec((d, tf), lambda i, k: (0, k)),
                  pl.BlockSpec((d, tf), lambda i, k: (0, k + nf)),
                  pl.BlockSpec((tf, d), lambda i, k: (k, 0))],
        out_specs=pl.BlockSpec((tm, d), lambda i, k: (i, 0)),
        scratch_shapes=[pltpu.VMEM((tm, d), BF16), pltpu.VMEM((tm, d), F32)],
        compiler_params=_params("parallel", "arbitrary"),
        name="ffn",
    )(x, mod3, mod3, mod3, g.reshape(1, d), w_up, w_up, w_down)


def _rope(x, cos, sin):
    return x * cos + pltpu.roll(x, HEAD_DIM // 2, axis=1) * sin


def _in_proj_kernel(x_ref, shift_ref, scale_ref, g_ref, w_ref, qn_ref, kn_ref, cos_ref, sin_ref,
                    q_ref, k_ref, v_ref, hy_ref, h_ref, *, nq, q_scale):
    j = pl.program_id(1)
    tn = w_ref.shape[1]
    heads = tn // HEAD_DIM

    @pl.when(j == 0)
    def _():
        y = _rms(x_ref[...]) * g_ref[...]
        h_ref[...] = (y * (1.0 + scale_ref[0]) + shift_ref[0]).astype(BF16)

    u = jnp.dot(h_ref[...], w_ref[...], preferred_element_type=F32)

    def normed(h, gain):
        return _rope(_rms(u[:, h * HEAD_DIM:(h + 1) * HEAD_DIM]) * gain, cos_ref[...], sin_ref[...])

    @pl.when(j < nq)
    def _():
        for h in range(heads):
            q_ref[:, h * HEAD_DIM:(h + 1) * HEAD_DIM] = (normed(h, qn_ref[...]) * q_scale).astype(BF16)

    @pl.when(j == nq)
    def _():
        for h in range(N_KV_HEADS):
            k_ref[:, h * HEAD_DIM:(h + 1) * HEAD_DIM] = normed(h, kn_ref[...]).astype(BF16)
        v_ref[...] = u[:, N_KV_HEADS * HEAD_DIM:].astype(BF16)

    @pl.when(j > nq)
    def _():
        hy_ref[...] = u


def _in_proj(x, mod3, row_of_tile, g, w_in, qn, kn, cos, sin, seq, tm):
    t, d = x.shape
    kvw = N_KV_HEADS * HEAD_DIM
    tn = 2 * kvw
    attn_w = d // 2
    hy_w = w_in.shape[1] - attn_w - 2 * kvw
    assert attn_w % tn == 0 and hy_w % tn == 0
    nq = attn_w // tn
    nj = w_in.shape[1] // tn
    pos_tiles = seq // tm
    mod_spec = lambda c: pl.BlockSpec((1, 1, d), lambda i, j: (row_of_tile(i), 0, 3 + c))
    kern = functools.partial(_in_proj_kernel, nq=nq, q_scale=HEAD_DIM ** -0.5)
    return pl.pallas_call(
        kern,
        out_shape=(jax.ShapeDtypeStruct((t, attn_w), BF16),
                   jax.ShapeDtypeStruct((t, kvw), BF16),
                   jax.ShapeDtypeStruct((t, kvw), BF16),
                   jax.ShapeDtypeStruct((t, hy_w), F32)),
        grid=(t // tm, nj),
        in_specs=[pl.BlockSpec((tm, d), lambda i, j: (i, 0)),
                  mod_spec(0), mod_spec(1),
                  pl.BlockSpec((1, d), lambda i, j: (0, 0)),
                  pl.BlockSpec((d, tn), lambda i, j: (0, j)),
                  pl.BlockSpec((1, HEAD_DIM), lambda i, j: (0, 0)),
                  pl.BlockSpec((1, HEAD_DIM), lambda i, j: (0, 0)),
                  pl.BlockSpec((tm, HEAD_DIM), lambda i, j: (i % pos_tiles, 0)),
                  pl.BlockSpec((tm, HEAD_DIM), lambda i, j: (i % pos_tiles, 0))],
        out_specs=(pl.BlockSpec((tm, tn), lambda i, j: (i, jnp.minimum(j, nq - 1))),
                   pl.BlockSpec((tm, kvw), lambda i, j: (i, 0)),
                   pl.BlockSpec((tm, kvw), lambda i, j: (i, 0)),
                   pl.BlockSpec((tm, tn), lambda i, j: (i, jnp.maximum(j - nq - 1, 0)))),
        scratch_shapes=[pltpu.VMEM((tm, d), BF16)],
        compiler_params=_params("parallel", "arbitrary"),
        name="in_proj",
    )(x, mod3, mod3, g.reshape(1, d), w_in, qn, kn, cos, sin)


def _ctx_kv_kernel(x_ref, shift_ref, scale_ref, g_ref, w_ref, kn_ref, k_ref, v_ref):
    y = _rms(x_ref[...]) * g_ref[...]
    h = (y * (1.0 + scale_ref[0]) + shift_ref[0]).astype(BF16)
    u = jnp.dot(h, w_ref[...], preferred_element_type=F32)
    for hd in range(N_KV_HEADS):
        k_ref[:, hd * HEAD_DIM:(hd + 1) * HEAD_DIM] = (
            _rms(u[:, hd * HEAD_DIM:(hd + 1) * HEAD_DIM]) * kn_ref[...]).astype(BF16)
    v_ref[...] = u[:, N_KV_HEADS * HEAD_DIM:].astype(BF16)


def _ctx_kv(x, mod3, ctx_row, g, w_kv, kn, tm):
    t, d = x.shape
    kvw = N_KV_HEADS * HEAD_DIM
    mod_spec = lambda c: pl.BlockSpec((1, 1, d), lambda i: (ctx_row, 0, 3 + c))
    return pl.pallas_call(
        _ctx_kv_kernel,
        out_shape=(jax.ShapeDtypeStruct((t, kvw), BF16), jax.ShapeDtypeStruct((t, kvw), BF16)),
        grid=(t // tm,),
        in_specs=[pl.BlockSpec((tm, d), lambda i: (i, 0)),
                  mod_spec(0), mod_spec(1),
                  pl.BlockSpec((1, d), lambda i: (0, 0)),
                  pl.BlockSpec((d, 2 * kvw), lambda i: (0, 0)),
                  pl.BlockSpec((1, HEAD_DIM), lambda i: (0, 0))],
        out_specs=(pl.BlockSpec((tm, kvw), lambda i: (i, 0)),
                   pl.BlockSpec((tm, kvw), lambda i: (i, 0))),
        compiler_params=_params("parallel"),
        name="ctx_kv",
    )(x, mod3, mod3, g.reshape(1, d), w_kv, kn)


def _attn_kernel(q_ref, k_ref, v_ref, o_ref, m_ref, l_ref, acc_ref, *, tk, group):
    tq = q_ref.shape[0]
    nk = k_ref.shape[1] // tk
    q = jnp.concatenate([q_ref[:, h * HEAD_DIM:(h + 1) * HEAD_DIM] for h in range(group)], axis=0)
    m_ref[...] = jnp.full_like(m_ref, -jnp.inf)
    l_ref[...] = jnp.zeros_like(l_ref)
    acc_ref[...] = jnp.zeros_like(acc_ref)

    def body(j, carry):
        start = pl.multiple_of(j * tk, tk)
        ks = k_ref[0, pl.ds(start, tk), :]
        vs = v_ref[0, pl.ds(start, tk), :]
        s = lax.dot_general(q, ks, (((1,), (1,)), ((), ())), preferred_element_type=F32)
        m_prev = m_ref[...]
        m_new = jnp.maximum(m_prev, jnp.max(s, axis=-1, keepdims=True))
        alpha = jnp.exp(m_prev - m_new)
        p = jnp.exp(s - m_new)
        l_ref[...] = alpha * l_ref[...] + jnp.sum(p, axis=-1, keepdims=True)
        acc_ref[...] = alpha * acc_ref[...] + jnp.dot(p.astype(BF16), vs, preferred_element_type=F32)
        m_ref[...] = m_new
        return carry

    lax.fori_loop(0, nk, body, 0)
    o = acc_ref[...] / l_ref[...]
    for h in range(group):
        o_ref[:, h * HEAD_DIM:(h + 1) * HEAD_DIM] = o[h * tq:(h + 1) * tq].astype(BF16)


def _attention(q, k_all, v_all, batch, seq):
    t, attn_w = q.shape
    lk = k_all.shape[1]
    group = attn_w // HEAD_DIM // N_KV_HEADS
    gw = group * HEAD_DIM
    tq = _pick_tile(seq, 128, 8)
    tk = _pick_tile(lk, 1024, 128)
    nq = seq // tq
    kern = functools.partial(_attn_kernel, tk=tk, group=group)
    return pl.pallas_call(
        kern,
        out_shape=jax.ShapeDtypeStruct((t, attn_w), BF16),
        grid=(batch, N_KV_HEADS, nq),
        in_specs=[pl.BlockSpec((tq, gw), lambda b, g, i: (b * nq + i, g)),
                  pl.BlockSpec((1, lk, HEAD_DIM), lambda b, g, i: (b, 0, g)),
                  pl.BlockSpec((1, lk, HEAD_DIM), lambda b, g, i: (b, 0, g))],
        out_specs=pl.BlockSpec((tq, gw), lambda b, g, i: (b * nq + i, g)),
        scratch_shapes=[pltpu.VMEM((group * tq, 1), F32), pltpu.VMEM((group * tq, 1), F32),
                        pltpu.VMEM((group * tq, HEAD_DIM), F32)],
        compiler_params=_params("parallel", "parallel", "arbitrary"),
        name="attention",
    )(q, k_all, v_all)


def _hy_pre_kernel(x0_ref, x1_ref, v_ref, x0p_ref, x1p_ref, vp_ref, x0n_ref, x1n_ref, vn_ref,
                   w_ref, b_ref, x0c_ref, vg_ref, *, tiles_per_seq):
    i = pl.program_id(0)
    tm = x0_ref.shape[0]
    first = (i % tiles_per_seq) == 0
    last = (i % tiles_per_seq) == tiles_per_seq - 1
    row = lax.broadcasted_iota(jnp.int32, x0_ref.shape, 0)

    def conv(u_ref, up_ref, un_ref, c):
        u = u_ref[...]
        prev_row = jnp.where(first, 0.0, up_ref[7:8, :])
        next_row = jnp.where(last, 0.0, un_ref[0:1, :])
        um = jnp.where(row == 0, prev_row, pltpu.roll(u, 1, axis=0))
        upl = jnp.where(row == tm - 1, next_row, pltpu.roll(u, tm - 1, axis=0))
        w = w_ref[c]
        return um * w[0:1, :] + u * w[1:2, :] + upl * w[2:3, :] + b_ref[c]

    x0c_ref[...] = conv(x0_ref, x0p_ref, x0n_ref, 0)
    vg_ref[...] = conv(v_ref, vp_ref, vn_ref, 2) * conv(x1_ref, x1p_ref, x1n_ref, 1)


def _hy_pre(hy, conv_w, conv_b, seq, tm):
    t, c3 = hy.shape
    c = c3 // 3
    cw = _pick_tile(c, 512, 128)
    ncw = c // cw
    tiles_per_seq = seq // tm
    rb = tm // 8
    nrb = t // 8
    w3 = conv_w.reshape(3, 3, c).transpose(1, 0, 2)
    b3 = conv_b.reshape(3, 1, c)
    main = lambda gidx: pl.BlockSpec((tm, cw), lambda i, j: (i, gidx * ncw + j))
    prev = lambda gidx: pl.BlockSpec((8, cw), lambda i, j: (jnp.maximum(i * rb - 1, 0), gidx * ncw + j))
    nxt = lambda gidx: pl.BlockSpec((8, cw), lambda i, j: (jnp.minimum((i + 1) * rb, nrb - 1), gidx * ncw + j))
    kern = functools.partial(_hy_pre_kernel, tiles_per_seq=tiles_per_seq)
    return pl.pallas_call(
        kern,
        out_shape=(jax.ShapeDtypeStruct((t, c), F32), jax.ShapeDtypeStruct((t, c), F32)),
        grid=(t // tm, ncw),
        in_specs=[main(0), main(1), main(2), prev(0), prev(1), prev(2), nxt(0), nxt(1), nxt(2),
                  pl.BlockSpec((3, 3, cw), lambda i, j: (0, 0, j)),
                  pl.BlockSpec((3, 1, cw), lambda i, j: (0, 0, j))],
        out_specs=(pl.BlockSpec((tm, cw), lambda i, j: (i, j)),
                   pl.BlockSpec((tm, cw), lambda i, j: (i, j))),
        compiler_params=_params("parallel", "parallel"),
        name="hyena_pre",
    )(hy, hy, hy, hy, hy, hy, hy, hy, hy, w3, b3)


def _dft_tables(n1, n2):
    n = n1 * n2
    k1 = np.arange(n1)[:, None]
    a = np.arange(n1)[None, :]
    m2 = np.arange(n2)[:, None, None]
    theta = 2.0 * np.pi * ((((n2 * a * k1)[None] + m2 * k1[None]) % n) / n)
    fwd = np.concatenate([np.cos(theta), -np.sin(theta)], axis=1)
    k2 = np.arange(n2)[:, None]
    b = np.arange(n2)[None, :]
    phi = 2.0 * np.pi * ((k2 * b) % n2) / n2
    c, s = np.cos(phi), np.sin(phi)
    fmat = np.block([[c, s], [-s, c]])
    return fwd, fmat


def _stage_a_kernel(x_ref, a_ref, yr_ref, yi_ref):
    n1 = yr_ref.shape[1]
    y = jnp.dot(a_ref[0], x_ref[0].astype(BF16), preferred_element_type=F32)
    yr_ref[0] = y[:n1].astype(BF16)
    yi_ref[0] = y[n1:].astype(BF16)


def _stage_a(vg3, a_tab, n1, c):
    b, n1h, _ = vg3.shape
    n2 = DFT_N2
    return pl.pallas_call(
        _stage_a_kernel,
        out_shape=(jax.ShapeDtypeStruct((b, n1, n2 * c), BF16),) * 2,
        grid=(b, n2),
        in_specs=[pl.BlockSpec((1, n1h, c), lambda bi, m: (bi, 0, m)),
                  pl.BlockSpec((1, 2 * n1, n1h), lambda bi, m: (m, 0, 0))],
        out_specs=(pl.BlockSpec((1, n1, c), lambda bi, m: (bi, 0, m)),) * 2,
        compiler_params=_params("parallel", "parallel"),
        name="hyena_dft_a",
    )(vg3, a_tab)


def _stage_b_kernel(yr_ref, yi_ref, f_ref, ft_ref, kr_ref, ki_ref, ur_ref, ui_ref):
    n2 = yr_ref.shape[2]
    y = jnp.concatenate([yr_ref[0, 0], yi_ref[0, 0]], axis=0)
    z = jnp.dot(f_ref[...], y, preferred_element_type=F32)
    zr, zi = z[:n2], z[n2:]
    kr = kr_ref[0].astype(F32)
    ki = ki_ref[0].astype(F32)
    p = jnp.concatenate([zr * kr - zi * ki, zr * ki + zi * kr], axis=0).astype(BF16)
    u = jnp.dot(ft_ref[...], p, preferred_element_type=F32)
    ur_ref[0] = u[:n2].astype(BF16)
    ui_ref[0] = u[n2:].astype(BF16)


def _stage_b(yr, yi, fmat, fmat_t, kfr, kfi, n1, c):
    b = yr.shape[0]
    n2 = DFT_N2
    return pl.pallas_call(
        _stage_b_kernel,
        out_shape=(jax.ShapeDtypeStruct((b, n2, n1 * c), BF16),) * 2,
        grid=(b, n1),
        in_specs=[pl.BlockSpec((1, 1, n2, c), lambda bi, k: (bi, k, 0, 0)),
                  pl.BlockSpec((1, 1, n2, c), lambda bi, k: (bi, k, 0, 0)),
                  pl.BlockSpec((2 * n2, 2 * n2), lambda bi, k: (0, 0)),
                  pl.BlockSpec((2 * n2, 2 * n2), lambda bi, k: (0, 0)),
                  pl.BlockSpec((1, n2, c), lambda bi, k: (k, 0, 0)),
                  pl.BlockSpec((1, n2, c), lambda bi, k: (k, 0, 0))],
        out_specs=(pl.BlockSpec((1, n2, c), lambda bi, k: (bi, 0, k)),) * 2,
        compiler_params=_params("parallel", "parallel"),
        name="hyena_dft_b",
    )(yr, yi, fmat, fmat_t, kfr, kfi)


def _stage_c_kernel(ur_ref, ui_ref, a_ref, vg_ref, x0_ref, hb_ref, g_ref, o_ref):
    u = jnp.concatenate([ur_ref[0, 0], ui_ref[0, 0]], axis=0)
    conv = jnp.dot(a_ref[0], u, preferred_element_type=F32)
    hyo = (conv + hb_ref[...] * vg_ref[0]) * x0_ref[0]
    o_ref[0] = (_rms(hyo) * g_ref[...]).astype(BF16)


def _stage_c(ur, ui, at_tab, vg3, x03, hy_bias, g_hy, n1, c):
    b, n1h, _ = vg3.shape
    n2 = DFT_N2
    return pl.pallas_call(
        _stage_c_kernel,
        out_shape=jax.ShapeDtypeStruct((b, n1h, n2 * c), BF16),
        grid=(b, n2),
        in_specs=[pl.BlockSpec((1, 1, n1, c), lambda bi, m: (bi, m, 0, 0)),
                  pl.BlockSpec((1, 1, n1, c), lambda bi, m: (bi, m, 0, 0)),
                  pl.BlockSpec((1, n1h, 2 * n1), lambda bi, m: (m, 0, 0)),
                  pl.BlockSpec((1, n1h, c), lambda bi, m: (bi, 0, m)),
                  pl.BlockSpec((1, n1h, c), lambda bi, m: (bi, 0, m)),
                  pl.BlockSpec((1, c), lambda bi, m: (0, 0)),
                  pl.BlockSpec((1, c), lambda bi, m: (0, 0))],
        out_specs=pl.BlockSpec((1, n1h, c), lambda bi, m: (bi, 0, m)),
        compiler_params=_params("parallel", "parallel"),
        name="hyena_dft_c",
    )(ur, ui, at_tab, vg3, x03, hy_bias.reshape(1, c), g_hy.reshape(1, c))


def _filter_a_kernel(a_ref, w1t_ref, w1c_ref, w1s_ref, b1_ref, w2_ref, b2_ref, w3_ref, b3_ref,
                     w4_ref, fr_ref, dec_ref, yr_ref, yi_ref, norm_ref, *, seq):
    m = pl.program_id(0)
    n1 = yr_ref.shape[0]
    c = yr_ref.shape[1]
    n1h = n1 // 2
    n2 = DFT_N2
    row = lax.broadcasted_iota(jnp.int32, (n1, 1), 0)
    tprime = row * n2 + m
    fwd = row < n1h
    pos = jnp.where(fwd, tprime, 2 * seq - tprime).astype(F32)
    t = pos / (seq - 1.0)
    w = (2.0 * math.pi * pos) / seq
    band = lax.broadcasted_iota(jnp.int32, (1, FILTER_BANDS), 1).astype(F32)
    f = 1e-4 + band * ((FILTER_BANDS - 1 - 1e-4) / (FILTER_BANDS - 1))
    ang = f * w
    fr = fr_ref[...]
    dot = functools.partial(jnp.dot, preferred_element_type=F32, precision=_HI)
    h = t * w1t_ref[...] + dot(jnp.cos(ang), w1c_ref[...]) + dot(-jnp.sin(ang), w1s_ref[...])
    h = jnp.sin(fr * (h + b1_ref[...]))
    h = jnp.sin(fr * (dot(h, w2_ref[...]) + b2_ref[...]))
    h = jnp.sin(fr * (dot(h, w3_ref[...]) + b3_ref[...]))
    h = dot(h, w4_ref[...])
    dec = jnp.abs(dec_ref[...])
    kf = h[:, :c] * jnp.exp(-t * dec[0:1, :])
    kb = h[:, c:] * jnp.exp(-t * dec[1:2, :])
    kk = jnp.where(fwd, kf, jnp.where(tprime == seq, 0.0, kb))

    @pl.when(m == 0)
    def _():
        norm_ref[...] = jnp.zeros_like(norm_ref)

    norm_ref[...] += jnp.sum(jnp.abs(kk), axis=0, keepdims=True)
    y = jnp.dot(a_ref[0], kk.astype(BF16), preferred_element_type=F32)
    yr_ref[...] = y[:n1].astype(BF16)
    yi_ref[...] = y[n1:].astype(BF16)


def _filter_a(a_full, flt, seq, n1, c):
    w1, b1, w2, b2, w3, b3, w4, freq, decay = flt
    hid = w2.shape[0]
    n2 = DFT_N2
    full = lambda shape: pl.BlockSpec(shape, lambda m: (0,) * len(shape))
    kern = functools.partial(_filter_a_kernel, seq=seq)
    return pl.pallas_call(
        kern,
        out_shape=(jax.ShapeDtypeStruct((n1, n2 * c), BF16),
                   jax.ShapeDtypeStruct((n1, n2 * c), BF16),
                   jax.ShapeDtypeStruct((1, c), F32)),
        grid=(n2,),
        in_specs=[pl.BlockSpec((1, 2 * n1, n1), lambda m: (m, 0, 0)),
                  full((1, hid)), full((FILTER_BANDS, hid)), full((FILTER_BANDS, hid)), full((1, hid)),
                  full((hid, hid)), full((1, hid)), full((hid, hid)), full((1, hid)),
                  full((hid, 2 * c)), full((1, hid)), full((2, c))],
        out_specs=(pl.BlockSpec((n1, c), lambda m: (0, m)),
                   pl.BlockSpec((n1, c), lambda m: (0, m)),
                   pl.BlockSpec((1, c), lambda m: (0, 0))),
        compiler_params=_params("arbitrary"),
        name="hyena_filter_a",
    )(a_full, w1[0:1], w1[1:1 + FILTER_BANDS], w1[1 + FILTER_BANDS:], b1.reshape(1, hid),
      w2, b2.reshape(1, hid), w3, b3.reshape(1, hid), w4, freq.reshape(1, hid), decay)


def _filter_b_kernel(yr_ref, yi_ref, f_ref, norm_ref, kr_ref, ki_ref):
    n2 = yr_ref.shape[1]
    y = jnp.concatenate([yr_ref[0], yi_ref[0]], axis=0)
    z = jnp.dot(f_ref[...], y, preferred_element_type=F32) / norm_ref[...]
    kr_ref[0] = z[:n2].astype(BF16)
    ki_ref[0] = z[n2:].astype(BF16)


def _filter_b(yr, yi, fmat, norm, n1, c):
    n2 = DFT_N2
    return pl.pallas_call(
        _filter_b_kernel,
        out_shape=(jax.ShapeDtypeStruct((n1, n2, c), BF16),) * 2,
        grid=(n1,),
        in_specs=[pl.BlockSpec((1, n2, c), lambda k: (k, 0, 0)),
                  pl.BlockSpec((1, n2, c), lambda k: (k, 0, 0)),
                  pl.BlockSpec((2 * n2, 2 * n2), lambda k: (0, 0)),
                  pl.BlockSpec((1, c), lambda k: (0, 0))],
        out_specs=(pl.BlockSpec((1, n2, c), lambda k: (k, 0, 0)),) * 2,
        compiler_params=_params("parallel"),
        name="hyena_filter_b",
    )(yr, yi, fmat, norm)


def _hyena(hy, flt, conv_w, conv_b, hy_bias, g_hy, batch, seq, tm):
    c = hy.shape[1] // 3
    n2 = DFT_N2
    n1h = seq // n2
    n1 = 2 * n1h
    fwd_np, fmat_np = _dft_tables(n1, n2)
    a_full = jnp.asarray(fwd_np, BF16)
    a_half = jnp.asarray(fwd_np[:, :, :n1h], BF16)
    a_inv = jnp.asarray(fwd_np[:, :, :n1h].transpose(0, 2, 1) / (n1 * n2), BF16)
    fmat = jnp.asarray(fmat_np, BF16)
    fmat_t = jnp.asarray(fmat_np.T, BF16)

    fyr, fyi, norm = _filter_a(a_full, flt, seq, n1, c)
    kfr, kfi = _filter_b(fyr.reshape(n1, n2, c), fyi.reshape(n1, n2, c), fmat, norm, n1, c)

    x0c, vg = _hy_pre(hy, conv_w, conv_b, seq, tm)
    vg3 = vg.reshape(batch, n1h, n2 * c)
    x03 = x0c.reshape(batch, n1h, n2 * c)
    yr, yi = _stage_a(vg3, a_half, n1, c)
    ur, ui = _stage_b(yr.reshape(batch, n1, n2, c), yi.reshape(batch, n1, n2, c),
                      fmat, fmat_t, kfr, kfi, n1, c)
    y = _stage_c(ur.reshape(batch, n2, n1, c), ui.reshape(batch, n2, n1, c), a_inv,
                 vg3, x03, hy_bias, g_hy, n1, c)
    return y.reshape(batch * seq, c)


def _out_proj_kernel(x_ref, gate_ref, attn_ref, hy_ref, ga_ref, wa_ref, wh_ref, o_ref):
    ya = (_rms(attn_ref[...].astype(F32)) * ga_ref[...]).astype(BF16)
    y = jnp.dot(ya, wa_ref[...], preferred_element_type=F32)
    y += jnp.dot(hy_ref[...], wh_ref[...], preferred_element_type=F32)
    o_ref[...] = x_ref[...] + gate_ref[0] * y


def _out_proj(x, mod3, row_of_tile, attn, hyn, g_attn, w_out, tm):
    t, d = x.shape
    aw = attn.shape[1]
    hw = hyn.shape[1]
    return pl.pallas_call(
        _out_proj_kernel,
        out_shape=jax.ShapeDtypeStruct((t, d), F32),
        grid=(t // tm,),
        in_specs=[pl.BlockSpec((tm, d), lambda i: (i, 0)),
                  pl.BlockSpec((1, 1, d), lambda i: (row_of_tile(i), 0, 5)),
                  pl.BlockSpec((tm, aw), lambda i: (i, 0)),
                  pl.BlockSpec((tm, hw), lambda i: (i, 0)),
                  pl.BlockSpec((1, aw), lambda i: (0, 0)),
                  pl.BlockSpec((aw, d), lambda i: (0, 0)),
                  pl.BlockSpec((hw, d), lambda i: (1, 0))],
        out_specs=pl.BlockSpec((tm, d), lambda i: (i, 0)),
        compiler_params=_params("parallel"),
        name="out_proj",
    )(x, mod3, attn, hyn, g_attn.reshape(1, aw), w_out, w_out)


def _rope_tables(seq):
    half = HEAD_DIM // 2
    pos = np.arange(seq)
    inv = ROPE_THETA ** (-np.arange(0, half, 2, dtype=np.float64) / half)
    ang = np.concatenate([(pos // GRID_W)[:, None] * inv, (pos % GRID_W)[:, None] * inv], axis=-1)
    cos = np.concatenate([np.cos(ang), np.cos(ang)], axis=-1)
    sin = np.concatenate([-np.sin(ang), np.sin(ang)], axis=-1)
    return jnp.asarray(cos, F32), jnp.asarray(sin, F32)


def _deinterleave_cols(w, heads):
    lead = w.shape[:-1]
    return w.reshape(*lead, heads, HEAD_DIM // 2, 2).swapaxes(-1, -2).reshape(*lead, heads * HEAD_DIM)


def kernel(x, c, ctx, c_ctx, w_ada, b_ada, g_norm, w_ffn1_up, w_ffn1_down, w_ffn2_up, w_ffn2_down,
           w_in, q_norm, k_norm, conv_w, conv_b, flt_w1, flt_b1, flt_w2, flt_b2, flt_w3, flt_b3,
           flt_w4, flt_freq, flt_decay, hy_bias, g_out, w_out):
    batch, seq, d = x.shape
    lc = ctx.shape[1]
    depth = w_ada.shape[0]
    assert depth == 1, "context-update path of deeper stacks is not implemented"
    attn_w = d // 2
    kvw = N_KV_HEADS * HEAD_DIM
    q_heads = attn_w // HEAD_DIM
    tm = _pick_tile(seq, 512, 128)
    tmc = _pick_tile(lc, 512, 8)
    tiles_per_seq = seq // tm
    lat_row = lambda i: i // tiles_per_seq
    ctx_row = lambda i: batch

    xs = x.reshape(batch * seq, d)
    cs = ctx.reshape(batch * lc, d)
    rows = -(-(batch + 1) // 8) * 8
    c_rows = jnp.zeros((rows, d), F32).at[:batch].set(c).at[batch].set(c_ctx)
    cos, sin = _rope_tables(seq)

    l = 0
    mod3 = _ada_mod(c_rows, w_ada[l], b_ada[l]).reshape(rows, 1, N_MOD * d)

    w1u, w1d = w_ffn1_up[l].astype(BF16), w_ffn1_down[l].astype(BF16)
    w2u, w2d = w_ffn2_up[l].astype(BF16), w_ffn2_down[l].astype(BF16)
    wi = w_in[l]
    k_end = attn_w + kvw
    wi = jnp.concatenate([_deinterleave_cols(wi[:, :k_end], q_heads + N_KV_HEADS), wi[:, k_end:]],
                         axis=1).astype(BF16)
    qn = _deinterleave_cols(q_norm[l], 1).reshape(1, HEAD_DIM)
    kn = _deinterleave_cols(k_norm[l], 1).reshape(1, HEAD_DIM)
    wo = w_out[l].astype(BF16)

    x1 = _ffn(xs, mod3, lat_row, 0, g_norm[l, 0], w1u, w1d, tm)
    c1 = _ffn(cs, mod3, ctx_row, 0, g_norm[l, 0], w1u, w1d, tmc)

    q, k, v, hy = _in_proj(x1, mod3, lat_row, g_norm[l, 1], wi, qn, kn, cos, sin, seq, tm)
    kc, vc = _ctx_kv(c1, mod3, batch, g_norm[l, 1], wi[:, attn_w:attn_w + 2 * kvw], kn, tmc)
    k_all = jnp.concatenate([kc.reshape(batch, lc, kvw), k.reshape(batch, seq, kvw)], axis=1)
    v_all = jnp.concatenate([vc.reshape(batch, lc, kvw), v.reshape(batch, seq, kvw)], axis=1)
    attn = _attention(q, k_all, v_all, batch, seq)

    flt = (flt_w1[l], flt_b1[l], flt_w2[l], flt_b2[l], flt_w3[l], flt_b3[l], flt_w4[l],
           flt_freq[l], flt_decay[l])
    hyn = _hyena(hy, flt, conv_w[l], conv_b[l], hy_bias[l], g_out[l, attn_w:], batch, seq, tm)
    x2 = _out_proj(x1, mod3, lat_row, attn, hyn, g_out[l, :attn_w], wo, tm)

    x3 = _ffn(x2, mod3, lat_row, 2, g_norm[l, 2], w2u, w2d, tm)
    return x3.reshape(batch, seq, d)
```

```python
import functools
import math

import jax
import jax.numpy as jnp
import numpy as np
from jax import lax
from jax.experimental import pallas as pl
from jax.experimental.pallas import tpu as pltpu

F32 = jnp.float32
BF16 = jnp.bfloat16

HEAD_DIM = 128
N_KV_HEADS = 2
GRID_W = 64
ROPE_THETA = 10000.0
FILTER_BANDS = 16
RMS_EPS = 1e-6
N_MOD = 9

V7X_VMEM_LIMIT_BYTES = 56 * 1024 * 1024
SUBLANES = 8
DFT_N2 = 128

_HI = lax.Precision.HIGHEST


def _params(*sem):
    return pltpu.CompilerParams(dimension_semantics=sem, vmem_limit_bytes=V7X_VMEM_LIMIT_BYTES)


def _pick_tile(n, cap, mult):
    best = None
    t = mult
    while t <= min(n, cap):
        if n % t == 0:
            best = t
        t += mult
    assert best is not None, (n, cap, mult)
    return best


def _rms(x, eps=RMS_EPS):
    return x * lax.rsqrt(jnp.mean(x * x, axis=-1, keepdims=True) + eps)


def _ada_kernel(c_ref, w_ref, b_ref, o_ref):
    c = c_ref[...]
    s = c * jax.nn.sigmoid(c)
    o_ref[...] = jnp.dot(s, w_ref[...], preferred_element_type=F32, precision=_HI) + b_ref[...]


def _ada_mod(c_rows, w_ada, b_ada):
    rows, d = c_rows.shape
    n = w_ada.shape[1]
    tn = _pick_tile(n, 1024, 128)
    return pl.pallas_call(
        _ada_kernel,
        out_shape=jax.ShapeDtypeStruct((rows, n), F32),
        grid=(n // tn,),
        in_specs=[pl.BlockSpec((rows, d), lambda j: (0, 0)),
                  pl.BlockSpec((d, tn), lambda j: (0, j)),
                  pl.BlockSpec((1, tn), lambda j: (0, j))],
        out_specs=pl.BlockSpec((rows, tn), lambda j: (0, j)),
        compiler_params=_params("arbitrary"),
        name="ada_mod",
    )(c_rows, w_ada, b_ada.reshape(1, n))


def _ffn_kernel(x_ref, shift_ref, scale_ref, gate_ref, g_ref, wg_ref, wu_ref, wd_ref, o_ref,
                h_ref, acc_ref):
    k = pl.program_id(1)

    @pl.when(k == 0)
    def _():
        y = _rms(x_ref[...]) * g_ref[...]
        h_ref[...] = (y * (1.0 + scale_ref[0]) + shift_ref[0]).astype(BF16)
        acc_ref[...] = jnp.zeros_like(acc_ref)

    h = h_ref[...]
    g = jnp.dot(h, wg_ref[...], preferred_element_type=F32)
    u = jnp.dot(h, wu_ref[...], preferred_element_type=F32)
    a = (g * jax.nn.sigmoid(g) * u).astype(BF16)
    acc_ref[...] += jnp.dot(a, wd_ref[...], preferred_element_type=F32)

    @pl.when(k == pl.num_programs(1) - 1)
    def _():
        o_ref[...] = x_ref[...] + 0.5 * gate_ref[0] * acc_ref[...]


def _ffn(x, mod3, row_of_tile, slot, g, w_up, w_down, tm):
    t, d = x.shape
    f = w_down.shape[0]
    tf = _pick_tile(f, 512, 128)
    nf = f // tf
    mod_spec = lambda c: pl.BlockSpec((1, 1, d), lambda i, k: (row_of_tile(i), 0, 3 * slot + c))
    return pl.pallas_call(
        _ffn_kernel,
        out_shape=jax.ShapeDtypeStruct((t, d), F32),
        grid=(t // tm, nf),
        in_specs=[pl.BlockSpec((tm, d), lambda i, k: (i, 0)),
                  mod_spec(0), mod_spec(1), mod_spec(2),
                  pl.BlockSpec((1, d), lambda i, k: (0, 0)),
                  pl.BlockSpec((d, tf), lambda i, k: (0, k)),
                  pl.BlockSpec((d, tf), lambda i, k: (0, k + nf)),
                  pl.BlockSpec((tf, d), lambda i, k: (k, 0))],
        out_specs=pl.BlockSpec((tm, d), lambda i, k: (i, 0)),
        scratch_shapes=[pltpu.VMEM((tm, d), BF16), pltpu.VMEM((tm, d), F32)],
        compiler_params=_params("parallel", "arbitrary"),
        name="ffn",
    )(x, mod3, mod3, mod3, g.reshape(1, d), w_up, w_up, w_down)


def _rope(x, cos, sin):
    return x * cos + pltpu.roll(x, HEAD_DIM // 2, axis=1) * sin


def _in_proj_kernel(x_ref, shift_ref, scale_ref, g_ref, wa_ref, wh_ref, qn_ref, kn_ref, cos_ref, sin_ref,
                    q_ref, k_ref, v_ref, hy_ref, h_ref, *, q_scale):
    j = pl.program_id(1)
    attn_w = q_ref.shape[1]
    kvw = k_ref.shape[1]

    @pl.when(j == 0)
    def _():
        y = _rms(x_ref[...]) * g_ref[...]
        h = (y * (1.0 + scale_ref[0]) + shift_ref[0]).astype(BF16)
        h_ref[...] = h
        u = jnp.dot(h, wa_ref[...], preferred_element_type=F32)

        def normed(col, gain):
            return _rope(_rms(u[:, col:col + HEAD_DIM]) * gain, cos_ref[...], sin_ref[...])

        for hd in range(attn_w // HEAD_DIM):
            col = hd * HEAD_DIM
            q_ref[:, col:col + HEAD_DIM] = (normed(col, qn_ref[...]) * q_scale).astype(BF16)
        for hd in range(kvw // HEAD_DIM):
            col = hd * HEAD_DIM
            k_ref[:, col:col + HEAD_DIM] = normed(attn_w + col, kn_ref[...]).astype(BF16)
        v_ref[...] = u[:, attn_w + kvw:].astype(BF16)

    @pl.when(j > 0)
    def _():
        hy_ref[...] = jnp.dot(h_ref[...], wh_ref[...], preferred_element_type=F32)


def _in_proj(x, mod3, row_of_tile, g, w_qkv, w_hy, qn, kn, cos, sin, seq, tm):
    t, d = x.shape
    kvw = N_KV_HEADS * HEAD_DIM
    qkv_w = w_qkv.shape[1]
    attn_w = qkv_w - 2 * kvw
    hy_w = w_hy.shape[1]
    tn = _pick_tile(hy_w, qkv_w, 128)
    pos_tiles = seq // tm
    mod_spec = lambda c: pl.BlockSpec((1, 1, d), lambda i, j: (row_of_tile(i), 0, 3 + c))
    kern = functools.partial(_in_proj_kernel, q_scale=HEAD_DIM ** -0.5 * math.log2(math.e))
    return pl.pallas_call(
        kern,
        out_shape=(jax.ShapeDtypeStruct((t, attn_w), BF16),
                   jax.ShapeDtypeStruct((t, kvw), BF16),
                   jax.ShapeDtypeStruct((t, kvw), BF16),
                   jax.ShapeDtypeStruct((t, hy_w), F32)),
        grid=(t // tm, 1 + hy_w // tn),
        in_specs=[pl.BlockSpec((tm, d), lambda i, j: (i, 0)),
                  mod_spec(0), mod_spec(1),
                  pl.BlockSpec((1, d), lambda i, j: (0, 0)),
                  pl.BlockSpec((d, qkv_w), lambda i, j: (0, 0)),
                  pl.BlockSpec((d, tn), lambda i, j: (0, jnp.maximum(j - 1, 0))),
                  pl.BlockSpec((1, HEAD_DIM), lambda i, j: (0, 0)),
                  pl.BlockSpec((1, HEAD_DIM), lambda i, j: (0, 0)),
                  pl.BlockSpec((tm, HEAD_DIM), lambda i, j: (i % pos_tiles, 0)),
                  pl.BlockSpec((tm, HEAD_DIM), lambda i, j: (i % pos_tiles, 0))],
        out_specs=(pl.BlockSpec((tm, attn_w), lambda i, j: (i, 0)),
                   pl.BlockSpec((tm, kvw), lambda i, j: (i, 0)),
                   pl.BlockSpec((tm, kvw), lambda i, j: (i, 0)),
                   pl.BlockSpec((tm, tn), lambda i, j: (i, jnp.maximum(j - 1, 0)))),
        scratch_shapes=[pltpu.VMEM((tm, d), BF16)],
        compiler_params=_params("parallel", "arbitrary"),
        name="in_proj",
    )(x, mod3, mod3, g.reshape(1, d), w_qkv, w_hy, qn, kn, cos, sin)


def _ctx_kv_kernel(x_ref, shift_ref, scale_ref, g_ref, w_ref, kn_ref, k_ref, v_ref):
    y = _rms(x_ref[...]) * g_ref[...]
    h = (y * (1.0 + scale_ref[0]) + shift_ref[0]).astype(BF16)
    u = jnp.dot(h, w_ref[...], preferred_element_type=F32)
    for hd in range(N_KV_HEADS):
        k_ref[:, hd * HEAD_DIM:(hd + 1) * HEAD_DIM] = (
            _rms(u[:, hd * HEAD_DIM:(hd + 1) * HEAD_DIM]) * kn_ref[...]).astype(BF16)
    v_ref[...] = u[:, N_KV_HEADS * HEAD_DIM:].astype(BF16)


def _ctx_kv(x, mod3, ctx_row, g, w_kv, kn, tm):
    t, d = x.shape
    kvw = N_KV_HEADS * HEAD_DIM
    mod_spec = lambda c: pl.BlockSpec((1, 1, d), lambda i: (ctx_row, 0, 3 + c))
    return pl.pallas_call(
        _ctx_kv_kernel,
        out_shape=(jax.ShapeDtypeStruct((t, kvw), BF16), jax.ShapeDtypeStruct((t, kvw), BF16)),
        grid=(t // tm,),
        in_specs=[pl.BlockSpec((tm, d), lambda i: (i, 0)),
                  mod_spec(0), mod_spec(1),
                  pl.BlockSpec((1, d), lambda i: (0, 0)),
                  pl.BlockSpec((d, 2 * kvw), lambda i: (0, 0)),
                  pl.BlockSpec((1, HEAD_DIM), lambda i: (0, 0))],
        out_specs=(pl.BlockSpec((tm, kvw), lambda i: (i, 0)),
                   pl.BlockSpec((tm, kvw), lambda i: (i, 0))),
        compiler_params=_params("parallel"),
        name="ctx_kv",
    )(x, mod3, mod3, g.reshape(1, d), w_kv, kn)


def _attn_kernel(q_ref, k_ref, v_ref, o_ref, s_ref, mx_ref, ls_ref, acc_ref, *, group):
    tq = q_ref.shape[0]
    nk, _, tk = s_ref.shape
    lanes = HEAD_DIM
    q = jnp.concatenate([q_ref[:, h * HEAD_DIM:(h + 1) * HEAD_DIM] for h in range(group)], axis=0)

    mx_ref[...] = jnp.full_like(mx_ref, -jnp.inf)

    def scores(j, carry):
        start = pl.multiple_of(j * tk, tk)
        s = lax.dot_general(q, k_ref[0, pl.ds(start, tk), :], (((1,), (1,)), ((), ())),
                            preferred_element_type=F32)
        s_ref[j] = s
        mx = mx_ref[...]
        for c in range(tk // lanes):
            mx = jnp.maximum(mx, s[:, c * lanes:(c + 1) * lanes])
        mx_ref[...] = mx
        return carry

    lax.fori_loop(0, nk, scores, 0)
    mx_ref[...] = jnp.broadcast_to(jnp.max(mx_ref[...], axis=-1, keepdims=True), mx_ref.shape)

    ls_ref[...] = jnp.zeros_like(ls_ref)
    acc_ref[...] = jnp.zeros_like(acc_ref)

    def values(j, carry):
        start = pl.multiple_of(j * tk, tk)
        s = s_ref[j]
        m = mx_ref[...]
        ls = ls_ref[...]
        parts = []
        for c in range(tk // lanes):
            pc = jnp.exp2(s[:, c * lanes:(c + 1) * lanes] - m)
            ls = ls + pc
            parts.append(pc.astype(BF16))
        ls_ref[...] = ls
        p = jnp.concatenate(parts, axis=1)
        acc_ref[...] += jnp.dot(p, v_ref[0, pl.ds(start, tk), :], preferred_element_type=F32)
        return carry

    lax.fori_loop(0, nk, values, 0)
    o = acc_ref[...] / jnp.sum(ls_ref[...], axis=-1, keepdims=True)
    for h in range(group):
        o_ref[:, h * HEAD_DIM:(h + 1) * HEAD_DIM] = o[h * tq:(h + 1) * tq].astype(BF16)


def _attention(q, k_all, v_all, batch, seq):
    t, attn_w = q.shape
    lk = k_all.shape[1]
    group = attn_w // HEAD_DIM // N_KV_HEADS
    gw = group * HEAD_DIM
    tq = _pick_tile(seq, 128, 8)
    tk = _pick_tile(lk, 1408, 128)
    nq = seq // tq
    rows = group * tq
    kern = functools.partial(_attn_kernel, group=group)
    return pl.pallas_call(
        kern,
        out_shape=jax.ShapeDtypeStruct((t, attn_w), BF16),
        grid=(batch, N_KV_HEADS, nq),
        in_specs=[pl.BlockSpec((tq, gw), lambda b, g, i: (b * nq + i, g)),
                  pl.BlockSpec((1, lk, HEAD_DIM), lambda b, g, i: (b, 0, g)),
                  pl.BlockSpec((1, lk, HEAD_DIM), lambda b, g, i: (b, 0, g))],
        out_specs=pl.BlockSpec((tq, gw), lambda b, g, i: (b * nq + i, g)),
        scratch_shapes=[pltpu.VMEM((lk // tk, rows, tk), F32), pltpu.VMEM((rows, HEAD_DIM), F32),
                        pltpu.VMEM((rows, HEAD_DIM), F32), pltpu.VMEM((rows, HEAD_DIM), F32)],
        compiler_params=_params("parallel", "parallel", "arbitrary"),
        name="attention",
    )(q, k_all, v_all)


def _hy_pre_kernel(x0_ref, x1_ref, v_ref, x0p_ref, x1p_ref, vp_ref, x0n_ref, x1n_ref, vn_ref,
                   w_ref, b_ref, x0c_ref, vg_ref, *, tiles_per_seq):
    i = pl.program_id(0)
    tm = x0_ref.shape[0]
    first = (i % tiles_per_seq) == 0
    last = (i % tiles_per_seq) == tiles_per_seq - 1
    row = lax.broadcasted_iota(jnp.int32, x0_ref.shape, 0)

    def conv(u_ref, up_ref, un_ref, c):
        u = u_ref[...]
        prev_row = jnp.where(first, 0.0, up_ref[7:8, :])
        next_row = jnp.where(last, 0.0, un_ref[0:1, :])
        um = jnp.where(row == 0, prev_row, pltpu.roll(u, 1, axis=0))
        upl = jnp.where(row == tm - 1, next_row, pltpu.roll(u, tm - 1, axis=0))
        w = w_ref[c]
        return um * w[0:1, :] + u * w[1:2, :] + upl * w[2:3, :] + b_ref[c]

    x0c_ref[...] = conv(x0_ref, x0p_ref, x0n_ref, 0)
    vg_ref[...] = conv(v_ref, vp_ref, vn_ref, 2) * conv(x1_ref, x1p_ref, x1n_ref, 1)


def _hy_pre(hy, conv_w, conv_b, seq, tm):
    t, c3 = hy.shape
    c = c3 // 3
    cw = _pick_tile(c, 512, 128)
    ncw = c // cw
    tiles_per_seq = seq // tm
    rb = tm // 8
    nrb = t // 8
    w3 = conv_w.reshape(3, 3, c).transpose(1, 0, 2)
    b3 = conv_b.reshape(3, 1, c)
    main = lambda gidx: pl.BlockSpec((tm, cw), lambda i, j: (i, gidx * ncw + j))
    prev = lambda gidx: pl.BlockSpec((8, cw), lambda i, j: (jnp.maximum(i * rb - 1, 0), gidx * ncw + j))
    nxt = lambda gidx: pl.BlockSpec((8, cw), lambda i, j: (jnp.minimum((i + 1) * rb, nrb - 1), gidx * ncw + j))
    kern = functools.partial(_hy_pre_kernel, tiles_per_seq=tiles_per_seq)
    return pl.pallas_call(
        kern,
        out_shape=(jax.ShapeDtypeStruct((t, c), F32), jax.ShapeDtypeStruct((t, c), F32)),
        grid=(t // tm, ncw),
        in_specs=[main(0), main(1), main(2), prev(0), prev(1), prev(2), nxt(0), nxt(1), nxt(2),
                  pl.BlockSpec((3, 3, cw), lambda i, j: (0, 0, j)),
                  pl.BlockSpec((3, 1, cw), lambda i, j: (0, 0, j))],
        out_specs=(pl.BlockSpec((tm, cw), lambda i, j: (i, j)),
                   pl.BlockSpec((tm, cw), lambda i, j: (i, j))),
        compiler_params=_params("parallel", "parallel"),
        name="hyena_pre",
    )(hy, hy, hy, hy, hy, hy, hy, hy, hy, w3, b3)


def _dft_tables(n1, n2, k1p):
    n = n1 * n2
    k1 = np.arange(k1p)[:, None]
    a = np.arange(n1)[None, :]
    m2 = np.arange(n2)[:, None, None]
    theta = 2.0 * np.pi * ((((n2 * a * k1)[None] + m2 * k1[None]) % n) / n)
    k2 = np.arange(n2)[:, None]
    b = np.arange(n2)[None, :]
    phi = 2.0 * np.pi * ((k2 * b) % n2) / n2
    c, s = np.cos(phi), np.sin(phi)
    fmat = np.block([[c, s], [-s, c]])
    return np.cos(theta), np.sin(theta), fmat


def _mix_forward(cos, sin, n1h):
    n2, k1p, _ = cos.shape
    g = n2 // SUBLANES
    a = np.stack([cos[:, :, :n1h], -sin[:, :, :n1h]], axis=1).reshape(g, SUBLANES, 2, k1p, n1h)
    m = np.zeros((g, 2, k1p, SUBLANES, n1h, SUBLANES))
    for j in range(SUBLANES):
        m[:, :, :, j, :, j] = a[:, j]
    return m.reshape(g, 2 * k1p * SUBLANES, n1h * SUBLANES)


def _mix_inverse(cos, sin, n1h, n1):
    n2, k1p, _ = cos.shape
    g = n2 // SUBLANES
    k1 = np.arange(k1p)
    weight = np.where((k1 == 0) | (k1 == n1 // 2), 1.0, np.where(k1 < n1 // 2, 2.0, 0.0)) / (n1 * n2)
    a = np.stack([cos[:, :, :n1h], -sin[:, :, :n1h]], axis=1) * weight[None, None, :, None]
    a = a.reshape(g, SUBLANES, 2, k1p, n1h)
    m = np.zeros((g, n1h, SUBLANES, 2, k1p, SUBLANES))
    for j in range(SUBLANES):
        m[:, :, j, :, :, j] = a[:, j].transpose(0, 3, 1, 2)
    return m.reshape(g, n1h * SUBLANES, 2 * k1p * SUBLANES)


def _stage_a_kernel(m_ref, x_ref, yr_ref, yi_ref):
    _, n1h, sub, c = x_ref.shape
    k1p = yr_ref.shape[1]
    x = x_ref[0].reshape(n1h * sub, c).astype(BF16)
    y = jnp.dot(m_ref[0], x, preferred_element_type=F32)
    yr_ref[0] = y[:k1p * sub].reshape(k1p, sub, c)
    yi_ref[0] = y[k1p * sub:].reshape(k1p, sub, c)


def _stage_a(mix, vg4, k1p):
    b, n1h, n2, c = vg4.shape
    groups = n2 // SUBLANES
    return pl.pallas_call(
        _stage_a_kernel,
        out_shape=(jax.ShapeDtypeStruct((b, k1p, n2, c), F32),) * 2,
        grid=(groups, b),
        in_specs=[pl.BlockSpec((1,) + mix.shape[1:], lambda g, bi: (g, 0, 0)),
                  pl.BlockSpec((1, n1h, SUBLANES, c), lambda g, bi: (bi, 0, g, 0))],
        out_specs=(pl.BlockSpec((1, k1p, SUBLANES, c), lambda g, bi: (bi, 0, g, 0)),) * 2,
        compiler_params=_params("parallel", "parallel"),
        name="hyena_dft_a",
    )(mix, vg4)


def _stage_b_kernel(yr_ref, yi_ref, f_ref, ft_ref, kr_ref, ki_ref, ur_ref, ui_ref):
    kb, n2 = yr_ref.shape[1:3]
    for r in range(kb):
        y = jnp.concatenate([yr_ref[0, r], yi_ref[0, r]], axis=0).astype(BF16)
        z = jnp.dot(f_ref[...], y, preferred_element_type=F32)
        zr, zi = z[:n2], z[n2:]
        kr = kr_ref[r].astype(F32)
        ki = ki_ref[r].astype(F32)
        p = jnp.concatenate([zr * kr - zi * ki, zr * ki + zi * kr], axis=0).astype(BF16)
        u = jnp.dot(ft_ref[...], p, preferred_element_type=F32)
        ur_ref[0, r] = u[:n2]
        ui_ref[0, r] = u[n2:]


def _stage_b(yr, yi, fmat, fmat_t, kfr, kfi, kb):
    b, k1p, n2, c = yr.shape
    blk = pl.BlockSpec((1, kb, n2, c), lambda bi, k: (bi, k, 0, 0))
    flt = pl.BlockSpec((kb, n2, c), lambda bi, k: (k, 0, 0))
    mat = pl.BlockSpec((2 * n2, 2 * n2), lambda bi, k: (0, 0))
    return pl.pallas_call(
        _stage_b_kernel,
        out_shape=(jax.ShapeDtypeStruct((b, k1p, n2, c), F32),) * 2,
        grid=(b, k1p // kb),
        in_specs=[blk, blk, mat, mat, flt, flt],
        out_specs=(blk, blk),
        compiler_params=_params("parallel", "parallel"),
        name="hyena_dft_b",
    )(yr, yi, fmat, fmat_t, kfr, kfi)


def _stage_c_kernel(m_ref, ur_ref, ui_ref, vg_ref, x0_ref, hb_ref, g_ref, o_ref):
    _, k1p, sub, c = ur_ref.shape
    n1h = vg_ref.shape[1]
    u = jnp.concatenate([ur_ref[0].reshape(k1p * sub, c), ui_ref[0].reshape(k1p * sub, c)], axis=0)
    conv = jnp.dot(m_ref[0], u.astype(BF16), preferred_element_type=F32)
    vg = vg_ref[0].reshape(n1h * sub, c)
    x0 = x0_ref[0].reshape(n1h * sub, c)
    hyo = (conv + hb_ref[...] * vg) * x0
    o_ref[0] = (_rms(hyo) * g_ref[...]).reshape(n1h, sub, c)


def _stage_c(mix, ur, ui, vg4, x04, hy_bias, g_hy):
    b, n1h, n2, c = vg4.shape
    k1p = ur.shape[1]
    groups = n2 // SUBLANES
    ublk = pl.BlockSpec((1, k1p, SUBLANES, c), lambda g, bi: (bi, 0, g, 0))
    xblk = pl.BlockSpec((1, n1h, SUBLANES, c), lambda g, bi: (bi, 0, g, 0))
    vec = pl.BlockSpec((1, c), lambda g, bi: (0, 0))
    return pl.pallas_call(
        _stage_c_kernel,
        out_shape=jax.ShapeDtypeStruct((b, n1h, n2, c), F32),
        grid=(groups, b),
        in_specs=[pl.BlockSpec((1,) + mix.shape[1:], lambda g, bi: (g, 0, 0)),
                  ublk, ublk, xblk, xblk, vec, vec],
        out_specs=xblk,
        compiler_params=_params("parallel", "parallel"),
        name="hyena_dft_c",
    )(mix, ur, ui, vg4, x04, hy_bias.reshape(1, c), g_hy.reshape(1, c))


def _filter_a_kernel(a_ref, w1t_ref, w1c_ref, w1s_ref, b1_ref, w2_ref, b2_ref, w3_ref, b3_ref,
                     w4_ref, fr_ref, dec_ref, yr_ref, yi_ref, norm_ref, *, seq):
    g = pl.program_id(0)
    k1p = yr_ref.shape[0]
    c = norm_ref.shape[1]
    n1 = a_ref.shape[2]
    n1h = n1 // 2
    n2 = DFT_N2
    row = lax.broadcasted_iota(jnp.int32, (n1, 1), 0)
    fwd = row < n1h
    band = lax.broadcasted_iota(jnp.int32, (1, FILTER_BANDS), 1).astype(F32)
    f = 1e-4 + band * ((FILTER_BANDS - 1 - 1e-4) / (FILTER_BANDS - 1))
    fr = fr_ref[...]
    dec = jnp.abs(dec_ref[...])
    dot = functools.partial(jnp.dot, preferred_element_type=F32, precision=_HI)
    total = jnp.zeros((1, c), F32)
    for j in range(SUBLANES):
        tprime = row * n2 + (g * SUBLANES + j)
        pos = jnp.where(fwd, tprime, 2 * seq - tprime).astype(F32)
        t = pos / (seq - 1.0)
        ang = f * ((2.0 * math.pi * pos) / seq)
        h = t * w1t_ref[...] + dot(jnp.cos(ang), w1c_ref[...]) + dot(-jnp.sin(ang), w1s_ref[...])
        h = jnp.sin(fr * (h + b1_ref[...]))
        h = jnp.sin(fr * (dot(h, w2_ref[...]) + b2_ref[...]))
        h = jnp.sin(fr * (dot(h, w3_ref[...]) + b3_ref[...]))
        h = dot(h, w4_ref[...])
        kf = h[:, :c] * jnp.exp(-t * dec[0:1, :])
        kb = h[:, c:] * jnp.exp(-t * dec[1:2, :])
        kk = jnp.where(fwd, kf, jnp.where(tprime == seq, 0.0, kb))
        total = total + jnp.sum(jnp.abs(kk), axis=0, keepdims=True)
        y = jnp.dot(a_ref[j], kk.astype(BF16), preferred_element_type=F32)
        yr_ref[:, j * c:(j + 1) * c] = y[:k1p].astype(BF16)
        yi_ref[:, j * c:(j + 1) * c] = y[k1p:].astype(BF16)

    @pl.when(g == 0)
    def _():
        norm_ref[...] = jnp.zeros_like(norm_ref)

    norm_ref[...] += total


def _filter_a(a_full, flt, seq, c):
    w1, b1, w2, b2, w3, b3, w4, freq, decay = flt
    hid = w2.shape[0]
    n2, k1p2, n1 = a_full.shape
    k1p = k1p2 // 2
    full = lambda shape: pl.BlockSpec(shape, lambda g: (0,) * len(shape))
    kern = functools.partial(_filter_a_kernel, seq=seq)
    return pl.pallas_call(
        kern,
        out_shape=(jax.ShapeDtypeStruct((k1p, n2 * c), BF16),
                   jax.ShapeDtypeStruct((k1p, n2 * c), BF16),
                   jax.ShapeDtypeStruct((1, c), F32)),
        grid=(n2 // SUBLANES,),
        in_specs=[pl.BlockSpec((SUBLANES, k1p2, n1), lambda g: (g, 0, 0)),
                  full((1, hid)), full((FILTER_BANDS, hid)), full((FILTER_BANDS, hid)), full((1, hid)),
                  full((hid, hid)), full((1, hid)), full((hid, hid)), full((1, hid)),
                  full((hid, 2 * c)), full((1, hid)), full((2, c))],
        out_specs=(pl.BlockSpec((k1p, SUBLANES * c), lambda g: (0, g)),
                   pl.BlockSpec((k1p, SUBLANES * c), lambda g: (0, g)),
                   pl.BlockSpec((1, c), lambda g: (0, 0))),
        compiler_params=_params("arbitrary"),
        name="hyena_filter_a",
    )(a_full, w1[0:1], w1[1:1 + FILTER_BANDS], w1[1 + FILTER_BANDS:], b1.reshape(1, hid),
      w2, b2.reshape(1, hid), w3, b3.reshape(1, hid), w4, freq.reshape(1, hid), decay)


def _filter_b_kernel(yr_ref, yi_ref, f_ref, norm_ref, kr_ref, ki_ref):
    kb, n2 = yr_ref.shape[:2]
    for r in range(kb):
        y = jnp.concatenate([yr_ref[r], yi_ref[r]], axis=0)
        z = jnp.dot(f_ref[...], y, preferred_element_type=F32) / norm_ref[...]
        kr_ref[r] = z[:n2].astype(BF16)
        ki_ref[r] = z[n2:].astype(BF16)


def _filter_b(yr, yi, fmat, norm, kb):
    k1p, n2, c = yr.shape
    blk = pl.BlockSpec((kb, n2, c), lambda k: (k, 0, 0))
    return pl.pallas_call(
        _filter_b_kernel,
        out_shape=(jax.ShapeDtypeStruct((k1p, n2, c), BF16),) * 2,
        grid=(k1p // kb,),
        in_specs=[blk, blk,
                  pl.BlockSpec((2 * n2, 2 * n2), lambda k: (0, 0)),
                  pl.BlockSpec((1, c), lambda k: (0, 0))],
        out_specs=(blk, blk),
        compiler_params=_params("parallel"),
        name="hyena_filter_b",
    )(yr, yi, fmat, norm)


def _hyena(hy, flt, conv_w, conv_b, hy_bias, g_hy, batch, seq, tm):
    c = hy.shape[1] // 3
    n2 = DFT_N2
    n1h = seq // n2
    n1 = 2 * n1h
    k1p = min(n1, -(-(n1 // 2 + 1) // SUBLANES) * SUBLANES)
    kb = _pick_tile(k1p, 4, 1)
    cos, sin, fmat_np = _dft_tables(n1, n2, k1p)
    a_full = jnp.asarray(np.concatenate([cos, -sin], axis=1), BF16)
    mix_a = jnp.asarray(_mix_forward(cos, sin, n1h), BF16)
    mix_c = jnp.asarray(_mix_inverse(cos, sin, n1h, n1), BF16)
    fmat = jnp.asarray(fmat_np, BF16)
    fmat_t = jnp.asarray(fmat_np.T, BF16)

    fyr, fyi, norm = _filter_a(a_full, flt, seq, c)
    kfr, kfi = _filter_b(fyr.reshape(k1p, n2, c), fyi.reshape(k1p, n2, c), fmat, norm, kb)

    x0c, vg = _hy_pre(hy, conv_w, conv_b, seq, tm)
    vg4 = vg.reshape(batch, n1h, n2, c)
    x04 = x0c.reshape(batch, n1h, n2, c)
    yr, yi = _stage_a(mix_a, vg4, k1p)
    ur, ui = _stage_b(yr, yi, fmat, fmat_t, kfr, kfi, kb)
    y = _stage_c(mix_c, ur, ui, vg4, x04, hy_bias, g_hy)
    return y.reshape(batch * seq, c)


def _out_proj_kernel(x_ref, gate_ref, attn_ref, hy_ref, ga_ref, wa_ref, wh_ref, o_ref):
    ya = (_rms(attn_ref[...].astype(F32)) * ga_ref[...]).astype(BF16)
    y = jnp.dot(ya, wa_ref[...], preferred_element_type=F32)
    y += jnp.dot(hy_ref[...].astype(BF16), wh_ref[...], preferred_element_type=F32)
    o_ref[...] = x_ref[...] + gate_ref[0] * y


def _out_proj(x, mod3, row_of_tile, attn, hyn, g_attn, w_out, tm):
    t, d = x.shape
    aw = attn.shape[1]
    hw = hyn.shape[1]
    return pl.pallas_call(
        _out_proj_kernel,
        out_shape=jax.ShapeDtypeStruct((t, d), F32),
        grid=(t // tm,),
        in_specs=[pl.BlockSpec((tm, d), lambda i: (i, 0)),
                  pl.BlockSpec((1, 1, d), lambda i: (row_of_tile(i), 0, 5)),
                  pl.BlockSpec((tm, aw), lambda i: (i, 0)),
                  pl.BlockSpec((tm, hw), lambda i: (i, 0)),
                  pl.BlockSpec((1, aw), lambda i: (0, 0)),
                  pl.BlockSpec((aw, d), lambda i: (0, 0)),
                  pl.BlockSpec((hw, d), lambda i: (1, 0))],
        out_specs=pl.BlockSpec((tm, d), lambda i: (i, 0)),
        compiler_params=_params("parallel"),
        name="out_proj",
    )(x, mod3, attn, hyn, g_attn.reshape(1, aw), w_out, w_out)


def _rope_tables(seq):
    half = HEAD_DIM // 2
    pos = np.arange(seq)
    inv = ROPE_THETA ** (-np.arange(0, half, 2, dtype=np.float64) / half)
    ang = np.concatenate([(pos // GRID_W)[:, None] * inv, (pos % GRID_W)[:, None] * inv], axis=-1)
    cos = np.concatenate([np.cos(ang), np.cos(ang)], axis=-1)
    sin = np.concatenate([-np.sin(ang), np.sin(ang)], axis=-1)
    return jnp.asarray(cos, F32), jnp.asarray(sin, F32)


def _deinterleave_cols(w, heads):
    lead = w.shape[:-1]
    return w.reshape(*lead, heads, HEAD_DIM // 2, 2).swapaxes(-1, -2).reshape(*lead, heads * HEAD_DIM)


def kernel(x, c, ctx, c_ctx, w_ada, b_ada, g_norm, w_ffn1_up, w_ffn1_down, w_ffn2_up, w_ffn2_down,
           w_in, q_norm, k_norm, conv_w, conv_b, flt_w1, flt_b1, flt_w2, flt_b2, flt_w3, flt_b3,
           flt_w4, flt_freq, flt_decay, hy_bias, g_out, w_out):
    batch, seq, d = x.shape
    lc = ctx.shape[1]
    depth = w_ada.shape[0]
    assert depth == 1, "context-update path of deeper stacks is not implemented"
    attn_w = d // 2
    kvw = N_KV_HEADS * HEAD_DIM
    q_heads = attn_w // HEAD_DIM
    tm = _pick_tile(seq, 512, 128)
    tmc = _pick_tile(lc, 512, 8)
    tiles_per_seq = seq // tm
    lat_row = lambda i: i // tiles_per_seq
    ctx_row = lambda i: batch

    xs = x.reshape(batch * seq, d)
    cs = ctx.reshape(batch * lc, d)
    rows = -(-(batch + 1) // 8) * 8
    c_rows = jnp.zeros((rows, d), F32).at[:batch].set(c).at[batch].set(c_ctx)
    cos, sin = _rope_tables(seq)

    l = 0
    mod3 = _ada_mod(c_rows, w_ada[l], b_ada[l]).reshape(rows, 1, N_MOD * d)

    w1u, w1d = w_ffn1_up[l].astype(BF16), w_ffn1_down[l].astype(BF16)
    w2u, w2d = w_ffn2_up[l].astype(BF16), w_ffn2_down[l].astype(BF16)
    wi = w_in[l]
    k_end = attn_w + kvw
    wi = jnp.concatenate([_deinterleave_cols(wi[:, :k_end], q_heads + N_KV_HEADS), wi[:, k_end:]],
                         axis=1).astype(BF16)
    qn = _deinterleave_cols(q_norm[l], 1).reshape(1, HEAD_DIM)
    kn = _deinterleave_cols(k_norm[l], 1).reshape(1, HEAD_DIM)
    wo = w_out[l].astype(BF16)

    x1 = _ffn(xs, mod3, lat_row, 0, g_norm[l, 0], w1u, w1d, tm)
    c1 = _ffn(cs, mod3, ctx_row, 0, g_norm[l, 0], w1u, w1d, tmc)

    qkv_w = attn_w + 2 * kvw
    q, k, v, hy = _in_proj(x1, mod3, lat_row, g_norm[l, 1], wi[:, :qkv_w], wi[:, qkv_w:], qn, kn,
                           cos, sin, seq, tm)
    kc, vc = _ctx_kv(c1, mod3, batch, g_norm[l, 1], wi[:, attn_w:qkv_w], kn, tmc)
    k_all = jnp.concatenate([kc.reshape(batch, lc, kvw), k.reshape(batch, seq, kvw)], axis=1)
    v_all = jnp.concatenate([vc.reshape(batch, lc, kvw), v.reshape(batch, seq, kvw)], axis=1)
    attn = _attention(q, k_all, v_all, batch, seq)

    flt = (flt_w1[l], flt_b1[l], flt_w2[l], flt_b2[l], flt_w3[l], flt_b3[l], flt_w4[l],
           flt_freq[l], flt_decay[l])
    hyn = _hyena(hy, flt, conv_w[l], conv_b[l], hy_bias[l], g_out[l, attn_w:], batch, seq, tm)
    x2 = _out_proj(x1, mod3, lat_row, attn, hyn, g_out[l, :attn_w], wo, tm)

    x3 = _ffn(x2, mod3, lat_row, 2, g_norm[l, 2], w2u, w2d, tm)
    return x3.reshape(batch, seq, d)
```

```python
import functools
import math

import jax
import jax.numpy as jnp
import numpy as np
from jax import lax
from jax.experimental import pallas as pl
from jax.experimental.pallas import tpu as pltpu

F32 = jnp.float32
BF16 = jnp.bfloat16

HEAD_DIM = 128
N_KV_HEADS = 2
GRID_W = 64
ROPE_THETA = 10000.0
FILTER_BANDS = 16
RMS_EPS = 1e-6
N_MOD = 9

V7X_VMEM_LIMIT_BYTES = 56 * 1024 * 1024
SUBLANES = 8
DFT_N2 = 128

_HI = lax.Precision.HIGHEST


def _params(*sem):
    return pltpu.CompilerParams(dimension_semantics=sem, vmem_limit_bytes=V7X_VMEM_LIMIT_BYTES)


def _pick_tile(n, cap, mult):
    best = None
    t = mult
    while t <= min(n, cap):
        if n % t == 0:
            best = t
        t += mult
    assert best is not None, (n, cap, mult)
    return best


def _rms(x, eps=RMS_EPS):
    return x * lax.rsqrt(jnp.mean(x * x, axis=-1, keepdims=True) + eps)


def _ada_kernel(c_ref, w_ref, b_ref, o_ref):
    c = c_ref[...]
    s = c * jax.nn.sigmoid(c)
    o_ref[...] = jnp.dot(s, w_ref[...], preferred_element_type=F32, precision=_HI) + b_ref[...]


def _ada_mod(c_rows, w_ada, b_ada):
    rows, d = c_rows.shape
    n = w_ada.shape[1]
    tn = _pick_tile(n, 2048, 128)
    return pl.pallas_call(
        _ada_kernel,
        out_shape=jax.ShapeDtypeStruct((rows, n), F32),
        grid=(n // tn,),
        in_specs=[pl.BlockSpec((rows, d), lambda j: (0, 0)),
                  pl.BlockSpec((d, tn), lambda j: (0, j)),
                  pl.BlockSpec((1, tn), lambda j: (0, j))],
        out_specs=pl.BlockSpec((rows, tn), lambda j: (0, j)),
        compiler_params=_params("arbitrary"),
        name="ada_mod",
    )(c_rows, w_ada, b_ada.reshape(1, n))


def _ffn_kernel(x_ref, shift_ref, scale_ref, gate_ref, g_ref, wg_ref, wu_ref, wd_ref, o_ref,
                h_ref, acc_ref):
    k = pl.program_id(1)

    @pl.when(k == 0)
    def _():
        y = _rms(x_ref[...]) * g_ref[...]
        h_ref[...] = (y * (1.0 + scale_ref[0]) + shift_ref[0]).astype(BF16)
        acc_ref[...] = jnp.zeros_like(acc_ref)

    h = h_ref[...]
    g = jnp.dot(h, wg_ref[...], preferred_element_type=F32)
    u = jnp.dot(h, wu_ref[...], preferred_element_type=F32)
    a = (g * jax.nn.sigmoid(g) * u).astype(BF16)
    acc_ref[...] += jnp.dot(a, wd_ref[...], preferred_element_type=F32)

    @pl.when(k == pl.num_programs(1) - 1)
    def _():
        o_ref[...] = x_ref[...] + 0.5 * gate_ref[0] * acc_ref[...]


def _ffn(x, mod3, row_of_tile, slot, g, w_up, w_down, tm):
    t, d = x.shape
    f = w_down.shape[0]
    tf = _pick_tile(f, 512, 128)
    nf = f // tf
    mod_spec = lambda c: pl.BlockSpec((1, 1, d), lambda i, k: (row_of_tile(i), 0, 3 * slot + c))
    return pl.pallas_call(
        _ffn_kernel,
        out_shape=jax.ShapeDtypeStruct((t, d), F32),
        grid=(t // tm, nf),
        in_specs=[pl.BlockSpec((tm, d), lambda i, k: (i, 0)),
                  mod_spec(0), mod_spec(1), mod_spec(2),
                  pl.BlockSpec((1, d), lambda i, k: (0, 0)),
                  pl.BlockSpec((d, tf), lambda i, k: (0, k)),
                  pl.BlockSpec((d, tf), lambda i, k: (0, k + nf)),
                  pl.BlockSpec((tf, d), lambda i, k: (k, 0))],
        out_specs=pl.BlockSpec((tm, d), lambda i, k: (i, 0)),
        scratch_shapes=[pltpu.VMEM((tm, d), BF16), pltpu.VMEM((tm, d), F32)],
        compiler_params=_params("parallel", "arbitrary"),
        name="ffn",
    )(x, mod3, mod3, mod3, g.reshape(1, d), w_up, w_up, w_down)


def _rope(x, cos, sin):
    return x * cos + pltpu.roll(x, HEAD_DIM // 2, axis=1) * sin


HALO = 16


def _in_proj_kernel(x_ref, xp_ref, xn_ref, shift_ref, scale_ref, g_ref, wa_ref, wh_ref, qn_ref, kn_ref,
                    cos_ref, sin_ref, cw_ref, cb_ref, q_ref, k_ref, v_ref, x0c_ref, vg_ref, *,
                    q_scale, tiles_per_seq, cw):
    i = pl.program_id(0)
    tm = x_ref.shape[0]
    attn_w = q_ref.shape[1]
    kvw = k_ref.shape[1]
    c = x0c_ref.shape[1]
    first = (i % tiles_per_seq) == 0
    last = (i % tiles_per_seq) == tiles_per_seq - 1

    x_ext = jnp.concatenate([xp_ref[...], x_ref[...], xn_ref[...]], axis=0)
    y = _rms(x_ext) * g_ref[...]
    h_ext = (y * (1.0 + scale_ref[0]) + shift_ref[0]).astype(BF16)
    h = h_ext[HALO:HALO + tm]

    u = jnp.dot(h, wa_ref[...], preferred_element_type=F32)

    def normed(col, gain):
        return _rope(_rms(u[:, col:col + HEAD_DIM]) * gain, cos_ref[...], sin_ref[...])

    for hd in range(attn_w // HEAD_DIM):
        col = hd * HEAD_DIM
        q_ref[:, col:col + HEAD_DIM] = (normed(col, qn_ref[...]) * q_scale).astype(BF16)
    for hd in range(kvw // HEAD_DIM):
        col = hd * HEAD_DIM
        k_ref[:, col:col + HEAD_DIM] = normed(attn_w + col, kn_ref[...]).astype(BF16)
    v_ref[...] = u[:, attn_w + kvw:].astype(BF16)

    row = lax.broadcasted_iota(jnp.int32, (tm, cw), 0)
    at_start = jnp.logical_and(first, row == 0)
    at_end = jnp.logical_and(last, row == tm - 1)
    ext = tm + 2 * HALO

    def conv(group, c0):
        ue = jnp.dot(h_ext, wh_ref[:, group * c + c0:group * c + c0 + cw], preferred_element_type=F32)
        um = jnp.where(at_start, 0.0, pltpu.roll(ue, 1, axis=0)[HALO:HALO + tm])
        up = jnp.where(at_end, 0.0, pltpu.roll(ue, ext - 1, axis=0)[HALO:HALO + tm])
        w = cw_ref[group][:, c0:c0 + cw]
        return um * w[0:1] + ue[HALO:HALO + tm] * w[1:2] + up * w[2:3] + cb_ref[group][:, c0:c0 + cw]

    for c0 in range(0, c, cw):
        x0c_ref[:, c0:c0 + cw] = conv(0, c0)
        vg_ref[:, c0:c0 + cw] = conv(2, c0) * conv(1, c0)


def _in_proj(x, mod3, row_of_tile, g, w_qkv, w_hy, qn, kn, cos, sin, conv_w, conv_b, seq, tm):
    t, d = x.shape
    kvw = N_KV_HEADS * HEAD_DIM
    qkv_w = w_qkv.shape[1]
    attn_w = qkv_w - 2 * kvw
    c = w_hy.shape[1] // 3
    cw = _pick_tile(c, 512, 128)
    pos_tiles = seq // tm
    hb = tm // HALO
    nhb = t // HALO
    w3 = conv_w.reshape(3, 3, c).transpose(1, 0, 2)
    b3 = conv_b.reshape(3, 1, c)
    mod_spec = lambda ch: pl.BlockSpec((1, 1, d), lambda i: (row_of_tile(i), 0, 3 + ch))
    const = lambda shape: pl.BlockSpec(shape, lambda i: (0,) * len(shape), pipeline_mode=pl.Buffered(1))
    kern = functools.partial(_in_proj_kernel, q_scale=HEAD_DIM ** -0.5 * math.log2(math.e),
                             tiles_per_seq=pos_tiles, cw=cw)
    return pl.pallas_call(
        kern,
        out_shape=(jax.ShapeDtypeStruct((t, attn_w), BF16),
                   jax.ShapeDtypeStruct((t, kvw), BF16),
                   jax.ShapeDtypeStruct((t, kvw), BF16),
                   jax.ShapeDtypeStruct((t, c), F32),
                   jax.ShapeDtypeStruct((t, c), F32)),
        grid=(t // tm,),
        in_specs=[pl.BlockSpec((tm, d), lambda i: (i, 0)),
                  pl.BlockSpec((HALO, d), lambda i: (jnp.maximum(i * hb - 1, 0), 0)),
                  pl.BlockSpec((HALO, d), lambda i: (jnp.minimum((i + 1) * hb, nhb - 1), 0)),
                  mod_spec(0), mod_spec(1),
                  const((1, d)), const((d, qkv_w)), const((d, 3 * c)),
                  const((1, HEAD_DIM)), const((1, HEAD_DIM)),
                  pl.BlockSpec((tm, HEAD_DIM), lambda i: (i % pos_tiles, 0)),
                  pl.BlockSpec((tm, HEAD_DIM), lambda i: (i % pos_tiles, 0)),
                  const((3, 3, c)), const((3, 1, c))],
        out_specs=(pl.BlockSpec((tm, attn_w), lambda i: (i, 0)),
                   pl.BlockSpec((tm, kvw), lambda i: (i, 0)),
                   pl.BlockSpec((tm, kvw), lambda i: (i, 0)),
                   pl.BlockSpec((tm, c), lambda i: (i, 0)),
                   pl.BlockSpec((tm, c), lambda i: (i, 0))),
        compiler_params=_params("parallel"),
        name="in_proj",
    )(x, x, x, mod3, mod3, g.reshape(1, d), w_qkv, w_hy, qn, kn, cos, sin, w3, b3)


def _ctx_kv_kernel(x_ref, shift_ref, scale_ref, g_ref, w_ref, kn_ref, k_ref, v_ref):
    y = _rms(x_ref[...]) * g_ref[...]
    h = (y * (1.0 + scale_ref[0]) + shift_ref[0]).astype(BF16)
    u = jnp.dot(h, w_ref[...], preferred_element_type=F32)
    for hd in range(N_KV_HEADS):
        k_ref[:, hd * HEAD_DIM:(hd + 1) * HEAD_DIM] = (
            _rms(u[:, hd * HEAD_DIM:(hd + 1) * HEAD_DIM]) * kn_ref[...]).astype(BF16)
    v_ref[...] = u[:, N_KV_HEADS * HEAD_DIM:].astype(BF16)


def _ctx_kv(x, mod3, ctx_row, g, w_kv, kn, tm):
    t, d = x.shape
    kvw = N_KV_HEADS * HEAD_DIM
    mod_spec = lambda c: pl.BlockSpec((1, 1, d), lambda i: (ctx_row, 0, 3 + c))
    return pl.pallas_call(
        _ctx_kv_kernel,
        out_shape=(jax.ShapeDtypeStruct((t, kvw), BF16), jax.ShapeDtypeStruct((t, kvw), BF16)),
        grid=(t // tm,),
        in_specs=[pl.BlockSpec((tm, d), lambda i: (i, 0)),
                  mod_spec(0), mod_spec(1),
                  pl.BlockSpec((1, d), lambda i: (0, 0)),
                  pl.BlockSpec((d, 2 * kvw), lambda i: (0, 0)),
                  pl.BlockSpec((1, HEAD_DIM), lambda i: (0, 0))],
        out_specs=(pl.BlockSpec((tm, kvw), lambda i: (i, 0)),
                   pl.BlockSpec((tm, kvw), lambda i: (i, 0))),
        compiler_params=_params("parallel"),
        name="ctx_kv",
    )(x, mod3, mod3, g.reshape(1, d), w_kv, kn)


def _attn_kernel(q_ref, k_ref, v_ref, o_ref, s0_ref, s1_ref, mx0_ref, mx1_ref, ls_ref, acc_ref, *,
                 group, nq):
    i = pl.program_id(2)
    tq = q_ref.shape[0]
    nk, _, tk = s0_ref.shape
    lanes = HEAD_DIM
    bufs = ((s0_ref, mx0_ref), (s1_ref, mx1_ref))

    def scores(j, q, s_ref, mx_ref):
        start = pl.multiple_of(j * tk, tk)
        s = lax.dot_general(q, k_ref[0, pl.ds(start, tk), :], (((1,), (1,)), ((), ())),
                            preferred_element_type=F32)
        s_ref[j] = s
        mx = mx_ref[...]
        for c in range(tk // lanes):
            mx = jnp.maximum(mx, s[:, c * lanes:(c + 1) * lanes])
        mx_ref[...] = mx

    def values(j, s_ref, mx_ref):
        start = pl.multiple_of(j * tk, tk)
        s = s_ref[j]
        m = mx_ref[...]
        ls = ls_ref[...]
        parts = []
        for c in range(tk // lanes):
            pc = jnp.exp2(s[:, c * lanes:(c + 1) * lanes] - m)
            ls = ls + pc
            parts.append(pc.astype(BF16))
        ls_ref[...] = ls
        p = jnp.concatenate(parts, axis=1)
        acc_ref[...] += jnp.dot(p, v_ref[0, pl.ds(start, tk), :], preferred_element_type=F32)

    def begin_scores(mx_ref):
        mx_ref[...] = jnp.full_like(mx_ref, -jnp.inf)
        return jnp.concatenate([q_ref[:, h * HEAD_DIM:(h + 1) * HEAD_DIM] for h in range(group)], axis=0)

    def end_scores(mx_ref):
        mx_ref[...] = jnp.broadcast_to(jnp.max(mx_ref[...], axis=-1, keepdims=True), mx_ref.shape)

    def begin_values():
        ls_ref[...] = jnp.zeros_like(ls_ref)
        acc_ref[...] = jnp.zeros_like(acc_ref)

    def end_values():
        o = acc_ref[...] / jnp.sum(ls_ref[...], axis=-1, keepdims=True)
        for h in range(group):
            o_ref[:, h * HEAD_DIM:(h + 1) * HEAD_DIM] = o[h * tq:(h + 1) * tq].astype(BF16)

    @pl.when(i == 0)
    def _():
        s_ref, mx_ref = bufs[0]
        q = begin_scores(mx_ref)
        lax.fori_loop(0, nk, lambda j, c: (scores(j, q, s_ref, mx_ref), c)[1], 0)
        end_scores(mx_ref)

    for parity in range(2):
        @pl.when(jnp.logical_and(jnp.logical_and(i > 0, i < nq), i % 2 == parity))
        def _():
            (cs_ref, cmx_ref), (ps_ref, pmx_ref) = bufs[parity], bufs[1 - parity]
            q = begin_scores(cmx_ref)
            begin_values()

            def both(j, c):
                scores(j, q, cs_ref, cmx_ref)
                values(j, ps_ref, pmx_ref)
                return c

            lax.fori_loop(0, nk, both, 0)
            end_scores(cmx_ref)
            end_values()

    @pl.when(i == nq)
    def _():
        s_ref, mx_ref = bufs[(nq - 1) % 2]
        begin_values()
        lax.fori_loop(0, nk, lambda j, c: (values(j, s_ref, mx_ref), c)[1], 0)
        end_values()


def _attention(q, k_all, v_all, batch, seq):
    t, attn_w = q.shape
    lk = k_all.shape[1]
    group = attn_w // HEAD_DIM // N_KV_HEADS
    gw = group * HEAD_DIM
    tq = _pick_tile(seq, 128, 8)
    tk = _pick_tile(lk, 1408, 128)
    nq = seq // tq
    rows = group * tq
    kern = functools.partial(_attn_kernel, group=group, nq=nq)
    return pl.pallas_call(
        kern,
        out_shape=jax.ShapeDtypeStruct((t, attn_w), BF16),
        grid=(batch, N_KV_HEADS, nq + 1),
        in_specs=[pl.BlockSpec((tq, gw), lambda b, g, i: (b * nq + jnp.minimum(i, nq - 1), g)),
                  pl.BlockSpec((1, lk, HEAD_DIM), lambda b, g, i: (b, 0, g)),
                  pl.BlockSpec((1, lk, HEAD_DIM), lambda b, g, i: (b, 0, g))],
        out_specs=pl.BlockSpec((tq, gw), lambda b, g, i: (b * nq + jnp.maximum(i - 1, 0), g)),
        scratch_shapes=[pltpu.VMEM((lk // tk, rows, tk), F32), pltpu.VMEM((lk // tk, rows, tk), F32),
                        pltpu.VMEM((rows, HEAD_DIM), F32), pltpu.VMEM((rows, HEAD_DIM), F32),
                        pltpu.VMEM((rows, HEAD_DIM), F32), pltpu.VMEM((rows, HEAD_DIM), F32)],
        compiler_params=_params("parallel", "parallel", "arbitrary"),
        name="attention",
    )(q, k_all, v_all)


def _dft_tables(n1, n2, k1p):
    n = n1 * n2
    k1 = np.arange(k1p)[:, None]
    a = np.arange(n1)[None, :]
    m2 = np.arange(n2)[:, None, None]
    theta = 2.0 * np.pi * ((((n2 * a * k1)[None] + m2 * k1[None]) % n) / n)
    k2 = np.arange(n2)[:, None]
    b = np.arange(n2)[None, :]
    phi = 2.0 * np.pi * ((k2 * b) % n2) / n2
    c, s = np.cos(phi), np.sin(phi)
    fmat = np.block([[c, s], [-s, c]])
    return np.cos(theta), np.sin(theta), fmat


def _mix_forward(cos, sin, n1h):
    n2, k1p, _ = cos.shape
    g = n2 // SUBLANES
    a = np.stack([cos[:, :, :n1h], -sin[:, :, :n1h]], axis=1).reshape(g, SUBLANES, 2, k1p, n1h)
    m = np.zeros((g, 2, k1p, SUBLANES, n1h, SUBLANES))
    for j in range(SUBLANES):
        m[:, :, :, j, :, j] = a[:, j]
    return m.reshape(g, 2 * k1p * SUBLANES, n1h * SUBLANES)


def _mix_inverse(cos, sin, n1h, n1):
    n2, k1p, _ = cos.shape
    g = n2 // SUBLANES
    k1 = np.arange(k1p)
    weight = np.where((k1 == 0) | (k1 == n1 // 2), 1.0, np.where(k1 < n1 // 2, 2.0, 0.0)) / (n1 * n2)
    a = np.stack([cos[:, :, :n1h], -sin[:, :, :n1h]], axis=1) * weight[None, None, :, None]
    a = a.reshape(g, SUBLANES, 2, k1p, n1h)
    m = np.zeros((g, n1h, SUBLANES, 2, k1p, SUBLANES))
    for j in range(SUBLANES):
        m[:, :, j, :, :, j] = a[:, j].transpose(0, 3, 1, 2)
    return m.reshape(g, n1h * SUBLANES, 2 * k1p * SUBLANES)


def _stage_a_kernel(m_ref, x_ref, yr_ref, yi_ref):
    _, n1h, sub, c = x_ref.shape
    k1p = yr_ref.shape[1]
    x = x_ref[0].reshape(n1h * sub, c).astype(BF16)
    y = jnp.dot(m_ref[0], x, preferred_element_type=F32)
    yr_ref[0] = y[:k1p * sub].reshape(k1p, sub, c)
    yi_ref[0] = y[k1p * sub:].reshape(k1p, sub, c)


def _stage_a(mix, vg4, k1p):
    b, n1h, n2, c = vg4.shape
    groups = n2 // SUBLANES
    return pl.pallas_call(
        _stage_a_kernel,
        out_shape=(jax.ShapeDtypeStruct((b, k1p, n2, c), F32),) * 2,
        grid=(groups, b),
        in_specs=[pl.BlockSpec((1,) + mix.shape[1:], lambda g, bi: (g, 0, 0)),
                  pl.BlockSpec((1, n1h, SUBLANES, c), lambda g, bi: (bi, 0, g, 0))],
        out_specs=(pl.BlockSpec((1, k1p, SUBLANES, c), lambda g, bi: (bi, 0, g, 0)),) * 2,
        compiler_params=_params("parallel", "parallel"),
        name="hyena_dft_a",
    )(mix, vg4)


def _stage_b_kernel(yr_ref, yi_ref, f_ref, ft_ref, kr_ref, ki_ref, ur_ref, ui_ref):
    kb, n2 = yr_ref.shape[1:3]
    for r in range(kb):
        y = jnp.concatenate([yr_ref[0, r], yi_ref[0, r]], axis=0).astype(BF16)
        z = jnp.dot(f_ref[...], y, preferred_element_type=F32)
        zr, zi = z[:n2], z[n2:]
        kr = kr_ref[r].astype(F32)
        ki = ki_ref[r].astype(F32)
        p = jnp.concatenate([zr * kr - zi * ki, zr * ki + zi * kr], axis=0).astype(BF16)
        u = jnp.dot(ft_ref[...], p, preferred_element_type=F32)
        ur_ref[0, r] = u[:n2]
        ui_ref[0, r] = u[n2:]


def _stage_b(yr, yi, fmat, fmat_t, kfr, kfi, kb):
    b, k1p, n2, c = yr.shape
    blk = pl.BlockSpec((1, kb, n2, c), lambda bi, k: (bi, k, 0, 0))
    flt = pl.BlockSpec((kb, n2, c), lambda bi, k: (k, 0, 0))
    mat = pl.BlockSpec((2 * n2, 2 * n2), lambda bi, k: (0, 0))
    return pl.pallas_call(
        _stage_b_kernel,
        out_shape=(jax.ShapeDtypeStruct((b, k1p, n2, c), F32),) * 2,
        grid=(b, k1p // kb),
        in_specs=[blk, blk, mat, mat, flt, flt],
        out_specs=(blk, blk),
        compiler_params=_params("parallel", "parallel"),
        name="hyena_dft_b",
    )(yr, yi, fmat, fmat_t, kfr, kfi)


def _stage_c_kernel(m_ref, ur_ref, ui_ref, vg_ref, x0_ref, hb_ref, g_ref, o_ref):
    _, k1p, sub, c = ur_ref.shape
    n1h = vg_ref.shape[1]
    u = jnp.concatenate([ur_ref[0].reshape(k1p * sub, c), ui_ref[0].reshape(k1p * sub, c)], axis=0)
    conv = jnp.dot(m_ref[0], u.astype(BF16), preferred_element_type=F32)
    vg = vg_ref[0].reshape(n1h * sub, c)
    x0 = x0_ref[0].reshape(n1h * sub, c)
    hyo = (conv + hb_ref[...] * vg) * x0
    o_ref[0] = (_rms(hyo) * g_ref[...]).reshape(n1h, sub, c)


def _stage_c(mix, ur, ui, vg4, x04, hy_bias, g_hy):
    b, n1h, n2, c = vg4.shape
    k1p = ur.shape[1]
    groups = n2 // SUBLANES
    ublk = pl.BlockSpec((1, k1p, SUBLANES, c), lambda g, bi: (bi, 0, g, 0))
    xblk = pl.BlockSpec((1, n1h, SUBLANES, c), lambda g, bi: (bi, 0, g, 0))
    vec = pl.BlockSpec((1, c), lambda g, bi: (0, 0))
    return pl.pallas_call(
        _stage_c_kernel,
        out_shape=jax.ShapeDtypeStruct((b, n1h, n2, c), F32),
        grid=(groups, b),
        in_specs=[pl.BlockSpec((1,) + mix.shape[1:], lambda g, bi: (g, 0, 0)),
                  ublk, ublk, xblk, xblk, vec, vec],
        out_specs=xblk,
        compiler_params=_params("parallel", "parallel"),
        name="hyena_dft_c",
    )(mix, ur, ui, vg4, x04, hy_bias.reshape(1, c), g_hy.reshape(1, c))


def _filter_a_kernel(a_ref, w1t_ref, w1c_ref, w1s_ref, b1_ref, w2_ref, b2_ref, w3_ref, b3_ref,
                     w4_ref, fr_ref, dec_ref, yr_ref, yi_ref, norm_ref, *, seq):
    g = pl.program_id(0)
    k1p = yr_ref.shape[0]
    c = norm_ref.shape[1]
    n1 = a_ref.shape[2]
    n1h = n1 // 2
    n2 = DFT_N2

    def lags(idx, j):
        tprime = idx * n2 + (g * SUBLANES + j)
        fwd = idx < n1h
        return tprime, fwd, jnp.where(fwd, tprime, 2 * seq - tprime).astype(F32)

    def stacked(idx, axis):
        parts = [lags(idx, j) for j in range(SUBLANES)]
        return [jnp.concatenate([p[k] for p in parts], axis=axis) for k in range(3)]

    _, _, pos_l = stacked(lax.broadcasted_iota(jnp.int32, (1, n1), 1), 1)
    tprime_r, fwd_r, pos_r = stacked(lax.broadcasted_iota(jnp.int32, (n1, 1), 0), 0)

    band = lax.broadcasted_iota(jnp.int32, (FILTER_BANDS, 1), 0).astype(F32)
    f = 1e-4 + band * ((FILTER_BANDS - 1 - 1e-4) / (FILTER_BANDS - 1))
    ang = f * ((2.0 * math.pi * pos_l) / seq)
    fr = fr_ref[...]
    dot = functools.partial(jnp.dot, preferred_element_type=F32, precision=_HI)
    h = w1t_ref[...] * (pos_l / (seq - 1.0)) + dot(w1c_ref[...], jnp.cos(ang)) + dot(w1s_ref[...], -jnp.sin(ang))
    h = jnp.sin(fr * (h + b1_ref[...]))
    h = jnp.sin(fr * (dot(w2_ref[...], h) + b2_ref[...]))
    h = jnp.sin(fr * (dot(w3_ref[...], h) + b3_ref[...]))
    h = lax.dot_general(h, w4_ref[...], (((0,), (0,)), ((), ())), preferred_element_type=F32)
    t_r = pos_r / (seq - 1.0)
    dec = jnp.abs(dec_ref[...])
    kf = h[:, :c] * jnp.exp(-t_r * dec[0:1, :])
    kb = h[:, c:] * jnp.exp(-t_r * dec[1:2, :])
    kk = jnp.where(fwd_r, kf, jnp.where(tprime_r == seq, 0.0, kb))

    @pl.when(g == 0)
    def _():
        norm_ref[...] = jnp.zeros_like(norm_ref)

    norm_ref[...] += jnp.sum(jnp.abs(kk), axis=0, keepdims=True)
    for j in range(SUBLANES):
        y = jnp.dot(a_ref[j], kk[j * n1:(j + 1) * n1].astype(BF16), preferred_element_type=F32)
        yr_ref[:, j * c:(j + 1) * c] = y[:k1p].astype(BF16)
        yi_ref[:, j * c:(j + 1) * c] = y[k1p:].astype(BF16)


def _filter_a(a_full, flt, seq, c):
    w1, b1, w2, b2, w3, b3, w4, freq, decay = flt
    hid = w2.shape[0]
    n2, k1p2, n1 = a_full.shape
    k1p = k1p2 // 2
    full = lambda shape: pl.BlockSpec(shape, lambda g: (0,) * len(shape))
    kern = functools.partial(_filter_a_kernel, seq=seq)
    return pl.pallas_call(
        kern,
        out_shape=(jax.ShapeDtypeStruct((k1p, n2 * c), BF16),
                   jax.ShapeDtypeStruct((k1p, n2 * c), BF16),
                   jax.ShapeDtypeStruct((1, c), F32)),
        grid=(n2 // SUBLANES,),
        in_specs=[pl.BlockSpec((SUBLANES, k1p2, n1), lambda g: (g, 0, 0)),
                  full((hid, 1)), full((hid, FILTER_BANDS)), full((hid, FILTER_BANDS)), full((hid, 1)),
                  full((hid, hid)), full((hid, 1)), full((hid, hid)), full((hid, 1)),
                  full((hid, 2 * c)), full((hid, 1)), full((2, c))],
        out_specs=(pl.BlockSpec((k1p, SUBLANES * c), lambda g: (0, g)),
                   pl.BlockSpec((k1p, SUBLANES * c), lambda g: (0, g)),
                   pl.BlockSpec((1, c), lambda g: (0, 0))),
        compiler_params=_params("arbitrary"),
        name="hyena_filter_a",
    )(a_full, w1[0:1].T, w1[1:1 + FILTER_BANDS].T, w1[1 + FILTER_BANDS:].T, b1.reshape(hid, 1),
      w2.T, b2.reshape(hid, 1), w3.T, b3.reshape(hid, 1), w4, freq.reshape(hid, 1), decay)


def _filter_b_kernel(yr_ref, yi_ref, f_ref, norm_ref, kr_ref, ki_ref):
    kb, n2 = yr_ref.shape[:2]
    for r in range(kb):
        y = jnp.concatenate([yr_ref[r], yi_ref[r]], axis=0)
        z = jnp.dot(f_ref[...], y, preferred_element_type=F32) / norm_ref[...]
        kr_ref[r] = z[:n2].astype(BF16)
        ki_ref[r] = z[n2:].astype(BF16)


def _filter_b(yr, yi, fmat, norm, kb):
    k1p, n2, c = yr.shape
    blk = pl.BlockSpec((kb, n2, c), lambda k: (k, 0, 0))
    return pl.pallas_call(
        _filter_b_kernel,
        out_shape=(jax.ShapeDtypeStruct((k1p, n2, c), BF16),) * 2,
        grid=(k1p // kb,),
        in_specs=[blk, blk,
                  pl.BlockSpec((2 * n2, 2 * n2), lambda k: (0, 0)),
                  pl.BlockSpec((1, c), lambda k: (0, 0))],
        out_specs=(blk, blk),
        compiler_params=_params("parallel"),
        name="hyena_filter_b",
    )(yr, yi, fmat, norm)


def _hyena(x0c, vg, flt, hy_bias, g_hy, batch, seq):
    c = vg.shape[1]
    n2 = DFT_N2
    n1h = seq // n2
    n1 = 2 * n1h
    k1p = min(n1, -(-(n1 // 2 + 1) // SUBLANES) * SUBLANES)
    kb = _pick_tile(k1p, 4, 1)
    cos, sin, fmat_np = _dft_tables(n1, n2, k1p)
    a_full = jnp.asarray(np.concatenate([cos, -sin], axis=1), BF16)
    mix_a = jnp.asarray(_mix_forward(cos, sin, n1h), BF16)
    mix_c = jnp.asarray(_mix_inverse(cos, sin, n1h, n1), BF16)
    fmat = jnp.asarray(fmat_np, BF16)
    fmat_t = jnp.asarray(fmat_np.T, BF16)

    fyr, fyi, norm = _filter_a(a_full, flt, seq, c)
    kfr, kfi = _filter_b(fyr.reshape(k1p, n2, c), fyi.reshape(k1p, n2, c), fmat, norm, kb)

    vg4 = vg.reshape(batch, n1h, n2, c)
    x04 = x0c.reshape(batch, n1h, n2, c)
    yr, yi = _stage_a(mix_a, vg4, k1p)
    ur, ui = _stage_b(yr, yi, fmat, fmat_t, kfr, kfi, kb)
    y = _stage_c(mix_c, ur, ui, vg4, x04, hy_bias, g_hy)
    return y.reshape(batch * seq, c)


def _out_proj_kernel(x_ref, gate_ref, attn_ref, hy_ref, ga_ref, wa_ref, wh_ref, o_ref):
    ya = (_rms(attn_ref[...].astype(F32)) * ga_ref[...]).astype(BF16)
    y = jnp.dot(ya, wa_ref[...], preferred_element_type=F32)
    y += jnp.dot(hy_ref[...].astype(BF16), wh_ref[...], preferred_element_type=F32)
    o_ref[...] = x_ref[...] + gate_ref[0] * y


def _out_proj(x, mod3, row_of_tile, attn, hyn, g_attn, w_out, tm):
    t, d = x.shape
    aw = attn.shape[1]
    hw = hyn.shape[1]
    return pl.pallas_call(
        _out_proj_kernel,
        out_shape=jax.ShapeDtypeStruct((t, d), F32),
        grid=(t // tm,),
        in_specs=[pl.BlockSpec((tm, d), lambda i: (i, 0)),
                  pl.BlockSpec((1, 1, d), lambda i: (row_of_tile(i), 0, 5)),
                  pl.BlockSpec((tm, aw), lambda i: (i, 0)),
                  pl.BlockSpec((tm, hw), lambda i: (i, 0)),
                  pl.BlockSpec((1, aw), lambda i: (0, 0)),
                  pl.BlockSpec((aw, d), lambda i: (0, 0)),
                  pl.BlockSpec((hw, d), lambda i: (1, 0))],
        out_specs=pl.BlockSpec((tm, d), lambda i: (i, 0)),
        compiler_params=_params("parallel"),
        name="out_proj",
    )(x, mod3, attn, hyn, g_attn.reshape(1, aw), w_out, w_out)


def _rope_tables(seq):
    half = HEAD_DIM // 2
    pos = np.arange(seq)
    inv = ROPE_THETA ** (-np.arange(0, half, 2, dtype=np.float64) / half)
    ang = np.concatenate([(pos // GRID_W)[:, None] * inv, (pos % GRID_W)[:, None] * inv], axis=-1)
    cos = np.concatenate([np.cos(ang), np.cos(ang)], axis=-1)
    sin = np.concatenate([-np.sin(ang), np.sin(ang)], axis=-1)
    return jnp.asarray(cos, F32), jnp.asarray(sin, F32)


def _deinterleave_cols(w, heads):
    lead = w.shape[:-1]
    return w.reshape(*lead, heads, HEAD_DIM // 2, 2).swapaxes(-1, -2).reshape(*lead, heads * HEAD_DIM)


def kernel(x, c, ctx, c_ctx, w_ada, b_ada, g_norm, w_ffn1_up, w_ffn1_down, w_ffn2_up, w_ffn2_down,
           w_in, q_norm, k_norm, conv_w, conv_b, flt_w1, flt_b1, flt_w2, flt_b2, flt_w3, flt_b3,
           flt_w4, flt_freq, flt_decay, hy_bias, g_out, w_out):
    batch, seq, d = x.shape
    lc = ctx.shape[1]
    depth = w_ada.shape[0]
    assert depth == 1, "context-update path of deeper stacks is not implemented"
    attn_w = d // 2
    kvw = N_KV_HEADS * HEAD_DIM
    q_heads = attn_w // HEAD_DIM
    tm = _pick_tile(seq, 512, 128)
    tmc = _pick_tile(batch * lc, 512, 8)
    tiles_per_seq = seq // tm
    lat_row = lambda i: i // tiles_per_seq
    ctx_row = lambda i: batch

    xs = x.reshape(batch * seq, d)
    cs = ctx.reshape(batch * lc, d)
    rows = -(-(batch + 1) // 8) * 8
    c_rows = jnp.zeros((rows, d), F32).at[:batch].set(c).at[batch].set(c_ctx)
    cos, sin = _rope_tables(seq)

    l = 0
    mod3 = _ada_mod(c_rows, w_ada[l], b_ada[l]).reshape(rows, 1, N_MOD * d)

    w1u, w1d = w_ffn1_up[l].astype(BF16), w_ffn1_down[l].astype(BF16)
    w2u, w2d = w_ffn2_up[l].astype(BF16), w_ffn2_down[l].astype(BF16)
    wi = w_in[l]
    k_end = attn_w + kvw
    wi = jnp.concatenate([_deinterleave_cols(wi[:, :k_end], q_heads + N_KV_HEADS), wi[:, k_end:]],
                         axis=1).astype(BF16)
    qn = _deinterleave_cols(q_norm[l], 1).reshape(1, HEAD_DIM)
    kn = _deinterleave_cols(k_norm[l], 1).reshape(1, HEAD_DIM)
    wo = w_out[l].astype(BF16)

    x1 = _ffn(xs, mod3, lat_row, 0, g_norm[l, 0], w1u, w1d, tm)
    c1 = _ffn(cs, mod3, ctx_row, 0, g_norm[l, 0], w1u, w1d, tmc)

    qkv_w = attn_w + 2 * kvw
    q, k, v, x0c, vg = _in_proj(x1, mod3, lat_row, g_norm[l, 1], wi[:, :qkv_w], wi[:, qkv_w:], qn, kn,
                                cos, sin, conv_w[l], conv_b[l], seq, tm)
    kc, vc = _ctx_kv(c1, mod3, batch, g_norm[l, 1], wi[:, attn_w:qkv_w], kn, tmc)
    k_all = jnp.concatenate([kc.reshape(batch, lc, kvw), k.reshape(batch, seq, kvw)], axis=1)
    v_all = jnp.concatenate([vc.reshape(batch, lc, kvw), v.reshape(batch, seq, kvw)], axis=1)
    attn = _attention(q, k_all, v_all, batch, seq)

    flt = (flt_w1[l], flt_b1[l], flt_w2[l], flt_b2[l], flt_w3[l], flt_b3[l], flt_w4[l],
           flt_freq[l], flt_decay[l])
    hyn = _hyena(x0c, vg, flt, hy_bias[l], g_out[l, attn_w:], batch, seq)
    x2 = _out_proj(x1, mod3, lat_row, attn, hyn, g_out[l, :attn_w], wo, tm)

    x3 = _ffn(x2, mod3, lat_row, 2, g_norm[l, 2], w2u, w2d, tm)
    return x3.reshape(batch, seq, d)
```

```python
import functools
import math

import jax
import jax.numpy as jnp
import numpy as np
from jax import lax
from jax.experimental import pallas as pl
from jax.experimental.pallas import tpu as pltpu

F32 = jnp.float32
BF16 = jnp.bfloat16

HEAD_DIM = 128
N_KV_HEADS = 2
GRID_W = 64
ROPE_THETA = 10000.0
FILTER_BANDS = 16
RMS_EPS = 1e-6
N_MOD = 9

V7X_VMEM_LIMIT_BYTES = 56 * 1024 * 1024
SUBLANES = 8
DFT_N2 = 128

_HI = lax.Precision.HIGHEST


def _params(*sem):
    return pltpu.CompilerParams(dimension_semantics=sem, vmem_limit_bytes=V7X_VMEM_LIMIT_BYTES)


def _pick_tile(n, cap, mult):
    best = None
    t = mult
    while t <= min(n, cap):
        if n % t == 0:
            best = t
        t += mult
    assert best is not None, (n, cap, mult)
    return best


def _rms(x, eps=RMS_EPS):
    return x * lax.rsqrt(jnp.mean(x * x, axis=-1, keepdims=True) + eps)


def _ada_kernel(ct_ref, w_ref, b_ref, o_ref, sb_ref, *, n_rows):
    lanes = sb_ref.shape[2]

    @pl.when(pl.program_id(0) == 0)
    def _():
        ct = ct_ref[...]
        s = ct * jax.nn.sigmoid(ct)
        for r in range(n_rows):
            sb_ref[r] = jnp.broadcast_to(s[:, r:r + 1], sb_ref.shape[1:])

    pad = jnp.zeros((o_ref.shape[0] - n_rows, lanes), F32)
    for lt in range(w_ref.shape[1] // lanes):
        w = w_ref[:, lt * lanes:(lt + 1) * lanes]
        rows = [jnp.sum(w * sb_ref[r], axis=0, keepdims=True) for r in range(n_rows)]
        o_ref[:, lt * lanes:(lt + 1) * lanes] = (jnp.concatenate(rows + [pad], axis=0)
                                                 + b_ref[:, lt * lanes:(lt + 1) * lanes])


def _ada_mod(c_cols, n_rows, w_ada, b_ada):
    d, rows = c_cols.shape
    n = w_ada.shape[1]
    tn = _pick_tile(n, 2048, 128)
    return pl.pallas_call(
        functools.partial(_ada_kernel, n_rows=n_rows),
        out_shape=jax.ShapeDtypeStruct((rows, n), F32),
        grid=(n // tn,),
        in_specs=[pl.BlockSpec((d, rows), lambda j: (0, 0)),
                  pl.BlockSpec((d, tn), lambda j: (0, j)),
                  pl.BlockSpec((1, tn), lambda j: (0, j))],
        out_specs=pl.BlockSpec((rows, tn), lambda j: (0, j)),
        scratch_shapes=[pltpu.VMEM((n_rows, d, HEAD_DIM), F32)],
        compiler_params=_params("arbitrary"),
        name="ada_mod",
    )(c_cols, w_ada, b_ada.reshape(1, n))


def _ffn_kernel(*refs, has_h):
    if has_h:
        x_ref, h_ref, gate_ref, wg_ref, wu_ref, wd_ref, o_ref, acc_ref = refs
    else:
        x_ref, shift_ref, scale_ref, gate_ref, g_ref, wg_ref, wu_ref, wd_ref, o_ref, h_ref, acc_ref = refs
    k = pl.program_id(1)

    @pl.when(k == 0)
    def _():
        if not has_h:
            y = _rms(x_ref[...]) * g_ref[...]
            h_ref[...] = (y * (1.0 + scale_ref[0]) + shift_ref[0]).astype(BF16)
        acc_ref[...] = jnp.zeros_like(acc_ref)

    h = h_ref[...]
    g = jnp.dot(h, wg_ref[...], preferred_element_type=F32)
    u = jnp.dot(h, wu_ref[...], preferred_element_type=F32)
    a = (g * jax.nn.sigmoid(g) * u).astype(BF16)
    acc_ref[...] += jnp.dot(a, wd_ref[...], preferred_element_type=F32)

    @pl.when(k == pl.num_programs(1) - 1)
    def _():
        o_ref[...] = x_ref[...] + 0.5 * gate_ref[0] * acc_ref[...]


def _ffn(x, mod3, row_of_tile, slot, g, w_up, w_down, tm, h=None):
    t, d = x.shape
    f = w_down.shape[0]
    tf = _pick_tile(f, 512, 128)
    nf = f // tf
    mod_spec = lambda c: pl.BlockSpec((1, 1, d), lambda i, k: (row_of_tile(i), 0, 3 * slot + c))
    row_tile = pl.BlockSpec((tm, d), lambda i, k: (i, 0))
    weights = [pl.BlockSpec((d, tf), lambda i, k: (0, k)),
               pl.BlockSpec((d, tf), lambda i, k: (0, k + nf)),
               pl.BlockSpec((tf, d), lambda i, k: (k, 0))]
    if h is None:
        in_specs = [row_tile, mod_spec(0), mod_spec(1), mod_spec(2), pl.BlockSpec((1, d), lambda i, k: (0, 0))]
        args = (x, mod3, mod3, mod3, g.reshape(1, d))
        scratch = [pltpu.VMEM((tm, d), BF16), pltpu.VMEM((tm, d), F32)]
    else:
        in_specs = [row_tile, row_tile, mod_spec(2)]
        args = (x, h, mod3)
        scratch = [pltpu.VMEM((tm, d), F32)]
    return pl.pallas_call(
        functools.partial(_ffn_kernel, has_h=h is not None),
        out_shape=jax.ShapeDtypeStruct((t, d), F32),
        grid=(t // tm, nf),
        in_specs=in_specs + weights,
        out_specs=row_tile,
        scratch_shapes=scratch,
        compiler_params=_params("parallel", "arbitrary"),
        name="ffn",
    )(*args, w_up, w_up, w_down)


def _rope(x, cos, sin):
    return x * cos + pltpu.roll(x, HEAD_DIM // 2, axis=1) * sin


HALO = 16


def _in_proj_kernel(x_ref, xp_ref, xn_ref, shift_ref, scale_ref, g_ref, wa_ref, wh_ref, qn_ref, kn_ref,
                    cos_ref, sin_ref, cw_ref, cb_ref, q_ref, k_ref, v_ref, x0c_ref, vg_ref, *,
                    q_scale, tiles_per_seq, cw):
    i = pl.program_id(0)
    tm = x_ref.shape[0]
    attn_w = q_ref.shape[1]
    kvw = k_ref.shape[1]
    c = x0c_ref.shape[1]
    first = (i % tiles_per_seq) == 0
    last = (i % tiles_per_seq) == tiles_per_seq - 1

    x_ext = jnp.concatenate([xp_ref[...], x_ref[...], xn_ref[...]], axis=0)
    y = _rms(x_ext) * g_ref[...]
    h_ext = (y * (1.0 + scale_ref[0]) + shift_ref[0]).astype(BF16)
    h = h_ext[HALO:HALO + tm]

    u = jnp.dot(h, wa_ref[...], preferred_element_type=F32)

    def normed(col, gain):
        return _rope(_rms(u[:, col:col + HEAD_DIM]) * gain, cos_ref[...], sin_ref[...])

    for hd in range(attn_w // HEAD_DIM):
        col = hd * HEAD_DIM
        q_ref[:, col:col + HEAD_DIM] = (normed(col, qn_ref[...]) * q_scale).astype(BF16)
    for hd in range(kvw // HEAD_DIM):
        col = hd * HEAD_DIM
        k_ref[:, col:col + HEAD_DIM] = normed(attn_w + col, kn_ref[...]).astype(BF16)
    v_ref[...] = u[:, attn_w + kvw:].astype(BF16)

    row = lax.broadcasted_iota(jnp.int32, (tm, cw), 0)
    at_start = jnp.logical_and(first, row == 0)
    at_end = jnp.logical_and(last, row == tm - 1)
    ext = tm + 2 * HALO

    def conv(group, c0):
        ue = jnp.dot(h_ext, wh_ref[:, group * c + c0:group * c + c0 + cw], preferred_element_type=F32)
        um = jnp.where(at_start, 0.0, pltpu.roll(ue, 1, axis=0)[HALO:HALO + tm])
        up = jnp.where(at_end, 0.0, pltpu.roll(ue, ext - 1, axis=0)[HALO:HALO + tm])
        w = cw_ref[group][:, c0:c0 + cw]
        return um * w[0:1] + ue[HALO:HALO + tm] * w[1:2] + up * w[2:3] + cb_ref[group][:, c0:c0 + cw]

    for c0 in range(0, c, cw):
        x0c_ref[:, c0:c0 + cw] = conv(0, c0)
        vg_ref[:, c0:c0 + cw] = conv(2, c0) * conv(1, c0)


def _in_proj(x, mod3, row_of_tile, g, w_qkv, w_hy, qn, kn, cos, sin, conv_w, conv_b, seq, tm):
    t, d = x.shape
    kvw = N_KV_HEADS * HEAD_DIM
    qkv_w = w_qkv.shape[1]
    attn_w = qkv_w - 2 * kvw
    c = w_hy.shape[1] // 3
    cw = _pick_tile(c, 512, 128)
    pos_tiles = seq // tm
    hb = tm // HALO
    nhb = t // HALO
    w3 = conv_w.reshape(3, 3, c).transpose(1, 0, 2)
    b3 = conv_b.reshape(3, 1, c)
    mod_spec = lambda ch: pl.BlockSpec((1, 1, d), lambda i: (row_of_tile(i), 0, 3 + ch))
    const = lambda shape: pl.BlockSpec(shape, lambda i: (0,) * len(shape), pipeline_mode=pl.Buffered(1))
    kern = functools.partial(_in_proj_kernel, q_scale=HEAD_DIM ** -0.5 * math.log2(math.e),
                             tiles_per_seq=pos_tiles, cw=cw)
    return pl.pallas_call(
        kern,
        out_shape=(jax.ShapeDtypeStruct((t, attn_w), BF16),
                   jax.ShapeDtypeStruct((t, kvw), BF16),
                   jax.ShapeDtypeStruct((t, kvw), BF16),
                   jax.ShapeDtypeStruct((t, c), F32),
                   jax.ShapeDtypeStruct((t, c), F32)),
        grid=(t // tm,),
        in_specs=[pl.BlockSpec((tm, d), lambda i: (i, 0)),
                  pl.BlockSpec((HALO, d), lambda i: (jnp.maximum(i * hb - 1, 0), 0)),
                  pl.BlockSpec((HALO, d), lambda i: (jnp.minimum((i + 1) * hb, nhb - 1), 0)),
                  mod_spec(0), mod_spec(1),
                  const((1, d)), const((d, qkv_w)), const((d, 3 * c)),
                  const((1, HEAD_DIM)), const((1, HEAD_DIM)),
                  pl.BlockSpec((tm, HEAD_DIM), lambda i: (i % pos_tiles, 0)),
                  pl.BlockSpec((tm, HEAD_DIM), lambda i: (i % pos_tiles, 0)),
                  const((3, 3, c)), const((3, 1, c))],
        out_specs=(pl.BlockSpec((tm, attn_w), lambda i: (i, 0)),
                   pl.BlockSpec((tm, kvw), lambda i: (i, 0)),
                   pl.BlockSpec((tm, kvw), lambda i: (i, 0)),
                   pl.BlockSpec((tm, c), lambda i: (i, 0)),
                   pl.BlockSpec((tm, c), lambda i: (i, 0))),
        compiler_params=_params("parallel"),
        name="in_proj",
    )(x, x, x, mod3, mod3, g.reshape(1, d), w_qkv, w_hy, qn, kn, cos, sin, w3, b3)


def _ctx_kv_kernel(x_ref, shift_ref, scale_ref, g_ref, w_ref, kn_ref, k_ref, v_ref):
    y = _rms(x_ref[...]) * g_ref[...]
    h = (y * (1.0 + scale_ref[0]) + shift_ref[0]).astype(BF16)
    u = jnp.dot(h, w_ref[...], preferred_element_type=F32)
    for hd in range(N_KV_HEADS):
        k_ref[:, hd * HEAD_DIM:(hd + 1) * HEAD_DIM] = (
            _rms(u[:, hd * HEAD_DIM:(hd + 1) * HEAD_DIM]) * kn_ref[...]).astype(BF16)
    v_ref[...] = u[:, N_KV_HEADS * HEAD_DIM:].astype(BF16)


def _ctx_kv(x, mod3, ctx_row, g, w_kv, kn, tm):
    t, d = x.shape
    kvw = N_KV_HEADS * HEAD_DIM
    mod_spec = lambda c: pl.BlockSpec((1, 1, d), lambda i: (ctx_row, 0, 3 + c))
    return pl.pallas_call(
        _ctx_kv_kernel,
        out_shape=(jax.ShapeDtypeStruct((t, kvw), BF16), jax.ShapeDtypeStruct((t, kvw), BF16)),
        grid=(t // tm,),
        in_specs=[pl.BlockSpec((tm, d), lambda i: (i, 0)),
                  mod_spec(0), mod_spec(1),
                  pl.BlockSpec((1, d), lambda i: (0, 0)),
                  pl.BlockSpec((d, 2 * kvw), lambda i: (0, 0)),
                  pl.BlockSpec((1, HEAD_DIM), lambda i: (0, 0))],
        out_specs=(pl.BlockSpec((tm, kvw), lambda i: (i, 0)),
                   pl.BlockSpec((tm, kvw), lambda i: (i, 0))),
        compiler_params=_params("parallel"),
        name="ctx_kv",
    )(x, mod3, mod3, g.reshape(1, d), w_kv, kn)


def _attn_kernel(q_ref, k_ref, v_ref, o_ref, s0_ref, s1_ref, mx0_ref, mx1_ref, ls_ref, acc_ref, *,
                 group, nq):
    i = pl.program_id(2)
    tq = q_ref.shape[0]
    nk, _, tk = s0_ref.shape
    lanes = HEAD_DIM
    bufs = ((s0_ref, mx0_ref), (s1_ref, mx1_ref))

    def scores(j, q, s_ref, mx_ref):
        start = pl.multiple_of(j * tk, tk)
        s = lax.dot_general(q, k_ref[0, pl.ds(start, tk), :], (((1,), (1,)), ((), ())),
                            preferred_element_type=F32)
        s_ref[j] = s
        mx = mx_ref[...]
        for c in range(tk // lanes):
            mx = jnp.maximum(mx, s[:, c * lanes:(c + 1) * lanes])
        mx_ref[...] = mx

    def values(j, s_ref, mx_ref):
        start = pl.multiple_of(j * tk, tk)
        s = s_ref[j]
        m = mx_ref[...]
        ls = ls_ref[...]
        parts = []
        for c in range(tk // lanes):
            pc = jnp.exp2(s[:, c * lanes:(c + 1) * lanes] - m)
            ls = ls + pc
            parts.append(pc.astype(BF16))
        ls_ref[...] = ls
        p = jnp.concatenate(parts, axis=1)
        acc_ref[...] += jnp.dot(p, v_ref[0, pl.ds(start, tk), :], preferred_element_type=F32)

    def begin_scores(mx_ref):
        mx_ref[...] = jnp.full_like(mx_ref, -jnp.inf)
        return jnp.concatenate([q_ref[:, h * HEAD_DIM:(h + 1) * HEAD_DIM] for h in range(group)], axis=0)

    def end_scores(mx_ref):
        mx_ref[...] = jnp.broadcast_to(jnp.max(mx_ref[...], axis=-1, keepdims=True), mx_ref.shape)

    def begin_values():
        ls_ref[...] = jnp.zeros_like(ls_ref)
        acc_ref[...] = jnp.zeros_like(acc_ref)

    def end_values():
        o = acc_ref[...] / jnp.sum(ls_ref[...], axis=-1, keepdims=True)
        for h in range(group):
            o_ref[:, h * HEAD_DIM:(h + 1) * HEAD_DIM] = o[h * tq:(h + 1) * tq].astype(BF16)

    @pl.when(i == 0)
    def _():
        s_ref, mx_ref = bufs[0]
        q = begin_scores(mx_ref)
        lax.fori_loop(0, nk, lambda j, c: (scores(j, q, s_ref, mx_ref), c)[1], 0)
        end_scores(mx_ref)

    for parity in range(2):
        @pl.when(jnp.logical_and(jnp.logical_and(i > 0, i < nq), i % 2 == parity))
        def _():
            (cs_ref, cmx_ref), (ps_ref, pmx_ref) = bufs[parity], bufs[1 - parity]
            q = begin_scores(cmx_ref)
            begin_values()

            def both(j, c):
                scores(j, q, cs_ref, cmx_ref)
                values(j, ps_ref, pmx_ref)
                return c

            lax.fori_loop(0, nk, both, 0, unroll=True)
            end_scores(cmx_ref)
            end_values()

    @pl.when(i == nq)
    def _():
        s_ref, mx_ref = bufs[(nq - 1) % 2]
        begin_values()
        lax.fori_loop(0, nk, lambda j, c: (values(j, s_ref, mx_ref), c)[1], 0)
        end_values()


def _attention(q, k_all, v_all, batch, seq):
    t, attn_w = q.shape
    lk = k_all.shape[1]
    group = attn_w // HEAD_DIM // N_KV_HEADS
    gw = group * HEAD_DIM
    tq = _pick_tile(seq, 128, 8)
    tk = _pick_tile(lk, 1408, 128)
    nq = seq // tq
    rows = group * tq
    kern = functools.partial(_attn_kernel, group=group, nq=nq)
    return pl.pallas_call(
        kern,
        out_shape=jax.ShapeDtypeStruct((t, attn_w), BF16),
        grid=(batch, N_KV_HEADS, nq + 1),
        in_specs=[pl.BlockSpec((tq, gw), lambda b, g, i: (b * nq + jnp.minimum(i, nq - 1), g)),
                  pl.BlockSpec((1, lk, HEAD_DIM), lambda b, g, i: (b, 0, g)),
                  pl.BlockSpec((1, lk, HEAD_DIM), lambda b, g, i: (b, 0, g))],
        out_specs=pl.BlockSpec((tq, gw), lambda b, g, i: (b * nq + jnp.maximum(i - 1, 0), g)),
        scratch_shapes=[pltpu.VMEM((lk // tk, rows, tk), F32), pltpu.VMEM((lk // tk, rows, tk), F32),
                        pltpu.VMEM((rows, HEAD_DIM), F32), pltpu.VMEM((rows, HEAD_DIM), F32),
                        pltpu.VMEM((rows, HEAD_DIM), F32), pltpu.VMEM((rows, HEAD_DIM), F32)],
        compiler_params=_params("parallel", "parallel", "arbitrary"),
        name="attention",
    )(q, k_all, v_all)


def _dft_tables(n1, n2, k1p):
    n = n1 * n2
    k1 = np.arange(k1p)[:, None]
    a = np.arange(n1)[None, :]
    m2 = np.arange(n2)[:, None, None]
    theta = 2.0 * np.pi * ((((n2 * a * k1)[None] + m2 * k1[None]) % n) / n)
    k2 = np.arange(n2)[:, None]
    b = np.arange(n2)[None, :]
    phi = 2.0 * np.pi * ((k2 * b) % n2) / n2
    c, s = np.cos(phi), np.sin(phi)
    fmat = np.block([[c, s], [-s, c]])
    return np.cos(theta), np.sin(theta), fmat


def _mix_forward(cos, sin, n1h):
    n2, k1p, _ = cos.shape
    g = n2 // SUBLANES
    a = np.stack([cos[:, :, :n1h], -sin[:, :, :n1h]], axis=1).reshape(g, SUBLANES, 2, k1p, n1h)
    m = np.zeros((g, 2, k1p, SUBLANES, n1h, SUBLANES))
    for j in range(SUBLANES):
        m[:, :, :, j, :, j] = a[:, j]
    return m.reshape(g, 2 * k1p * SUBLANES, n1h * SUBLANES)


def _mix_inverse(cos, sin, n1h, n1):
    n2, k1p, _ = cos.shape
    g = n2 // SUBLANES
    k1 = np.arange(k1p)
    weight = np.where((k1 == 0) | (k1 == n1 // 2), 1.0, np.where(k1 < n1 // 2, 2.0, 0.0)) / (n1 * n2)
    a = np.stack([cos[:, :, :n1h], -sin[:, :, :n1h]], axis=1) * weight[None, None, :, None]
    a = a.reshape(g, SUBLANES, 2, k1p, n1h)
    m = np.zeros((g, n1h, SUBLANES, 2, k1p, SUBLANES))
    for j in range(SUBLANES):
        m[:, :, j, :, :, j] = a[:, j].transpose(0, 3, 1, 2)
    return m.reshape(g, n1h * SUBLANES, 2 * k1p * SUBLANES)


def _stage_a_kernel(m_ref, x_ref, yr_ref, yi_ref):
    _, n1h, sub, c = x_ref.shape
    k1p = yr_ref.shape[1]
    x = x_ref[0].reshape(n1h * sub, c).astype(BF16)
    y = jnp.dot(m_ref[0], x, preferred_element_type=F32)
    yr_ref[0] = y[:k1p * sub].reshape(k1p, sub, c)
    yi_ref[0] = y[k1p * sub:].reshape(k1p, sub, c)


def _stage_a(mix, vg4, k1p):
    b, n1h, n2, c = vg4.shape
    groups = n2 // SUBLANES
    return pl.pallas_call(
        _stage_a_kernel,
        out_shape=(jax.ShapeDtypeStruct((b, k1p, n2, c), F32),) * 2,
        grid=(groups, b),
        in_specs=[pl.BlockSpec((1,) + mix.shape[1:], lambda g, bi: (g, 0, 0)),
                  pl.BlockSpec((1, n1h, SUBLANES, c), lambda g, bi: (bi, 0, g, 0))],
        out_specs=(pl.BlockSpec((1, k1p, SUBLANES, c), lambda g, bi: (bi, 0, g, 0)),) * 2,
        compiler_params=_params("parallel", "parallel"),
        name="hyena_dft_a",
    )(mix, vg4)


def _stage_b_kernel(yr_ref, yi_ref, f_ref, ft_ref, kr_ref, ki_ref, ur_ref, ui_ref):
    kb, n2 = yr_ref.shape[1:3]
    for r in range(kb):
        y = jnp.concatenate([yr_ref[0, r], yi_ref[0, r]], axis=0).astype(BF16)
        z = jnp.dot(f_ref[...], y, preferred_element_type=F32)
        zr, zi = z[:n2], z[n2:]
        kr = kr_ref[r].astype(F32)
        ki = ki_ref[r].astype(F32)
        p = jnp.concatenate([zr * kr - zi * ki, zr * ki + zi * kr], axis=0).astype(BF16)
        u = jnp.dot(ft_ref[...], p, preferred_element_type=F32)
        ur_ref[0, r] = u[:n2]
        ui_ref[0, r] = u[n2:]


def _stage_b(yr, yi, fmat, fmat_t, kfr, kfi, kb):
    b, k1p, n2, c = yr.shape
    blk = pl.BlockSpec((1, kb, n2, c), lambda bi, k: (bi, k, 0, 0))
    flt = pl.BlockSpec((kb, n2, c), lambda bi, k: (k, 0, 0))
    mat = pl.BlockSpec((2 * n2, 2 * n2), lambda bi, k: (0, 0))
    return pl.pallas_call(
        _stage_b_kernel,
        out_shape=(jax.ShapeDtypeStruct((b, k1p, n2, c), F32),) * 2,
        grid=(b, k1p // kb),
        in_specs=[blk, blk, mat, mat, flt, flt],
        out_specs=(blk, blk),
        compiler_params=_params("parallel", "parallel"),
        name="hyena_dft_b",
    )(yr, yi, fmat, fmat_t, kfr, kfi)


def _stage_c_kernel(m_ref, ur_ref, ui_ref, vg_ref, x0_ref, hb_ref, g_ref, o_ref):
    _, k1p, sub, c = ur_ref.shape
    n1h = vg_ref.shape[1]
    u = jnp.concatenate([ur_ref[0].reshape(k1p * sub, c), ui_ref[0].reshape(k1p * sub, c)], axis=0)
    conv = jnp.dot(m_ref[0], u.astype(BF16), preferred_element_type=F32)
    vg = vg_ref[0].reshape(n1h * sub, c)
    x0 = x0_ref[0].reshape(n1h * sub, c)
    hyo = (conv + hb_ref[...] * vg) * x0
    o_ref[0] = (_rms(hyo) * g_ref[...]).reshape(n1h, sub, c)


def _stage_c(mix, ur, ui, vg4, x04, hy_bias, g_hy):
    b, n1h, n2, c = vg4.shape
    k1p = ur.shape[1]
    groups = n2 // SUBLANES
    ublk = pl.BlockSpec((1, k1p, SUBLANES, c), lambda g, bi: (bi, 0, g, 0))
    xblk = pl.BlockSpec((1, n1h, SUBLANES, c), lambda g, bi: (bi, 0, g, 0))
    vec = pl.BlockSpec((1, c), lambda g, bi: (0, 0))
    return pl.pallas_call(
        _stage_c_kernel,
        out_shape=jax.ShapeDtypeStruct((b, n1h, n2, c), F32),
        grid=(groups, b),
        in_specs=[pl.BlockSpec((1,) + mix.shape[1:], lambda g, bi: (g, 0, 0)),
                  ublk, ublk, xblk, xblk, vec, vec],
        out_specs=xblk,
        compiler_params=_params("parallel", "parallel"),
        name="hyena_dft_c",
    )(mix, ur, ui, vg4, x04, hy_bias.reshape(1, c), g_hy.reshape(1, c))


def _filter_a_kernel(a_ref, w1t_ref, w1c_ref, w1s_ref, b1_ref, w2_ref, b2_ref, w3_ref, b3_ref,
                     w4_ref, fr_ref, dec_ref, yr_ref, yi_ref, norm_ref, *, seq):
    g = pl.program_id(0)
    k1p = yr_ref.shape[0]
    c = norm_ref.shape[1]
    n1 = a_ref.shape[2]
    n1h = n1 // 2
    n2 = DFT_N2

    def lags(idx, j):
        tprime = idx * n2 + (g * SUBLANES + j)
        fwd = idx < n1h
        return tprime, fwd, jnp.where(fwd, tprime, 2 * seq - tprime).astype(F32)

    def stacked(idx, axis):
        parts = [lags(idx, j) for j in range(SUBLANES)]
        return [jnp.concatenate([p[k] for p in parts], axis=axis) for k in range(3)]

    _, _, pos_l = stacked(lax.broadcasted_iota(jnp.int32, (1, n1), 1), 1)
    tprime_r, fwd_r, pos_r = stacked(lax.broadcasted_iota(jnp.int32, (n1, 1), 0), 0)

    band = lax.broadcasted_iota(jnp.int32, (FILTER_BANDS, 1), 0).astype(F32)
    f = 1e-4 + band * ((FILTER_BANDS - 1 - 1e-4) / (FILTER_BANDS - 1))
    ang = f * ((2.0 * math.pi * pos_l) / seq)
    fr = fr_ref[...]
    dot = functools.partial(jnp.dot, preferred_element_type=F32, precision=_HI)
    h = w1t_ref[...] * (pos_l / (seq - 1.0)) + dot(w1c_ref[...], jnp.cos(ang)) + dot(w1s_ref[...], -jnp.sin(ang))
    h = jnp.sin(fr * (h + b1_ref[...]))
    h = jnp.sin(fr * (dot(w2_ref[...], h) + b2_ref[...]))
    h = jnp.sin(fr * (dot(w3_ref[...], h) + b3_ref[...]))
    h = lax.dot_general(h, w4_ref[...], (((0,), (0,)), ((), ())), preferred_element_type=F32)
    t_r = pos_r / (seq - 1.0)
    dec = jnp.abs(dec_ref[...])
    kf = h[:, :c] * jnp.exp(-t_r * dec[0:1, :])
    kb = h[:, c:] * jnp.exp(-t_r * dec[1:2, :])
    kk = jnp.where(fwd_r, kf, jnp.where(tprime_r == seq, 0.0, kb))

    @pl.when(g == 0)
    def _():
        norm_ref[...] = jnp.zeros_like(norm_ref)

    norm_ref[...] += jnp.sum(jnp.abs(kk), axis=0, keepdims=True)
    for j in range(SUBLANES):
        y = jnp.dot(a_ref[j], kk[j * n1:(j + 1) * n1].astype(BF16), preferred_element_type=F32)
        yr_ref[:, j * c:(j + 1) * c] = y[:k1p].astype(BF16)
        yi_ref[:, j * c:(j + 1) * c] = y[k1p:].astype(BF16)


def _filter_a(a_full, flt, seq, c):
    w1, b1, w2, b2, w3, b3, w4, freq, decay = flt
    hid = w2.shape[0]
    n2, k1p2, n1 = a_full.shape
    k1p = k1p2 // 2
    full = lambda shape: pl.BlockSpec(shape, lambda g: (0,) * len(shape))
    kern = functools.partial(_filter_a_kernel, seq=seq)
    return pl.pallas_call(
        kern,
        out_shape=(jax.ShapeDtypeStruct((k1p, n2 * c), BF16),
                   jax.ShapeDtypeStruct((k1p, n2 * c), BF16),
                   jax.ShapeDtypeStruct((1, c), F32)),
        grid=(n2 // SUBLANES,),
        in_specs=[pl.BlockSpec((SUBLANES, k1p2, n1), lambda g: (g, 0, 0)),
                  full((hid, 1)), full((hid, FILTER_BANDS)), full((hid, FILTER_BANDS)), full((hid, 1)),
                  full((hid, hid)), full((hid, 1)), full((hid, hid)), full((hid, 1)),
                  full((hid, 2 * c)), full((hid, 1)), full((2, c))],
        out_specs=(pl.BlockSpec((k1p, SUBLANES * c), lambda g: (0, g)),
                   pl.BlockSpec((k1p, SUBLANES * c), lambda g: (0, g)),
                   pl.BlockSpec((1, c), lambda g: (0, 0))),
        compiler_params=_params("arbitrary"),
        name="hyena_filter_a",
    )(a_full, w1[0:1].T, w1[1:1 + FILTER_BANDS].T, w1[1 + FILTER_BANDS:].T, b1.reshape(hid, 1),
      w2.T, b2.reshape(hid, 1), w3.T, b3.reshape(hid, 1), w4, freq.reshape(hid, 1), decay)


def _filter_b_kernel(yr_ref, yi_ref, f_ref, norm_ref, kr_ref, ki_ref):
    kb, n2 = yr_ref.shape[:2]
    for r in range(kb):
        y = jnp.concatenate([yr_ref[r], yi_ref[r]], axis=0)
        z = jnp.dot(f_ref[...], y, preferred_element_type=F32) / norm_ref[...]
        kr_ref[r] = z[:n2].astype(BF16)
        ki_ref[r] = z[n2:].astype(BF16)


def _filter_b(yr, yi, fmat, norm, kb):
    k1p, n2, c = yr.shape
    blk = pl.BlockSpec((kb, n2, c), lambda k: (k, 0, 0))
    return pl.pallas_call(
        _filter_b_kernel,
        out_shape=(jax.ShapeDtypeStruct((k1p, n2, c), BF16),) * 2,
        grid=(k1p // kb,),
        in_specs=[blk, blk,
                  pl.BlockSpec((2 * n2, 2 * n2), lambda k: (0, 0)),
                  pl.BlockSpec((1, c), lambda k: (0, 0))],
        out_specs=(blk, blk),
        compiler_params=_params("parallel"),
        name="hyena_filter_b",
    )(yr, yi, fmat, norm)


def _hyena(x0c, vg, flt, hy_bias, g_hy, batch, seq):
    c = vg.shape[1]
    n2 = DFT_N2
    n1h = seq // n2
    n1 = 2 * n1h
    k1p = min(n1, -(-(n1 // 2 + 1) // SUBLANES) * SUBLANES)
    kb = _pick_tile(k1p, 4, 1)
    cos, sin, fmat_np = _dft_tables(n1, n2, k1p)
    a_full = jnp.asarray(np.concatenate([cos, -sin], axis=1), BF16)
    mix_a = jnp.asarray(_mix_forward(cos, sin, n1h), BF16)
    mix_c = jnp.asarray(_mix_inverse(cos, sin, n1h, n1), BF16)
    fmat = jnp.asarray(fmat_np, BF16)
    fmat_t = jnp.asarray(fmat_np.T, BF16)

    fyr, fyi, norm = _filter_a(a_full, flt, seq, c)
    kfr, kfi = _filter_b(fyr.reshape(k1p, n2, c), fyi.reshape(k1p, n2, c), fmat, norm, kb)

    vg4 = vg.reshape(batch, n1h, n2, c)
    x04 = x0c.reshape(batch, n1h, n2, c)
    yr, yi = _stage_a(mix_a, vg4, k1p)
    ur, ui = _stage_b(yr, yi, fmat, fmat_t, kfr, kfi, kb)
    y = _stage_c(mix_c, ur, ui, vg4, x04, hy_bias, g_hy)
    return y.reshape(batch * seq, c)


def _out_proj_kernel(x_ref, gate_ref, shift_ref, scale_ref, attn_ref, hy_ref, ga_ref, gn_ref, wa_ref, wh_ref,
                     o_ref, h_ref):
    ya = (_rms(attn_ref[...].astype(F32)) * ga_ref[...]).astype(BF16)
    y = jnp.dot(ya, wa_ref[...], preferred_element_type=F32)
    y += jnp.dot(hy_ref[...].astype(BF16), wh_ref[...], preferred_element_type=F32)
    x = x_ref[...] + gate_ref[0] * y
    o_ref[...] = x
    h_ref[...] = (_rms(x) * gn_ref[...] * (1.0 + scale_ref[0]) + shift_ref[0]).astype(BF16)


def _out_proj(x, mod3, row_of_tile, attn, hyn, g_attn, g_next, w_out, tm):
    t, d = x.shape
    aw = attn.shape[1]
    hw = hyn.shape[1]
    mod_spec = lambda ch: pl.BlockSpec((1, 1, d), lambda i: (row_of_tile(i), 0, ch))
    const = lambda shape, idx: pl.BlockSpec(shape, lambda i: idx, pipeline_mode=pl.Buffered(1))
    return pl.pallas_call(
        _out_proj_kernel,
        out_shape=(jax.ShapeDtypeStruct((t, d), F32), jax.ShapeDtypeStruct((t, d), BF16)),
        grid=(t // tm,),
        in_specs=[pl.BlockSpec((tm, d), lambda i: (i, 0)),
                  mod_spec(5), mod_spec(6), mod_spec(7),
                  pl.BlockSpec((tm, aw), lambda i: (i, 0)),
                  pl.BlockSpec((tm, hw), lambda i: (i, 0)),
                  const((1, aw), (0, 0)), const((1, d), (0, 0)),
                  const((aw, d), (0, 0)), const((hw, d), (1, 0))],
        out_specs=(pl.BlockSpec((tm, d), lambda i: (i, 0)), pl.BlockSpec((tm, d), lambda i: (i, 0))),
        compiler_params=_params("parallel"),
        name="out_proj",
    )(x, mod3, mod3, mod3, attn, hyn, g_attn.reshape(1, aw), g_next.reshape(1, d), w_out, w_out)


def _rope_tables(seq):
    half = HEAD_DIM // 2
    pos = np.arange(seq)
    inv = ROPE_THETA ** (-np.arange(0, half, 2, dtype=np.float64) / half)
    ang = np.concatenate([(pos // GRID_W)[:, None] * inv, (pos % GRID_W)[:, None] * inv], axis=-1)
    cos = np.concatenate([np.cos(ang), np.cos(ang)], axis=-1)
    sin = np.concatenate([-np.sin(ang), np.sin(ang)], axis=-1)
    return jnp.asarray(cos, F32), jnp.asarray(sin, F32)


def _deinterleave_cols(w, heads):
    lead = w.shape[:-1]
    return w.reshape(*lead, heads, HEAD_DIM // 2, 2).swapaxes(-1, -2).reshape(*lead, heads * HEAD_DIM)


def kernel(x, c, ctx, c_ctx, w_ada, b_ada, g_norm, w_ffn1_up, w_ffn1_down, w_ffn2_up, w_ffn2_down,
           w_in, q_norm, k_norm, conv_w, conv_b, flt_w1, flt_b1, flt_w2, flt_b2, flt_w3, flt_b3,
           flt_w4, flt_freq, flt_decay, hy_bias, g_out, w_out):
    batch, seq, d = x.shape
    lc = ctx.shape[1]
    depth = w_ada.shape[0]
    assert depth == 1, "context-update path of deeper stacks is not implemented"
    attn_w = d // 2
    kvw = N_KV_HEADS * HEAD_DIM
    q_heads = attn_w // HEAD_DIM
    tm = _pick_tile(seq, 512, 128)
    tmc = _pick_tile(batch * lc, 512, 8)
    tiles_per_seq = seq // tm
    lat_row = lambda i: i // tiles_per_seq
    ctx_row = lambda i: batch

    xs = x.reshape(batch * seq, d)
    cs = ctx.reshape(batch * lc, d)
    rows = -(-(batch + 1) // 8) * 8
    c_cols = jnp.zeros((d, rows), F32).at[:, :batch].set(c.T).at[:, batch].set(c_ctx)
    cos, sin = _rope_tables(seq)

    l = 0
    mod3 = _ada_mod(c_cols, batch + 1, w_ada[l], b_ada[l]).reshape(rows, 1, N_MOD * d)

    w1u, w1d = w_ffn1_up[l].astype(BF16), w_ffn1_down[l].astype(BF16)
    w2u, w2d = w_ffn2_up[l].astype(BF16), w_ffn2_down[l].astype(BF16)
    wi = w_in[l]
    k_end = attn_w + kvw
    qkv_w = attn_w + 2 * kvw
    w_qkv = jnp.concatenate([_deinterleave_cols(wi[:, :k_end], q_heads + N_KV_HEADS), wi[:, k_end:qkv_w]],
                            axis=1).astype(BF16)
    w_hy = wi[:, qkv_w:].astype(BF16)
    qn = _deinterleave_cols(q_norm[l], 1).reshape(1, HEAD_DIM)
    kn = _deinterleave_cols(k_norm[l], 1).reshape(1, HEAD_DIM)
    wo = w_out[l].astype(BF16)

    x1 = _ffn(xs, mod3, lat_row, 0, g_norm[l, 0], w1u, w1d, tm)
    c1 = _ffn(cs, mod3, ctx_row, 0, g_norm[l, 0], w1u, w1d, tmc)

    q, k, v, x0c, vg = _in_proj(x1, mod3, lat_row, g_norm[l, 1], w_qkv, w_hy, qn, kn,
                                cos, sin, conv_w[l], conv_b[l], seq, tm)
    kc, vc = _ctx_kv(c1, mod3, batch, g_norm[l, 1], w_qkv[:, attn_w:], kn, tmc)
    k_all = jnp.concatenate([kc.reshape(batch, lc, kvw), k.reshape(batch, seq, kvw)], axis=1)
    v_all = jnp.concatenate([vc.reshape(batch, lc, kvw), v.reshape(batch, seq, kvw)], axis=1)
    attn = _attention(q, k_all, v_all, batch, seq)

    flt = (flt_w1[l], flt_b1[l], flt_w2[l], flt_b2[l], flt_w3[l], flt_b3[l], flt_w4[l],
           flt_freq[l], flt_decay[l])
    hyn = _hyena(x0c, vg, flt, hy_bias[l], g_out[l, attn_w:], batch, seq)
    x2, h2 = _out_proj(x1, mod3, lat_row, attn, hyn, g_out[l, :attn_w], g_norm[l, 2], wo, tm)

    x3 = _ffn(x2, mod3, lat_row, 2, g_norm[l, 2], w2u, w2d, tm, h=h2)
    return x3.reshape(batch, seq, d)
```

```python
import functools
import math

import jax
import jax.numpy as jnp
import numpy as np
from jax import lax
from jax.experimental import pallas as pl
from jax.experimental.pallas import tpu as pltpu

F32 = jnp.float32
BF16 = jnp.bfloat16

HEAD_DIM = 128
N_KV_HEADS = 2
GRID_W = 64
ROPE_THETA = 10000.0
FILTER_BANDS = 16
RMS_EPS = 1e-6
N_MOD = 9

V7X_VMEM_LIMIT_BYTES = 56 * 1024 * 1024
SUBLANES = 8
DFT_N2 = 128

_HI = lax.Precision.HIGHEST


def _params(*sem):
    return pltpu.CompilerParams(dimension_semantics=sem, vmem_limit_bytes=V7X_VMEM_LIMIT_BYTES)


def _pick_tile(n, cap, mult):
    best = None
    t = mult
    while t <= min(n, cap):
        if n % t == 0:
            best = t
        t += mult
    assert best is not None, (n, cap, mult)
    return best


def _rms(x, eps=RMS_EPS):
    return x * lax.rsqrt(jnp.mean(x * x, axis=-1, keepdims=True) + eps)


def _ada_kernel(ct_ref, w_ref, b_ref, o_ref, sb_ref, *, n_rows):
    lanes = sb_ref.shape[2]

    @pl.when(pl.program_id(0) == 0)
    def _():
        ct = ct_ref[...]
        s = ct * jax.nn.sigmoid(ct)
        for r in range(n_rows):
            sb_ref[r] = jnp.broadcast_to(s[:, r:r + 1], sb_ref.shape[1:])

    pad = jnp.zeros((o_ref.shape[0] - n_rows, lanes), F32)
    for lt in range(w_ref.shape[1] // lanes):
        w = w_ref[:, lt * lanes:(lt + 1) * lanes]
        rows = [jnp.sum(w * sb_ref[r], axis=0, keepdims=True) for r in range(n_rows)]
        o_ref[:, lt * lanes:(lt + 1) * lanes] = (jnp.concatenate(rows + [pad], axis=0)
                                                 + b_ref[:, lt * lanes:(lt + 1) * lanes])


def _ada_mod(c_cols, n_rows, w_ada, b_ada):
    d, rows = c_cols.shape
    n = w_ada.shape[1]
    tn = _pick_tile(n, 2048, 128)
    return pl.pallas_call(
        functools.partial(_ada_kernel, n_rows=n_rows),
        out_shape=jax.ShapeDtypeStruct((rows, n), F32),
        grid=(n // tn,),
        in_specs=[pl.BlockSpec((d, rows), lambda j: (0, 0)),
                  pl.BlockSpec((d, tn), lambda j: (0, j)),
                  pl.BlockSpec((1, tn), lambda j: (0, j))],
        out_specs=pl.BlockSpec((rows, tn), lambda j: (0, j)),
        scratch_shapes=[pltpu.VMEM((n_rows, d, HEAD_DIM), F32)],
        compiler_params=_params("arbitrary"),
        name="ada_mod",
    )(c_cols, w_ada, b_ada.reshape(1, n))


def _ffn_kernel(*refs, has_h):
    if has_h:
        x_ref, h_ref, gate_ref, wg_ref, wu_ref, wd_ref, o_ref, acc_ref = refs
    else:
        x_ref, shift_ref, scale_ref, gate_ref, g_ref, wg_ref, wu_ref, wd_ref, o_ref, h_ref, acc_ref = refs
    k = pl.program_id(1)
    last = pl.num_programs(1) - 1

    def chunk(h):
        g = jnp.dot(h, wg_ref[...], preferred_element_type=F32)
        u = jnp.dot(h, wu_ref[...], preferred_element_type=F32)
        a = (g * jax.nn.sigmoid(g) * u).astype(BF16)
        return jnp.dot(a, wd_ref[...], preferred_element_type=F32)

    @pl.when(k == 0)
    def _():
        if has_h:
            h = h_ref[...]
        else:
            y = _rms(x_ref[...]) * g_ref[...]
            h = (y * (1.0 + scale_ref[0]) + shift_ref[0]).astype(BF16)
            h_ref[...] = h
        acc_ref[...] = chunk(h)

    @pl.when(jnp.logical_and(k > 0, k < last))
    def _():
        acc_ref[...] += chunk(h_ref[...])

    @pl.when(jnp.logical_and(k > 0, k == last))
    def _():
        o_ref[...] = x_ref[...] + 0.5 * gate_ref[0] * (acc_ref[...] + chunk(h_ref[...]))


def _ffn(x, mod3, row_of_tile, slot, g, w_up, w_down, tm, h=None):
    t, d = x.shape
    f = w_down.shape[0]
    tf = _pick_tile(f, 512, 128)
    nf = f // tf
    assert nf >= 2, "the kernel keeps separate first / last chunk paths"
    mod_spec = lambda c: pl.BlockSpec((1, 1, d), lambda i, k: (row_of_tile(i), 0, 3 * slot + c))
    row_tile = pl.BlockSpec((tm, d), lambda i, k: (i, 0))
    weights = [pl.BlockSpec((d, tf), lambda i, k: (0, k)),
               pl.BlockSpec((d, tf), lambda i, k: (0, k + nf)),
               pl.BlockSpec((tf, d), lambda i, k: (k, 0))]
    if h is None:
        in_specs = [row_tile, mod_spec(0), mod_spec(1), mod_spec(2), pl.BlockSpec((1, d), lambda i, k: (0, 0))]
        args = (x, mod3, mod3, mod3, g.reshape(1, d))
        scratch = [pltpu.VMEM((tm, d), BF16), pltpu.VMEM((tm, d), F32)]
    else:
        in_specs = [row_tile, row_tile, mod_spec(2)]
        args = (x, h, mod3)
        scratch = [pltpu.VMEM((tm, d), F32)]
    return pl.pallas_call(
        functools.partial(_ffn_kernel, has_h=h is not None),
        out_shape=jax.ShapeDtypeStruct((t, d), F32),
        grid=(t // tm, nf),
        in_specs=in_specs + weights,
        out_specs=row_tile,
        scratch_shapes=scratch,
        compiler_params=_params("parallel", "arbitrary"),
        name="ffn",
    )(*args, w_up, w_up, w_down)


def _rope(x, cos, sin):
    lane = lax.broadcasted_iota(jnp.int32, x.shape, 1)
    partner = jnp.where(lane % 2 == 0, pltpu.roll(x, HEAD_DIM - 1, axis=1), pltpu.roll(x, 1, axis=1))
    return x * cos + partner * sin


HALO = 16


def _in_proj_kernel(x_ref, xp_ref, xn_ref, shift_ref, scale_ref, g_ref, w_ref, qn_ref, kn_ref,
                    cos_ref, sin_ref, cw_ref, cb_ref, q_ref, k_ref, v_ref, x0c_ref, vg_ref, *,
                    q_scale, tiles_per_seq, cw):
    i = pl.program_id(0)
    tm = x_ref.shape[0]
    attn_w = q_ref.shape[1]
    kvw = k_ref.shape[1]
    c = x0c_ref.shape[1]
    first = (i % tiles_per_seq) == 0
    last = (i % tiles_per_seq) == tiles_per_seq - 1

    x_ext = jnp.concatenate([xp_ref[...], x_ref[...], xn_ref[...]], axis=0)
    y = _rms(x_ext) * g_ref[...]
    h_ext = (y * (1.0 + scale_ref[0]) + shift_ref[0]).astype(BF16)
    h = h_ext[HALO:HALO + tm]

    qkv_w = attn_w + 2 * kvw
    u = jnp.dot(h, w_ref[:, :qkv_w], preferred_element_type=F32)

    def normed(col, gain):
        return _rope(_rms(u[:, col:col + HEAD_DIM]) * gain, cos_ref[...], sin_ref[...])

    for hd in range(attn_w // HEAD_DIM):
        col = hd * HEAD_DIM
        q_ref[:, col:col + HEAD_DIM] = (normed(col, qn_ref[...]) * q_scale).astype(BF16)
    for hd in range(kvw // HEAD_DIM):
        col = hd * HEAD_DIM
        k_ref[:, col:col + HEAD_DIM] = normed(attn_w + col, kn_ref[...]).astype(BF16)
    v_ref[...] = u[:, attn_w + kvw:].astype(BF16)

    row = lax.broadcasted_iota(jnp.int32, (tm, cw), 0)
    at_start = jnp.logical_and(first, row == 0)
    at_end = jnp.logical_and(last, row == tm - 1)
    ext = tm + 2 * HALO

    def conv(group, c0):
        col = qkv_w + group * c + c0
        ue = jnp.dot(h_ext, w_ref[:, col:col + cw], preferred_element_type=F32)
        um = jnp.where(at_start, 0.0, pltpu.roll(ue, 1, axis=0)[HALO:HALO + tm])
        up = jnp.where(at_end, 0.0, pltpu.roll(ue, ext - 1, axis=0)[HALO:HALO + tm])
        w = cw_ref[group][:, c0:c0 + cw]
        return um * w[0:1] + ue[HALO:HALO + tm] * w[1:2] + up * w[2:3] + cb_ref[group][:, c0:c0 + cw]

    for c0 in range(0, c, cw):
        x0c_ref[:, c0:c0 + cw] = conv(0, c0)
        vg_ref[:, c0:c0 + cw] = conv(2, c0) * conv(1, c0)


def _in_proj(x, mod3, row_of_tile, g, w_in, qn, kn, cos, sin, conv_w, conv_b, seq, tm):
    t, d = x.shape
    kvw = N_KV_HEADS * HEAD_DIM
    attn_w = d // 2
    c = (w_in.shape[1] - attn_w - 2 * kvw) // 3
    cw = _pick_tile(c, 512, 128)
    pos_tiles = seq // tm
    hb = tm // HALO
    nhb = t // HALO
    w3 = conv_w.reshape(3, 3, c).transpose(1, 0, 2)
    b3 = conv_b.reshape(3, 1, c)
    mod_spec = lambda ch: pl.BlockSpec((1, 1, d), lambda i: (row_of_tile(i), 0, 3 + ch))
    const = lambda shape: pl.BlockSpec(shape, lambda i: (0,) * len(shape), pipeline_mode=pl.Buffered(1))
    kern = functools.partial(_in_proj_kernel, q_scale=HEAD_DIM ** -0.5 * math.log2(math.e),
                             tiles_per_seq=pos_tiles, cw=cw)
    return pl.pallas_call(
        kern,
        out_shape=(jax.ShapeDtypeStruct((t, attn_w), BF16),
                   jax.ShapeDtypeStruct((t, kvw), BF16),
                   jax.ShapeDtypeStruct((t, kvw), BF16),
                   jax.ShapeDtypeStruct((t, c), F32),
                   jax.ShapeDtypeStruct((t, c), F32)),
        grid=(t // tm,),
        in_specs=[pl.BlockSpec((tm, d), lambda i: (i, 0)),
                  pl.BlockSpec((HALO, d), lambda i: (jnp.maximum(i * hb - 1, 0), 0)),
                  pl.BlockSpec((HALO, d), lambda i: (jnp.minimum((i + 1) * hb, nhb - 1), 0)),
                  mod_spec(0), mod_spec(1),
                  const((1, d)), const(w_in.shape),
                  const((1, HEAD_DIM)), const((1, HEAD_DIM)),
                  pl.BlockSpec((tm, HEAD_DIM), lambda i: (i % pos_tiles, 0)),
                  pl.BlockSpec((tm, HEAD_DIM), lambda i: (i % pos_tiles, 0)),
                  const((3, 3, c)), const((3, 1, c))],
        out_specs=(pl.BlockSpec((tm, attn_w), lambda i: (i, 0)),
                   pl.BlockSpec((tm, kvw), lambda i: (i, 0)),
                   pl.BlockSpec((tm, kvw), lambda i: (i, 0)),
                   pl.BlockSpec((tm, c), lambda i: (i, 0)),
                   pl.BlockSpec((tm, c), lambda i: (i, 0))),
        compiler_params=_params("parallel"),
        name="in_proj",
    )(x, x, x, mod3, mod3, g.reshape(1, d), w_in, qn, kn, cos, sin, w3, b3)


def _ctx_kv_kernel(x_ref, shift_ref, scale_ref, g_ref, w_ref, kn_ref, k_ref, v_ref):
    y = _rms(x_ref[...]) * g_ref[...]
    h = (y * (1.0 + scale_ref[0]) + shift_ref[0]).astype(BF16)
    u = jnp.dot(h, w_ref[...], preferred_element_type=F32)
    for hd in range(N_KV_HEADS):
        k_ref[:, hd * HEAD_DIM:(hd + 1) * HEAD_DIM] = (
            _rms(u[:, hd * HEAD_DIM:(hd + 1) * HEAD_DIM]) * kn_ref[...]).astype(BF16)
    v_ref[...] = u[:, N_KV_HEADS * HEAD_DIM:].astype(BF16)


def _ctx_kv(x, mod3, ctx_row, g, w_kv, kn, tm):
    t, d = x.shape
    kvw = N_KV_HEADS * HEAD_DIM
    mod_spec = lambda c: pl.BlockSpec((1, 1, d), lambda i: (ctx_row, 0, 3 + c))
    return pl.pallas_call(
        _ctx_kv_kernel,
        out_shape=(jax.ShapeDtypeStruct((t, kvw), BF16), jax.ShapeDtypeStruct((t, kvw), BF16)),
        grid=(t // tm,),
        in_specs=[pl.BlockSpec((tm, d), lambda i: (i, 0)),
                  mod_spec(0), mod_spec(1),
                  pl.BlockSpec((1, d), lambda i: (0, 0)),
                  pl.BlockSpec((d, 2 * kvw), lambda i: (0, 0)),
                  pl.BlockSpec((1, HEAD_DIM), lambda i: (0, 0))],
        out_specs=(pl.BlockSpec((tm, kvw), lambda i: (i, 0)),
                   pl.BlockSpec((tm, kvw), lambda i: (i, 0))),
        compiler_params=_params("parallel"),
        name="ctx_kv",
    )(x, mod3, mod3, g.reshape(1, d), w_kv, kn)


def _attn_kernel(q_ref, k_ref, v_ref, o_ref, s0_ref, s1_ref, mx0_ref, mx1_ref, ls_ref, acc_ref, *,
                 group, nq):
    i = pl.program_id(2)
    tq = q_ref.shape[0]
    nk, _, tk = s0_ref.shape
    lanes = HEAD_DIM
    bufs = ((s0_ref, mx0_ref), (s1_ref, mx1_ref))

    def scores(j, q, s_ref, mx_ref):
        start = pl.multiple_of(j * tk, tk)
        s = lax.dot_general(q, k_ref[0, pl.ds(start, tk), :], (((1,), (1,)), ((), ())),
                            preferred_element_type=F32)
        s_ref[j] = s
        mx = mx_ref[...]
        for c in range(tk // lanes):
            mx = jnp.maximum(mx, s[:, c * lanes:(c + 1) * lanes])
        mx_ref[...] = mx

    def values(j, s_ref, mx_ref):
        start = pl.multiple_of(j * tk, tk)
        s = s_ref[j]
        m = mx_ref[...]
        ls = ls_ref[...]
        parts = []
        for c in range(tk // lanes):
            pc = jnp.exp2(s[:, c * lanes:(c + 1) * lanes] - m)
            ls = ls + pc
            parts.append(pc.astype(BF16))
        ls_ref[...] = ls
        p = jnp.concatenate(parts, axis=1)
        acc_ref[...] += jnp.dot(p, v_ref[0, pl.ds(start, tk), :], preferred_element_type=F32)

    def begin_scores(mx_ref):
        mx_ref[...] = jnp.full_like(mx_ref, -jnp.inf)
        return jnp.concatenate([q_ref[:, h * HEAD_DIM:(h + 1) * HEAD_DIM] for h in range(group)], axis=0)

    def end_scores(mx_ref):
        mx_ref[...] = jnp.broadcast_to(jnp.max(mx_ref[...], axis=-1, keepdims=True), mx_ref.shape)

    def begin_values():
        ls_ref[...] = jnp.zeros_like(ls_ref)
        acc_ref[...] = jnp.zeros_like(acc_ref)

    def end_values():
        o = acc_ref[...] / jnp.sum(ls_ref[...], axis=-1, keepdims=True)
        for h in range(group):
            o_ref[:, h * HEAD_DIM:(h + 1) * HEAD_DIM] = o[h * tq:(h + 1) * tq].astype(BF16)

    @pl.when(i == 0)
    def _():
        s_ref, mx_ref = bufs[0]
        q = begin_scores(mx_ref)
        lax.fori_loop(0, nk, lambda j, c: (scores(j, q, s_ref, mx_ref), c)[1], 0)
        end_scores(mx_ref)

    for parity in range(2):
        @pl.when(jnp.logical_and(jnp.logical_and(i > 0, i < nq), i % 2 == parity))
        def _():
            (cs_ref, cmx_ref), (ps_ref, pmx_ref) = bufs[parity], bufs[1 - parity]
            q = begin_scores(cmx_ref)
            begin_values()

            def both(j, c):
                scores(j, q, cs_ref, cmx_ref)
                values(j, ps_ref, pmx_ref)
                return c

            lax.fori_loop(0, nk, both, 0, unroll=True)
            end_scores(cmx_ref)
            end_values()

    @pl.when(i == nq)
    def _():
        s_ref, mx_ref = bufs[(nq - 1) % 2]
        begin_values()
        lax.fori_loop(0, nk, lambda j, c: (values(j, s_ref, mx_ref), c)[1], 0)
        end_values()


def _attention(q, k_all, v_all, batch, seq):
    t, attn_w = q.shape
    lk = k_all.shape[1]
    group = attn_w // HEAD_DIM // N_KV_HEADS
    gw = group * HEAD_DIM
    tq = _pick_tile(seq, 128, 8)
    tk = _pick_tile(lk, 1408, 128)
    nq = seq // tq
    rows = group * tq
    kern = functools.partial(_attn_kernel, group=group, nq=nq)
    return pl.pallas_call(
        kern,
        out_shape=jax.ShapeDtypeStruct((t, attn_w), BF16),
        grid=(batch, N_KV_HEADS, nq + 1),
        in_specs=[pl.BlockSpec((tq, gw), lambda b, g, i: (b * nq + jnp.minimum(i, nq - 1), g)),
                  pl.BlockSpec((1, lk, HEAD_DIM), lambda b, g, i: (b, 0, g)),
                  pl.BlockSpec((1, lk, HEAD_DIM), lambda b, g, i: (b, 0, g))],
        out_specs=pl.BlockSpec((tq, gw), lambda b, g, i: (b * nq + jnp.maximum(i - 1, 0), g)),
        scratch_shapes=[pltpu.VMEM((lk // tk, rows, tk), F32), pltpu.VMEM((lk // tk, rows, tk), F32),
                        pltpu.VMEM((rows, HEAD_DIM), F32), pltpu.VMEM((rows, HEAD_DIM), F32),
                        pltpu.VMEM((rows, HEAD_DIM), F32), pltpu.VMEM((rows, HEAD_DIM), F32)],
        compiler_params=_params("parallel", "parallel", "arbitrary"),
        name="attention",
    )(q, k_all, v_all)


def _dft_tables(n1, n2, k1p):
    n = n1 * n2
    k1 = np.arange(k1p)[:, None]
    a = np.arange(n1)[None, :]
    m2 = np.arange(n2)[:, None, None]
    theta = 2.0 * np.pi * ((((n2 * a * k1)[None] + m2 * k1[None]) % n) / n)
    k2 = np.arange(n2)[:, None]
    b = np.arange(n2)[None, :]
    phi = 2.0 * np.pi * ((k2 * b) % n2) / n2
    c, s = np.cos(phi), np.sin(phi)
    fmat = np.block([[c, s], [-s, c]])
    return np.cos(theta), np.sin(theta), fmat


def _mix_forward(cos, sin, n1h):
    n2, k1p, _ = cos.shape
    g = n2 // SUBLANES
    a = np.stack([cos[:, :, :n1h], -sin[:, :, :n1h]], axis=1).reshape(g, SUBLANES, 2, k1p, n1h)
    m = np.zeros((g, 2, k1p, SUBLANES, n1h, SUBLANES))
    for j in range(SUBLANES):
        m[:, :, :, j, :, j] = a[:, j]
    return m.reshape(g, 2 * k1p * SUBLANES, n1h * SUBLANES)


def _mix_inverse(cos, sin, n1h, n1):
    n2, k1p, _ = cos.shape
    g = n2 // SUBLANES
    k1 = np.arange(k1p)
    weight = np.where((k1 == 0) | (k1 == n1 // 2), 1.0, np.where(k1 < n1 // 2, 2.0, 0.0)) / (n1 * n2)
    a = np.stack([cos[:, :, :n1h], -sin[:, :, :n1h]], axis=1) * weight[None, None, :, None]
    a = a.reshape(g, SUBLANES, 2, k1p, n1h)
    m = np.zeros((g, n1h, SUBLANES, 2, k1p, SUBLANES))
    for j in range(SUBLANES):
        m[:, :, j, :, :, j] = a[:, j].transpose(0, 3, 1, 2)
    return m.reshape(g, n1h * SUBLANES, 2 * k1p * SUBLANES)


def _stage_a_kernel(m_ref, x_ref, yr_ref, yi_ref):
    _, n1h, sub, c = x_ref.shape
    k1p = yr_ref.shape[1]
    x = x_ref[0].reshape(n1h * sub, c).astype(BF16)
    y = jnp.dot(m_ref[0], x, preferred_element_type=F32)
    yr_ref[0] = y[:k1p * sub].reshape(k1p, sub, c)
    yi_ref[0] = y[k1p * sub:].reshape(k1p, sub, c)


def _stage_a(mix, vg4, k1p):
    b, n1h, n2, c = vg4.shape
    groups = n2 // SUBLANES
    return pl.pallas_call(
        _stage_a_kernel,
        out_shape=(jax.ShapeDtypeStruct((b, k1p, n2, c), F32),) * 2,
        grid=(groups, b),
        in_specs=[pl.BlockSpec((1,) + mix.shape[1:], lambda g, bi: (g, 0, 0)),
                  pl.BlockSpec((1, n1h, SUBLANES, c), lambda g, bi: (bi, 0, g, 0))],
        out_specs=(pl.BlockSpec((1, k1p, SUBLANES, c), lambda g, bi: (bi, 0, g, 0)),) * 2,
        compiler_params=_params("parallel", "parallel"),
        name="hyena_dft_a",
    )(mix, vg4)


def _stage_b_kernel(yr_ref, yi_ref, f_ref, ft_ref, kr_ref, ki_ref, ur_ref, ui_ref):
    kb, n2 = yr_ref.shape[1:3]
    for r in range(kb):
        y = jnp.concatenate([yr_ref[0, r], yi_ref[0, r]], axis=0).astype(BF16)
        z = jnp.dot(f_ref[...], y, preferred_element_type=F32)
        zr, zi = z[:n2], z[n2:]
        kr = kr_ref[r].astype(F32)
        ki = ki_ref[r].astype(F32)
        p = jnp.concatenate([zr * kr - zi * ki, zr * ki + zi * kr], axis=0).astype(BF16)
        u = jnp.dot(ft_ref[...], p, preferred_element_type=F32)
        ur_ref[0, r] = u[:n2]
        ui_ref[0, r] = u[n2:]


def _stage_b(yr, yi, fmat, fmat_t, kfr, kfi, kb):
    b, k1p, n2, c = yr.shape
    blk = pl.BlockSpec((1, kb, n2, c), lambda bi, k: (bi, k, 0, 0))
    flt = pl.BlockSpec((kb, n2, c), lambda bi, k: (k, 0, 0))
    mat = pl.BlockSpec((2 * n2, 2 * n2), lambda bi, k: (0, 0))
    return pl.pallas_call(
        _stage_b_kernel,
        out_shape=(jax.ShapeDtypeStruct((b, k1p, n2, c), F32),) * 2,
        grid=(b, k1p // kb),
        in_specs=[blk, blk, mat, mat, flt, flt],
        out_specs=(blk, blk),
        compiler_params=_params("parallel", "parallel"),
        name="hyena_dft_b",
    )(yr, yi, fmat, fmat_t, kfr, kfi)


def _stage_c_kernel(m_ref, ur_ref, ui_ref, vg_ref, x0_ref, hb_ref, g_ref, o_ref):
    _, k1p, sub, c = ur_ref.shape
    n1h = vg_ref.shape[1]
    u = jnp.concatenate([ur_ref[0].reshape(k1p * sub, c), ui_ref[0].reshape(k1p * sub, c)], axis=0)
    conv = jnp.dot(m_ref[0], u.astype(BF16), preferred_element_type=F32)
    vg = vg_ref[0].reshape(n1h * sub, c)
    x0 = x0_ref[0].reshape(n1h * sub, c)
    hyo = (conv + hb_ref[...] * vg) * x0
    o_ref[0] = (_rms(hyo) * g_ref[...]).reshape(n1h, sub, c)


def _stage_c(mix, ur, ui, vg4, x04, hy_bias, g_hy):
    b, n1h, n2, c = vg4.shape
    k1p = ur.shape[1]
    groups = n2 // SUBLANES
    ublk = pl.BlockSpec((1, k1p, SUBLANES, c), lambda g, bi: (bi, 0, g, 0))
    xblk = pl.BlockSpec((1, n1h, SUBLANES, c), lambda g, bi: (bi, 0, g, 0))
    vec = pl.BlockSpec((1, c), lambda g, bi: (0, 0))
    return pl.pallas_call(
        _stage_c_kernel,
        out_shape=jax.ShapeDtypeStruct((b, n1h, n2, c), F32),
        grid=(groups, b),
        in_specs=[pl.BlockSpec((1,) + mix.shape[1:], lambda g, bi: (g, 0, 0)),
                  ublk, ublk, xblk, xblk, vec, vec],
        out_specs=xblk,
        compiler_params=_params("parallel", "parallel"),
        name="hyena_dft_c",
    )(mix, ur, ui, vg4, x04, hy_bias.reshape(1, c), g_hy.reshape(1, c))


def _filter_a_kernel(a_ref, w1t_ref, w1c_ref, w1s_ref, b1_ref, w2_ref, b2_ref, w3_ref, b3_ref,
                     w4_ref, fr_ref, dec_ref, yr_ref, yi_ref, norm_ref, *, seq):
    g = pl.program_id(0)
    k1p = yr_ref.shape[0]
    c = norm_ref.shape[1]
    n1 = a_ref.shape[2]
    n1h = n1 // 2
    n2 = DFT_N2

    def lags(idx, j):
        tprime = idx * n2 + (g * SUBLANES + j)
        fwd = idx < n1h
        return tprime, fwd, jnp.where(fwd, tprime, 2 * seq - tprime).astype(F32)

    def stacked(idx, axis):
        parts = [lags(idx, j) for j in range(SUBLANES)]
        return [jnp.concatenate([p[k] for p in parts], axis=axis) for k in range(3)]

    _, _, pos_l = stacked(lax.broadcasted_iota(jnp.int32, (1, n1), 1), 1)
    tprime_r, fwd_r, pos_r = stacked(lax.broadcasted_iota(jnp.int32, (n1, 1), 0), 0)

    band = lax.broadcasted_iota(jnp.int32, (FILTER_BANDS, 1), 0).astype(F32)
    f = 1e-4 + band * ((FILTER_BANDS - 1 - 1e-4) / (FILTER_BANDS - 1))
    ang = f * ((2.0 * math.pi * pos_l) / seq)
    fr = fr_ref[...]
    dot = functools.partial(jnp.dot, preferred_element_type=F32, precision=_HI)
    h = w1t_ref[...] * (pos_l / (seq - 1.0)) + dot(w1c_ref[...], jnp.cos(ang)) + dot(w1s_ref[...], -jnp.sin(ang))
    h = jnp.sin(fr * (h + b1_ref[...]))
    h = jnp.sin(fr * (dot(w2_ref[...], h) + b2_ref[...]))
    h = jnp.sin(fr * (dot(w3_ref[...], h) + b3_ref[...]))
    h = lax.dot_general(h, w4_ref[...], (((0,), (0,)), ((), ())), preferred_element_type=F32)
    t_r = pos_r / (seq - 1.0)
    dec = jnp.abs(dec_ref[...])
    kf = h[:, :c] * jnp.exp(-t_r * dec[0:1, :])
    kb = h[:, c:] * jnp.exp(-t_r * dec[1:2, :])
    kk = jnp.where(fwd_r, kf, jnp.where(tprime_r == seq, 0.0, kb))

    @pl.when(g == 0)
    def _():
        norm_ref[...] = jnp.zeros_like(norm_ref)

    norm_ref[...] += jnp.sum(jnp.abs(kk), axis=0, keepdims=True)
    for j in range(SUBLANES):
        y = jnp.dot(a_ref[j], kk[j * n1:(j + 1) * n1].astype(BF16), preferred_element_type=F32)
        yr_ref[:, j * c:(j + 1) * c] = y[:k1p].astype(BF16)
        yi_ref[:, j * c:(j + 1) * c] = y[k1p:].astype(BF16)


def _filter_a(a_full, flt, seq, c):
    w1, b1, w2, b2, w3, b3, w4, freq, decay = flt
    hid = w2.shape[0]
    n2, k1p2, n1 = a_full.shape
    k1p = k1p2 // 2
    full = lambda shape: pl.BlockSpec(shape, lambda g: (0,) * len(shape))
    kern = functools.partial(_filter_a_kernel, seq=seq)
    return pl.pallas_call(
        kern,
        out_shape=(jax.ShapeDtypeStruct((k1p, n2 * c), BF16),
                   jax.ShapeDtypeStruct((k1p, n2 * c), BF16),
                   jax.ShapeDtypeStruct((1, c), F32)),
        grid=(n2 // SUBLANES,),
        in_specs=[pl.BlockSpec((SUBLANES, k1p2, n1), lambda g: (g, 0, 0)),
                  full((hid, 1)), full((hid, FILTER_BANDS)), full((hid, FILTER_BANDS)), full((hid, 1)),
                  full((hid, hid)), full((hid, 1)), full((hid, hid)), full((hid, 1)),
                  full((hid, 2 * c)), full((hid, 1)), full((2, c))],
        out_specs=(pl.BlockSpec((k1p, SUBLANES * c), lambda g: (0, g)),
                   pl.BlockSpec((k1p, SUBLANES * c), lambda g: (0, g)),
                   pl.BlockSpec((1, c), lambda g: (0, 0))),
        compiler_params=_params("arbitrary"),
        name="hyena_filter_a",
    )(a_full, w1[0:1].T, w1[1:1 + FILTER_BANDS].T, w1[1 + FILTER_BANDS:].T, b1.reshape(hid, 1),
      w2.T, b2.reshape(hid, 1), w3.T, b3.reshape(hid, 1), w4, freq.reshape(hid, 1), decay)


def _filter_b_kernel(yr_ref, yi_ref, f_ref, norm_ref, kr_ref, ki_ref):
    kb, n2 = yr_ref.shape[:2]
    for r in range(kb):
        y = jnp.concatenate([yr_ref[r], yi_ref[r]], axis=0)
        z = jnp.dot(f_ref[...], y, preferred_element_type=F32) / norm_ref[...]
        kr_ref[r] = z[:n2].astype(BF16)
        ki_ref[r] = z[n2:].astype(BF16)


def _filter_b(yr, yi, fmat, norm, kb):
    k1p, n2, c = yr.shape
    blk = pl.BlockSpec((kb, n2, c), lambda k: (k, 0, 0))
    return pl.pallas_call(
        _filter_b_kernel,
        out_shape=(jax.ShapeDtypeStruct((k1p, n2, c), BF16),) * 2,
        grid=(k1p // kb,),
        in_specs=[blk, blk,
                  pl.BlockSpec((2 * n2, 2 * n2), lambda k: (0, 0)),
                  pl.BlockSpec((1, c), lambda k: (0, 0))],
        out_specs=(blk, blk),
        compiler_params=_params("parallel"),
        name="hyena_filter_b",
    )(yr, yi, fmat, norm)


def _hyena(x0c, vg, flt, hy_bias, g_hy, batch, seq):
    c = vg.shape[1]
    n2 = DFT_N2
    n1h = seq // n2
    n1 = 2 * n1h
    k1p = min(n1, -(-(n1 // 2 + 1) // SUBLANES) * SUBLANES)
    kb = _pick_tile(k1p, 4, 1)
    cos, sin, fmat_np = _dft_tables(n1, n2, k1p)
    a_full = jnp.asarray(np.concatenate([cos, -sin], axis=1), BF16)
    mix_a = jnp.asarray(_mix_forward(cos, sin, n1h), BF16)
    mix_c = jnp.asarray(_mix_inverse(cos, sin, n1h, n1), BF16)
    fmat = jnp.asarray(fmat_np, BF16)
    fmat_t = jnp.asarray(fmat_np.T, BF16)

    fyr, fyi, norm = _filter_a(a_full, flt, seq, c)
    kfr, kfi = _filter_b(fyr.reshape(k1p, n2, c), fyi.reshape(k1p, n2, c), fmat, norm, kb)

    vg4 = vg.reshape(batch, n1h, n2, c)
    x04 = x0c.reshape(batch, n1h, n2, c)
    yr, yi = _stage_a(mix_a, vg4, k1p)
    ur, ui = _stage_b(yr, yi, fmat, fmat_t, kfr, kfi, kb)
    y = _stage_c(mix_c, ur, ui, vg4, x04, hy_bias, g_hy)
    return y.reshape(batch * seq, c)


def _out_proj_kernel(x_ref, gate_ref, shift_ref, scale_ref, attn_ref, hy_ref, ga_ref, gn_ref, wa_ref, wh_ref,
                     o_ref, h_ref):
    ya = (_rms(attn_ref[...].astype(F32)) * ga_ref[...]).astype(BF16)
    y = jnp.dot(ya, wa_ref[...], preferred_element_type=F32)
    y += jnp.dot(hy_ref[...].astype(BF16), wh_ref[...], preferred_element_type=F32)
    x = x_ref[...] + gate_ref[0] * y
    o_ref[...] = x
    h_ref[...] = (_rms(x) * gn_ref[...] * (1.0 + scale_ref[0]) + shift_ref[0]).astype(BF16)


def _out_proj(x, mod3, row_of_tile, attn, hyn, g_attn, g_next, w_out, tm):
    t, d = x.shape
    aw = attn.shape[1]
    hw = hyn.shape[1]
    mod_spec = lambda ch: pl.BlockSpec((1, 1, d), lambda i: (row_of_tile(i), 0, ch))
    const = lambda shape, idx: pl.BlockSpec(shape, lambda i: idx, pipeline_mode=pl.Buffered(1))
    return pl.pallas_call(
        _out_proj_kernel,
        out_shape=(jax.ShapeDtypeStruct((t, d), F32), jax.ShapeDtypeStruct((t, d), BF16)),
        grid=(t // tm,),
        in_specs=[pl.BlockSpec((tm, d), lambda i: (i, 0)),
                  mod_spec(5), mod_spec(6), mod_spec(7),
                  pl.BlockSpec((tm, aw), lambda i: (i, 0)),
                  pl.BlockSpec((tm, hw), lambda i: (i, 0)),
                  const((1, aw), (0, 0)), const((1, d), (0, 0)),
                  const((aw, d), (0, 0)), const((hw, d), (1, 0))],
        out_specs=(pl.BlockSpec((tm, d), lambda i: (i, 0)), pl.BlockSpec((tm, d), lambda i: (i, 0))),
        compiler_params=_params("parallel"),
        name="out_proj",
    )(x, mod3, mod3, mod3, attn, hyn, g_attn.reshape(1, aw), g_next.reshape(1, d), w_out, w_out)


def _rope_tables(seq):
    half = HEAD_DIM // 2
    pos = np.arange(seq)
    inv = ROPE_THETA ** (-np.arange(0, half, 2, dtype=np.float64) / half)
    ang = np.concatenate([(pos // GRID_W)[:, None] * inv, (pos % GRID_W)[:, None] * inv], axis=-1)
    cos = np.repeat(np.cos(ang), 2, axis=-1)
    sin = np.stack([-np.sin(ang), np.sin(ang)], axis=-1).reshape(seq, HEAD_DIM)
    return jnp.asarray(cos, F32), jnp.asarray(sin, F32)


def kernel(x, c, ctx, c_ctx, w_ada, b_ada, g_norm, w_ffn1_up, w_ffn1_down, w_ffn2_up, w_ffn2_down,
           w_in, q_norm, k_norm, conv_w, conv_b, flt_w1, flt_b1, flt_w2, flt_b2, flt_w3, flt_b3,
           flt_w4, flt_freq, flt_decay, hy_bias, g_out, w_out):
    batch, seq, d = x.shape
    lc = ctx.shape[1]
    depth = w_ada.shape[0]
    assert depth == 1, "context-update path of deeper stacks is not implemented"
    attn_w = d // 2
    kvw = N_KV_HEADS * HEAD_DIM
    q_heads = attn_w // HEAD_DIM
    tm = _pick_tile(seq, 512, 128)
    tmc = _pick_tile(batch * lc, 512, 8)
    tiles_per_seq = seq // tm
    lat_row = lambda i: i // tiles_per_seq
    ctx_row = lambda i: batch

    xs = x.reshape(batch * seq, d)
    cs = ctx.reshape(batch * lc, d)
    rows = -(-(batch + 1) // 8) * 8
    c_cols = jnp.zeros((d, rows), F32).at[:, :batch].set(c.T).at[:, batch].set(c_ctx)
    cos, sin = _rope_tables(seq)

    l = 0
    mod3 = _ada_mod(c_cols, batch + 1, w_ada[l], b_ada[l]).reshape(rows, 1, N_MOD * d)

    w1u, w1d = w_ffn1_up[l].astype(BF16), w_ffn1_down[l].astype(BF16)
    w2u, w2d = w_ffn2_up[l].astype(BF16), w_ffn2_down[l].astype(BF16)
    wi = w_in[l].astype(BF16)
    qkv_w = attn_w + 2 * kvw
    qn = q_norm[l].reshape(1, HEAD_DIM)
    kn = k_norm[l].reshape(1, HEAD_DIM)
    wo = w_out[l].astype(BF16)

    x1 = _ffn(xs, mod3, lat_row, 0, g_norm[l, 0], w1u, w1d, tm)
    c1 = _ffn(cs, mod3, ctx_row, 0, g_norm[l, 0], w1u, w1d, tmc)

    q, k, v, x0c, vg = _in_proj(x1, mod3, lat_row, g_norm[l, 1], wi, qn, kn,
                                cos, sin, conv_w[l], conv_b[l], seq, tm)
    kc, vc = _ctx_kv(c1, mod3, batch, g_norm[l, 1], wi[:, attn_w:qkv_w], kn, tmc)
    k_all = jnp.concatenate([kc.reshape(batch, lc, kvw), k.reshape(batch, seq, kvw)], axis=1)
    v_all = jnp.concatenate([vc.reshape(batch, lc, kvw), v.reshape(batch, seq, kvw)], axis=1)
    attn = _attention(q, k_all, v_all, batch, seq)

    flt = (flt_w1[l], flt_b1[l], flt_w2[l], flt_b2[l], flt_w3[l], flt_b3[l], flt_w4[l],
           flt_freq[l], flt_decay[l])
    hyn = _hyena(x0c, vg, flt, hy_bias[l], g_out[l, attn_w:], batch, seq)
    x2, h2 = _out_proj(x1, mod3, lat_row, attn, hyn, g_out[l, :attn_w], g_norm[l, 2], wo, tm)

    x3 = _ffn(x2, mod3, lat_row, 2, g_norm[l, 2], w2u, w2d, tm, h=h2)
    return x3.reshape(batch, seq, d)
```

```python
import functools
import math

import jax
import jax.numpy as jnp
import numpy as np
from jax import lax
from jax.experimental import pallas as pl
from jax.experimental.pallas import tpu as pltpu

F32 = jnp.float32
BF16 = jnp.bfloat16

HEAD_DIM = 128
N_KV_HEADS = 2
GRID_W = 64
ROPE_THETA = 10000.0
FILTER_BANDS = 16
RMS_EPS = 1e-6
N_MOD = 9

V7X_VMEM_LIMIT_BYTES = 56 * 1024 * 1024
SUBLANES = 8
FFN_ROWS = 1024
DFT_N2 = 128

_HI = lax.Precision.HIGHEST


def _params(*sem):
    return pltpu.CompilerParams(dimension_semantics=sem, vmem_limit_bytes=V7X_VMEM_LIMIT_BYTES)


def _pick_tile(n, cap, mult):
    best = None
    t = mult
    while t <= min(n, cap):
        if n % t == 0:
            best = t
        t += mult
    assert best is not None, (n, cap, mult)
    return best


def _rms(x, eps=RMS_EPS):
    return x * lax.rsqrt(jnp.mean(x * x, axis=-1, keepdims=True) + eps)


def _ada_kernel(ct_ref, w_ref, b_ref, o_ref, sb_ref, *, n_rows):
    lanes = sb_ref.shape[2]

    @pl.when(pl.program_id(0) == 0)
    def _():
        ct = ct_ref[...]
        s = ct * jax.nn.sigmoid(ct)
        for r in range(n_rows):
            sb_ref[r] = jnp.broadcast_to(s[:, r:r + 1], sb_ref.shape[1:])

    pad = jnp.zeros((o_ref.shape[0] - n_rows, lanes), F32)
    for lt in range(w_ref.shape[1] // lanes):
        w = w_ref[:, lt * lanes:(lt + 1) * lanes]
        rows = [jnp.sum(w * sb_ref[r], axis=0, keepdims=True) for r in range(n_rows)]
        o_ref[:, lt * lanes:(lt + 1) * lanes] = (jnp.concatenate(rows + [pad], axis=0)
                                                 + b_ref[:, lt * lanes:(lt + 1) * lanes])


def _ada_mod(c_cols, n_rows, w_ada, b_ada):
    d, rows = c_cols.shape
    n = w_ada.shape[1]
    tn = _pick_tile(n, 2048, 128)
    return pl.pallas_call(
        functools.partial(_ada_kernel, n_rows=n_rows),
        out_shape=jax.ShapeDtypeStruct((rows, n), F32),
        grid=(n // tn,),
        in_specs=[pl.BlockSpec((d, rows), lambda j: (0, 0)),
                  pl.BlockSpec((d, tn), lambda j: (0, j)),
                  pl.BlockSpec((1, tn), lambda j: (0, j))],
        out_specs=pl.BlockSpec((rows, tn), lambda j: (0, j)),
        scratch_shapes=[pltpu.VMEM((n_rows, d, HEAD_DIM), F32)],
        compiler_params=_params("arbitrary"),
        name="ada_mod",
    )(c_cols, w_ada, b_ada.reshape(1, n))


def _ffn_kernel(x_ref, shift_ref, scale_ref, gate_ref, g_ref, wg_ref, wu_ref, wd_ref, o_ref, h_ref, *, tn):
    k = pl.program_id(1)
    last = pl.num_programs(1) - 1
    d = o_ref.shape[1]

    def chunk(h, emit):
        g = jnp.dot(h, wg_ref[...], preferred_element_type=F32)
        u = jnp.dot(h, wu_ref[...], preferred_element_type=F32)
        a = (g * jax.nn.sigmoid(g) * u).astype(BF16)
        for n0 in range(0, d, tn):
            emit(n0, jnp.dot(a, wd_ref[:, n0:n0 + tn], preferred_element_type=F32))

    @pl.when(k == 0)
    def _():
        y = _rms(x_ref[...]) * g_ref[...]
        h = (y * (1.0 + scale_ref[0]) + shift_ref[0]).astype(BF16)
        h_ref[...] = h

        def emit(n0, part):
            o_ref[:, n0:n0 + tn] = part
        chunk(h, emit)

    @pl.when(jnp.logical_and(k > 0, k < last))
    def _():
        def emit(n0, part):
            o_ref[:, n0:n0 + tn] += part
        chunk(h_ref[...], emit)

    @pl.when(jnp.logical_and(k > 0, k == last))
    def _():
        def emit(n0, part):
            cols = slice(n0, n0 + tn)
            o_ref[:, cols] = x_ref[:, cols] + 0.5 * gate_ref[0][:, cols] * (o_ref[:, cols] + part)
        chunk(h_ref[...], emit)


def _ffn(x, mod3, row_of_tile, slot, g, w_up, w_down, tm):
    t, d = x.shape
    f = w_down.shape[0]
    tf = _pick_tile(f, 512, 128)
    nf = f // tf
    assert nf >= 2, "the kernel keeps separate first / last chunk paths"
    mod_spec = lambda c: pl.BlockSpec((1, 1, d), lambda i, k: (row_of_tile(i), 0, 3 * slot + c))
    row_tile = pl.BlockSpec((tm, d), lambda i, k: (i, 0))
    return pl.pallas_call(
        functools.partial(_ffn_kernel, tn=_pick_tile(d, 512, 128)),
        out_shape=jax.ShapeDtypeStruct((t, d), F32),
        grid=(t // tm, nf),
        in_specs=[row_tile, mod_spec(0), mod_spec(1), mod_spec(2),
                  pl.BlockSpec((1, d), lambda i, k: (0, 0)),
                  pl.BlockSpec((d, tf), lambda i, k: (0, k)),
                  pl.BlockSpec((d, tf), lambda i, k: (0, k + nf)),
                  pl.BlockSpec((tf, d), lambda i, k: (k, 0))],
        out_specs=row_tile,
        scratch_shapes=[pltpu.VMEM((tm, d), BF16)],
        compiler_params=_params("parallel", "arbitrary"),
        name="ffn",
    )(x, mod3, mod3, mod3, g.reshape(1, d), w_up, w_up, w_down)


def _rope(x, cos, sin):
    lane = lax.broadcasted_iota(jnp.int32, x.shape, 1)
    partner = jnp.where(lane % 2 == 0, pltpu.roll(x, HEAD_DIM - 1, axis=1), pltpu.roll(x, 1, axis=1))
    return x * cos + partner * sin


HALO = 16


def _in_proj_kernel(x_ref, xp_ref, xn_ref, shift_ref, scale_ref, g_ref, w_ref, qn_ref, kn_ref,
                    cos_ref, sin_ref, cw_ref, cb_ref, q_ref, k_ref, v_ref, x0c_ref, vg_ref, *,
                    q_scale, tiles_per_seq, cw):
    i = pl.program_id(0)
    tm = x_ref.shape[0]
    attn_w = q_ref.shape[1]
    kvw = k_ref.shape[1]
    c = x0c_ref.shape[1]
    first = (i % tiles_per_seq) == 0
    last = (i % tiles_per_seq) == tiles_per_seq - 1

    x_ext = jnp.concatenate([xp_ref[...], x_ref[...], xn_ref[...]], axis=0)
    y = _rms(x_ext) * g_ref[...]
    h_ext = (y * (1.0 + scale_ref[0]) + shift_ref[0]).astype(BF16)
    h = h_ext[HALO:HALO + tm]

    qkv_w = attn_w + 2 * kvw
    u = jnp.dot(h, w_ref[:, :qkv_w], preferred_element_type=F32)

    def normed(col, gain):
        return _rope(_rms(u[:, col:col + HEAD_DIM]) * gain, cos_ref[...], sin_ref[...])

    for hd in range(attn_w // HEAD_DIM):
        col = hd * HEAD_DIM
        q_ref[:, col:col + HEAD_DIM] = (normed(col, qn_ref[...]) * q_scale).astype(BF16)
    for hd in range(kvw // HEAD_DIM):
        col = hd * HEAD_DIM
        k_ref[:, col:col + HEAD_DIM] = normed(attn_w + col, kn_ref[...]).astype(BF16)
    v_ref[...] = u[:, attn_w + kvw:].astype(BF16)

    row = lax.broadcasted_iota(jnp.int32, (tm, cw), 0)
    at_start = jnp.logical_and(first, row == 0)
    at_end = jnp.logical_and(last, row == tm - 1)
    ext = tm + 2 * HALO

    def conv(group, c0):
        col = qkv_w + group * c + c0
        ue = jnp.dot(h_ext, w_ref[:, col:col + cw], preferred_element_type=F32)
        um = jnp.where(at_start, 0.0, pltpu.roll(ue, 1, axis=0)[HALO:HALO + tm])
        up = jnp.where(at_end, 0.0, pltpu.roll(ue, ext - 1, axis=0)[HALO:HALO + tm])
        w = cw_ref[group][:, c0:c0 + cw]
        return um * w[0:1] + ue[HALO:HALO + tm] * w[1:2] + up * w[2:3] + cb_ref[group][:, c0:c0 + cw]

    for c0 in range(0, c, cw):
        x0c_ref[:, c0:c0 + cw] = conv(0, c0)
        vg_ref[:, c0:c0 + cw] = conv(2, c0) * conv(1, c0)


def _in_proj(x, mod3, row_of_tile, g, w_in, qn, kn, cos, sin, conv_w, conv_b, seq, tm):
    t, d = x.shape
    kvw = N_KV_HEADS * HEAD_DIM
    attn_w = d // 2
    c = (w_in.shape[1] - attn_w - 2 * kvw) // 3
    cw = _pick_tile(c, 512, 128)
    pos_tiles = seq // tm
    hb = tm // HALO
    nhb = t // HALO
    w3 = conv_w.reshape(3, 3, c).transpose(1, 0, 2)
    b3 = conv_b.reshape(3, 1, c)
    mod_spec = lambda ch: pl.BlockSpec((1, 1, d), lambda i: (row_of_tile(i), 0, 3 + ch))
    const = lambda shape: pl.BlockSpec(shape, lambda i: (0,) * len(shape), pipeline_mode=pl.Buffered(1))
    kern = functools.partial(_in_proj_kernel, q_scale=HEAD_DIM ** -0.5 * math.log2(math.e),
                             tiles_per_seq=pos_tiles, cw=cw)
    return pl.pallas_call(
        kern,
        out_shape=(jax.ShapeDtypeStruct((t, attn_w), BF16),
                   jax.ShapeDtypeStruct((t, kvw), BF16),
                   jax.ShapeDtypeStruct((t, kvw), BF16),
                   jax.ShapeDtypeStruct((t, c), F32),
                   jax.ShapeDtypeStruct((t, c), F32)),
        grid=(t // tm,),
        in_specs=[pl.BlockSpec((tm, d), lambda i: (i, 0)),
                  pl.BlockSpec((HALO, d), lambda i: (jnp.maximum(i * hb - 1, 0), 0)),
                  pl.BlockSpec((HALO, d), lambda i: (jnp.minimum((i + 1) * hb, nhb - 1), 0)),
                  mod_spec(0), mod_spec(1),
                  const((1, d)), const(w_in.shape),
                  const((1, HEAD_DIM)), const((1, HEAD_DIM)),
                  pl.BlockSpec((tm, HEAD_DIM), lambda i: (i % pos_tiles, 0)),
                  pl.BlockSpec((tm, HEAD_DIM), lambda i: (i % pos_tiles, 0)),
                  const((3, 3, c)), const((3, 1, c))],
        out_specs=(pl.BlockSpec((tm, attn_w), lambda i: (i, 0)),
                   pl.BlockSpec((tm, kvw), lambda i: (i, 0)),
                   pl.BlockSpec((tm, kvw), lambda i: (i, 0)),
                   pl.BlockSpec((tm, c), lambda i: (i, 0)),
                   pl.BlockSpec((tm, c), lambda i: (i, 0))),
        compiler_params=_params("parallel"),
        name="in_proj",
    )(x, x, x, mod3, mod3, g.reshape(1, d), w_in, qn, kn, cos, sin, w3, b3)


def _ctx_kv_kernel(x_ref, shift_ref, scale_ref, g_ref, w_ref, kn_ref, k_ref, v_ref):
    y = _rms(x_ref[...]) * g_ref[...]
    h = (y * (1.0 + scale_ref[0]) + shift_ref[0]).astype(BF16)
    u = jnp.dot(h, w_ref[...], preferred_element_type=F32)
    for hd in range(N_KV_HEADS):
        k_ref[:, hd * HEAD_DIM:(hd + 1) * HEAD_DIM] = (
            _rms(u[:, hd * HEAD_DIM:(hd + 1) * HEAD_DIM]) * kn_ref[...]).astype(BF16)
    v_ref[...] = u[:, N_KV_HEADS * HEAD_DIM:].astype(BF16)


def _ctx_kv(x, mod3, ctx_row, g, w_kv, kn, tm):
    t, d = x.shape
    kvw = N_KV_HEADS * HEAD_DIM
    mod_spec = lambda c: pl.BlockSpec((1, 1, d), lambda i: (ctx_row, 0, 3 + c))
    return pl.pallas_call(
        _ctx_kv_kernel,
        out_shape=(jax.ShapeDtypeStruct((t, kvw), BF16), jax.ShapeDtypeStruct((t, kvw), BF16)),
        grid=(t // tm,),
        in_specs=[pl.BlockSpec((tm, d), lambda i: (i, 0)),
                  mod_spec(0), mod_spec(1),
                  pl.BlockSpec((1, d), lambda i: (0, 0)),
                  pl.BlockSpec((d, 2 * kvw), lambda i: (0, 0)),
                  pl.BlockSpec((1, HEAD_DIM), lambda i: (0, 0))],
        out_specs=(pl.BlockSpec((tm, kvw), lambda i: (i, 0)),
                   pl.BlockSpec((tm, kvw), lambda i: (i, 0))),
        compiler_params=_params("parallel"),
        name="ctx_kv",
    )(x, mod3, mod3, g.reshape(1, d), w_kv, kn)


def _attn_kernel(q_ref, k_ref, v_ref, o_ref, s0_ref, s1_ref, mx0_ref, mx1_ref, ls_ref, acc_ref, *,
                 group, nq):
    i = pl.program_id(2)
    tq = q_ref.shape[0]
    nk, _, tk = s0_ref.shape
    lanes = HEAD_DIM
    bufs = ((s0_ref, mx0_ref), (s1_ref, mx1_ref))

    def scores(j, q, s_ref, mx_ref):
        start = pl.multiple_of(j * tk, tk)
        s = lax.dot_general(q, k_ref[0, pl.ds(start, tk), :], (((1,), (1,)), ((), ())),
                            preferred_element_type=F32)
        s_ref[j] = s
        mx = mx_ref[...]
        for c in range(tk // lanes):
            mx = jnp.maximum(mx, s[:, c * lanes:(c + 1) * lanes])
        mx_ref[...] = mx

    def values(j, s_ref, mx_ref):
        start = pl.multiple_of(j * tk, tk)
        s = s_ref[j]
        m = mx_ref[...]
        ls = ls_ref[...]
        parts = []
        for c in range(tk // lanes):
            pc = jnp.exp2(s[:, c * lanes:(c + 1) * lanes] - m)
            ls = ls + pc
            parts.append(pc.astype(BF16))
        ls_ref[...] = ls
        p = jnp.concatenate(parts, axis=1)
        acc_ref[...] += jnp.dot(p, v_ref[0, pl.ds(start, tk), :], preferred_element_type=F32)

    def begin_scores(mx_ref):
        mx_ref[...] = jnp.full_like(mx_ref, -jnp.inf)
        return jnp.concatenate([q_ref[:, h * HEAD_DIM:(h + 1) * HEAD_DIM] for h in range(group)], axis=0)

    def end_scores(mx_ref):
        mx_ref[...] = jnp.broadcast_to(jnp.max(mx_ref[...], axis=-1, keepdims=True), mx_ref.shape)

    def begin_values():
        ls_ref[...] = jnp.zeros_like(ls_ref)
        acc_ref[...] = jnp.zeros_like(acc_ref)

    def end_values():
        o = acc_ref[...] / jnp.sum(ls_ref[...], axis=-1, keepdims=True)
        for h in range(group):
            o_ref[:, h * HEAD_DIM:(h + 1) * HEAD_DIM] = o[h * tq:(h + 1) * tq].astype(BF16)

    @pl.when(i == 0)
    def _():
        s_ref, mx_ref = bufs[0]
        q = begin_scores(mx_ref)
        lax.fori_loop(0, nk, lambda j, c: (scores(j, q, s_ref, mx_ref), c)[1], 0)
        end_scores(mx_ref)

    for parity in range(2):
        @pl.when(jnp.logical_and(jnp.logical_and(i > 0, i < nq), i % 2 == parity))
        def _():
            (cs_ref, cmx_ref), (ps_ref, pmx_ref) = bufs[parity], bufs[1 - parity]
            q = begin_scores(cmx_ref)
            begin_values()

            def both(j, c):
                scores(j, q, cs_ref, cmx_ref)
                values(j, ps_ref, pmx_ref)
                return c

            lax.fori_loop(0, nk, both, 0, unroll=True)
            end_scores(cmx_ref)
            end_values()

    @pl.when(i == nq)
    def _():
        s_ref, mx_ref = bufs[(nq - 1) % 2]
        begin_values()
        lax.fori_loop(0, nk, lambda j, c: (values(j, s_ref, mx_ref), c)[1], 0)
        end_values()


def _attention(q, k_all, v_all, batch, seq):
    t, attn_w = q.shape
    lk = k_all.shape[1]
    group = attn_w // HEAD_DIM // N_KV_HEADS
    gw = group * HEAD_DIM
    tq = _pick_tile(seq, 128, 8)
    tk = _pick_tile(lk, 1408, 128)
    nq = seq // tq
    rows = group * tq
    kern = functools.partial(_attn_kernel, group=group, nq=nq)
    return pl.pallas_call(
        kern,
        out_shape=jax.ShapeDtypeStruct((t, attn_w), BF16),
        grid=(batch, N_KV_HEADS, nq + 1),
        in_specs=[pl.BlockSpec((tq, gw), lambda b, g, i: (b * nq + jnp.minimum(i, nq - 1), g)),
                  pl.BlockSpec((1, lk, HEAD_DIM), lambda b, g, i: (b, 0, g)),
                  pl.BlockSpec((1, lk, HEAD_DIM), lambda b, g, i: (b, 0, g))],
        out_specs=pl.BlockSpec((tq, gw), lambda b, g, i: (b * nq + jnp.maximum(i - 1, 0), g)),
        scratch_shapes=[pltpu.VMEM((lk // tk, rows, tk), F32), pltpu.VMEM((lk // tk, rows, tk), F32),
                        pltpu.VMEM((rows, HEAD_DIM), F32), pltpu.VMEM((rows, HEAD_DIM), F32),
                        pltpu.VMEM((rows, HEAD_DIM), F32), pltpu.VMEM((rows, HEAD_DIM), F32)],
        compiler_params=_params("parallel", "parallel", "arbitrary"),
        name="attention",
    )(q, k_all, v_all)


def _dft_tables(n1, n2, k1p):
    n = n1 * n2
    k1 = np.arange(k1p)[:, None]
    a = np.arange(n1)[None, :]
    m2 = np.arange(n2)[:, None, None]
    theta = 2.0 * np.pi * ((((n2 * a * k1)[None] + m2 * k1[None]) % n) / n)
    k2 = np.arange(n2)[:, None]
    b = np.arange(n2)[None, :]
    phi = 2.0 * np.pi * ((k2 * b) % n2) / n2
    c, s = np.cos(phi), np.sin(phi)
    fmat = np.block([[c, s], [-s, c]])
    return np.cos(theta), np.sin(theta), fmat


def _mix_forward(cos, sin, n1h):
    n2, k1p, _ = cos.shape
    g = n2 // SUBLANES
    a = np.stack([cos[:, :, :n1h], -sin[:, :, :n1h]], axis=1).reshape(g, SUBLANES, 2, k1p, n1h)
    m = np.zeros((g, 2, k1p, SUBLANES, n1h, SUBLANES))
    for j in range(SUBLANES):
        m[:, :, :, j, :, j] = a[:, j]
    return m.reshape(g, 2 * k1p * SUBLANES, n1h * SUBLANES)


def _mix_inverse(cos, sin, n1h, n1):
    n2, k1p, _ = cos.shape
    g = n2 // SUBLANES
    k1 = np.arange(k1p)
    weight = np.where((k1 == 0) | (k1 == n1 // 2), 1.0, np.where(k1 < n1 // 2, 2.0, 0.0)) / (n1 * n2)
    a = np.stack([cos[:, :, :n1h], -sin[:, :, :n1h]], axis=1) * weight[None, None, :, None]
    a = a.reshape(g, SUBLANES, 2, k1p, n1h)
    m = np.zeros((g, n1h, SUBLANES, 2, k1p, SUBLANES))
    for j in range(SUBLANES):
        m[:, :, j, :, :, j] = a[:, j].transpose(0, 3, 1, 2)
    return m.reshape(g, n1h * SUBLANES, 2 * k1p * SUBLANES)


PACKED_ROWS = 2 * SUBLANES


def _stage_a_kernel(m_ref, x_ref, yr_ref, yi_ref):
    _, n1h, _, c = x_ref.shape
    k1p = yr_ref.shape[1]
    re, im = [], []
    for half in range(2):
        rows = slice(half * SUBLANES, (half + 1) * SUBLANES)
        x = x_ref[0, :, rows, :].reshape(n1h * SUBLANES, c).astype(BF16)
        y = jnp.dot(m_ref[half], x, preferred_element_type=F32)
        re.append(y[:k1p * SUBLANES].reshape(k1p, SUBLANES, c))
        im.append(y[k1p * SUBLANES:].reshape(k1p, SUBLANES, c))
    yr_ref[0] = jnp.concatenate(re, axis=1).astype(BF16)
    yi_ref[0] = jnp.concatenate(im, axis=1).astype(BF16)


def _stage_a(mix, vg4, k1p):
    b, n1h, n2, c = vg4.shape
    return pl.pallas_call(
        _stage_a_kernel,
        out_shape=(jax.ShapeDtypeStruct((b, k1p, n2, c), BF16),) * 2,
        grid=(n2 // PACKED_ROWS, b),
        in_specs=[pl.BlockSpec((2,) + mix.shape[1:], lambda g, bi: (g, 0, 0)),
                  pl.BlockSpec((1, n1h, PACKED_ROWS, c), lambda g, bi: (bi, 0, g, 0))],
        out_specs=(pl.BlockSpec((1, k1p, PACKED_ROWS, c), lambda g, bi: (bi, 0, g, 0)),) * 2,
        compiler_params=_params("parallel", "parallel"),
        name="hyena_dft_a",
    )(mix, vg4)


def _stage_b_kernel(yr_ref, yi_ref, f_ref, ft_ref, kr_ref, ki_ref, ur_ref, ui_ref):
    kb, n2 = yr_ref.shape[1:3]
    for r in range(kb):
        y = jnp.concatenate([yr_ref[0, r], yi_ref[0, r]], axis=0)
        z = jnp.dot(f_ref[...], y, preferred_element_type=F32)
        zr, zi = z[:n2], z[n2:]
        kr = kr_ref[r].astype(F32)
        ki = ki_ref[r].astype(F32)
        p = jnp.concatenate([zr * kr - zi * ki, zr * ki + zi * kr], axis=0).astype(BF16)
        u = jnp.dot(ft_ref[...], p, preferred_element_type=F32)
        ur_ref[0, r] = u[:n2].astype(BF16)
        ui_ref[0, r] = u[n2:].astype(BF16)


def _stage_b(yr, yi, fmat, fmat_t, kfr, kfi, kb):
    b, k1p, n2, c = yr.shape
    blk = pl.BlockSpec((1, kb, n2, c), lambda k, bi: (bi, k, 0, 0))
    flt = pl.BlockSpec((kb, n2, c), lambda k, bi: (k, 0, 0))
    mat = pl.BlockSpec((2 * n2, 2 * n2), lambda k, bi: (0, 0))
    return pl.pallas_call(
        _stage_b_kernel,
        out_shape=(jax.ShapeDtypeStruct((b, k1p, n2, c), BF16),) * 2,
        grid=(k1p // kb, b),
        in_specs=[blk, blk, mat, mat, flt, flt],
        out_specs=(blk, blk),
        compiler_params=_params("parallel", "parallel"),
        name="hyena_dft_b",
    )(yr, yi, fmat, fmat_t, kfr, kfi)


def _stage_c_kernel(m_ref, ur_ref, ui_ref, vg_ref, x0_ref, hb_ref, g_ref, o_ref):
    _, k1p, _, c = ur_ref.shape
    n1h = vg_ref.shape[1]
    ur = ur_ref[0].astype(F32)
    ui = ui_ref[0].astype(F32)
    for half in range(2):
        rows = slice(half * SUBLANES, (half + 1) * SUBLANES)
        u = jnp.concatenate([ur[:, rows, :].reshape(k1p * SUBLANES, c),
                             ui[:, rows, :].reshape(k1p * SUBLANES, c)], axis=0).astype(BF16)
        conv = jnp.dot(m_ref[half], u, preferred_element_type=F32)
        vg = vg_ref[0, :, rows, :].reshape(n1h * SUBLANES, c)
        x0 = x0_ref[0, :, rows, :].reshape(n1h * SUBLANES, c)
        hyo = (conv + hb_ref[...] * vg) * x0
        o_ref[0, :, rows, :] = (_rms(hyo) * g_ref[...]).reshape(n1h, SUBLANES, c)


def _stage_c(mix, ur, ui, vg4, x04, hy_bias, g_hy):
    b, n1h, n2, c = vg4.shape
    k1p = ur.shape[1]
    ublk = pl.BlockSpec((1, k1p, PACKED_ROWS, c), lambda g, bi: (bi, 0, g, 0))
    xblk = pl.BlockSpec((1, n1h, PACKED_ROWS, c), lambda g, bi: (bi, 0, g, 0))
    vec = pl.BlockSpec((1, c), lambda g, bi: (0, 0))
    return pl.pallas_call(
        _stage_c_kernel,
        out_shape=jax.ShapeDtypeStruct((b, n1h, n2, c), F32),
        grid=(n2 // PACKED_ROWS, b),
        in_specs=[pl.BlockSpec((2,) + mix.shape[1:], lambda g, bi: (g, 0, 0)),
                  ublk, ublk, xblk, xblk, vec, vec],
        out_specs=xblk,
        compiler_params=_params("parallel", "parallel"),
        name="hyena_dft_c",
    )(mix, ur, ui, vg4, x04, hy_bias.reshape(1, c), g_hy.reshape(1, c))


def _filter_a_kernel(a_ref, w1t_ref, w1c_ref, w1s_ref, b1_ref, w2_ref, b2_ref, w3_ref, b3_ref,
                     w4_ref, fr_ref, dec_ref, yr_ref, yi_ref, norm_ref, *, seq):
    g = pl.program_id(0)
    k1p = yr_ref.shape[0]
    c = norm_ref.shape[1]
    n1 = a_ref.shape[2]
    n1h = n1 // 2
    n2 = DFT_N2

    def lags(idx, j):
        tprime = idx * n2 + (g * SUBLANES + j)
        fwd = idx < n1h
        return tprime, fwd, jnp.where(fwd, tprime, 2 * seq - tprime).astype(F32)

    def stacked(idx, axis):
        parts = [lags(idx, j) for j in range(SUBLANES)]
        return [jnp.concatenate([p[k] for p in parts], axis=axis) for k in range(3)]

    _, _, pos_l = stacked(lax.broadcasted_iota(jnp.int32, (1, n1), 1), 1)
    tprime_r, fwd_r, pos_r = stacked(lax.broadcasted_iota(jnp.int32, (n1, 1), 0), 0)

    band = lax.broadcasted_iota(jnp.int32, (FILTER_BANDS, 1), 0).astype(F32)
    f = 1e-4 + band * ((FILTER_BANDS - 1 - 1e-4) / (FILTER_BANDS - 1))
    ang = f * ((2.0 * math.pi * pos_l) / seq)
    fr = fr_ref[...]
    dot = functools.partial(jnp.dot, preferred_element_type=F32, precision=_HI)
    h = w1t_ref[...] * (pos_l / (seq - 1.0)) + dot(w1c_ref[...], jnp.cos(ang)) + dot(w1s_ref[...], -jnp.sin(ang))
    h = jnp.sin(fr * (h + b1_ref[...]))
    h = jnp.sin(fr * (dot(w2_ref[...], h) + b2_ref[...]))
    h = jnp.sin(fr * (dot(w3_ref[...], h) + b3_ref[...]))
    h = lax.dot_general(h, w4_ref[...], (((0,), (0,)), ((), ())), preferred_element_type=F32)
    t_r = pos_r / (seq - 1.0)
    dec = jnp.abs(dec_ref[...])
    kf = h[:, :c] * jnp.exp(-t_r * dec[0:1, :])
    kb = h[:, c:] * jnp.exp(-t_r * dec[1:2, :])
    kk = jnp.where(fwd_r, kf, jnp.where(tprime_r == seq, 0.0, kb))

    @pl.when(g == 0)
    def _():
        norm_ref[...] = jnp.zeros_like(norm_ref)

    norm_ref[...] += jnp.sum(jnp.abs(kk), axis=0, keepdims=True)
    for j in range(SUBLANES):
        y = jnp.dot(a_ref[j], kk[j * n1:(j + 1) * n1].astype(BF16), preferred_element_type=F32)
        yr_ref[:, j * c:(j + 1) * c] = y[:k1p].astype(BF16)
        yi_ref[:, j * c:(j + 1) * c] = y[k1p:].astype(BF16)


def _filter_a(a_full, flt, seq, c):
    w1, b1, w2, b2, w3, b3, w4, freq, decay = flt
    hid = w2.shape[0]
    n2, k1p2, n1 = a_full.shape
    k1p = k1p2 // 2
    full = lambda shape: pl.BlockSpec(shape, lambda g: (0,) * len(shape))
    kern = functools.partial(_filter_a_kernel, seq=seq)
    return pl.pallas_call(
        kern,
        out_shape=(jax.ShapeDtypeStruct((k1p, n2 * c), BF16),
                   jax.ShapeDtypeStruct((k1p, n2 * c), BF16),
                   jax.ShapeDtypeStruct((1, c), F32)),
        grid=(n2 // SUBLANES,),
        in_specs=[pl.BlockSpec((SUBLANES, k1p2, n1), lambda g: (g, 0, 0)),
                  full((hid, 1)), full((hid, FILTER_BANDS)), full((hid, FILTER_BANDS)), full((hid, 1)),
                  full((hid, hid)), full((hid, 1)), full((hid, hid)), full((hid, 1)),
                  full((hid, 2 * c)), full((hid, 1)), full((2, c))],
        out_specs=(pl.BlockSpec((k1p, SUBLANES * c), lambda g: (0, g)),
                   pl.BlockSpec((k1p, SUBLANES * c), lambda g: (0, g)),
                   pl.BlockSpec((1, c), lambda g: (0, 0))),
        compiler_params=_params("arbitrary"),
        name="hyena_filter_a",
    )(a_full, w1[0:1].T, w1[1:1 + FILTER_BANDS].T, w1[1 + FILTER_BANDS:].T, b1.reshape(hid, 1),
      w2.T, b2.reshape(hid, 1), w3.T, b3.reshape(hid, 1), w4, freq.reshape(hid, 1), decay)


def _filter_b_kernel(yr_ref, yi_ref, f_ref, norm_ref, kr_ref, ki_ref):
    kb, n2 = yr_ref.shape[:2]
    for r in range(kb):
        y = jnp.concatenate([yr_ref[r], yi_ref[r]], axis=0)
        z = jnp.dot(f_ref[...], y, preferred_element_type=F32) / norm_ref[...]
        kr_ref[r] = z[:n2].astype(BF16)
        ki_ref[r] = z[n2:].astype(BF16)


def _filter_b(yr, yi, fmat, norm, kb):
    k1p, n2, c = yr.shape
    blk = pl.BlockSpec((kb, n2, c), lambda k: (k, 0, 0))
    return pl.pallas_call(
        _filter_b_kernel,
        out_shape=(jax.ShapeDtypeStruct((k1p, n2, c), BF16),) * 2,
        grid=(k1p // kb,),
        in_specs=[blk, blk,
                  pl.BlockSpec((2 * n2, 2 * n2), lambda k: (0, 0)),
                  pl.BlockSpec((1, c), lambda k: (0, 0))],
        out_specs=(blk, blk),
        compiler_params=_params("parallel"),
        name="hyena_filter_b",
    )(yr, yi, fmat, norm)


def _hyena(x0c, vg, flt, hy_bias, g_hy, batch, seq):
    c = vg.shape[1]
    n2 = DFT_N2
    n1h = seq // n2
    n1 = 2 * n1h
    k1p = min(n1, -(-(n1 // 2 + 1) // SUBLANES) * SUBLANES)
    kb = _pick_tile(k1p, 4, 1)
    cos, sin, fmat_np = _dft_tables(n1, n2, k1p)
    a_full = jnp.asarray(np.concatenate([cos, -sin], axis=1), BF16)
    mix_a = jnp.asarray(_mix_forward(cos, sin, n1h), BF16)
    mix_c = jnp.asarray(_mix_inverse(cos, sin, n1h, n1), BF16)
    fmat = jnp.asarray(fmat_np, BF16)
    fmat_t = jnp.asarray(fmat_np.T, BF16)

    fyr, fyi, norm = _filter_a(a_full, flt, seq, c)
    kfr, kfi = _filter_b(fyr.reshape(k1p, n2, c), fyi.reshape(k1p, n2, c), fmat, norm, kb)

    vg4 = vg.reshape(batch, n1h, n2, c)
    x04 = x0c.reshape(batch, n1h, n2, c)
    yr, yi = _stage_a(mix_a, vg4, k1p)
    ur, ui = _stage_b(yr, yi, fmat, fmat_t, kfr, kfi, kb)
    y = _stage_c(mix_c, ur, ui, vg4, x04, hy_bias, g_hy)
    return y.reshape(batch * seq, c)


def _out_proj_kernel(x_ref, gate_ref, attn_ref, hy_ref, ga_ref, wa_ref, wh_ref, o_ref):
    ya = (_rms(attn_ref[...].astype(F32)) * ga_ref[...]).astype(BF16)
    y = jnp.dot(ya, wa_ref[...], preferred_element_type=F32)
    y += jnp.dot(hy_ref[...].astype(BF16), wh_ref[...], preferred_element_type=F32)
    o_ref[...] = x_ref[...] + gate_ref[0] * y


def _out_proj(x, mod3, row_of_tile, attn, hyn, g_attn, w_out, tm):
    t, d = x.shape
    aw = attn.shape[1]
    hw = hyn.shape[1]
    const = lambda shape, idx: pl.BlockSpec(shape, lambda i: idx, pipeline_mode=pl.Buffered(1))
    return pl.pallas_call(
        _out_proj_kernel,
        out_shape=jax.ShapeDtypeStruct((t, d), F32),
        grid=(t // tm,),
        in_specs=[pl.BlockSpec((tm, d), lambda i: (i, 0)),
                  pl.BlockSpec((1, 1, d), lambda i: (row_of_tile(i), 0, 5)),
                  pl.BlockSpec((tm, aw), lambda i: (i, 0)),
                  pl.BlockSpec((tm, hw), lambda i: (i, 0)),
                  const((1, aw), (0, 0)), const((aw, d), (0, 0)), const((hw, d), (1, 0))],
        out_specs=pl.BlockSpec((tm, d), lambda i: (i, 0)),
        compiler_params=_params("parallel"),
        name="out_proj",
    )(x, mod3, attn, hyn, g_attn.reshape(1, aw), w_out, w_out)


def _rope_tables(seq):
    half = HEAD_DIM // 2
    pos = np.arange(seq)
    inv = ROPE_THETA ** (-np.arange(0, half, 2, dtype=np.float64) / half)
    ang = np.concatenate([(pos // GRID_W)[:, None] * inv, (pos % GRID_W)[:, None] * inv], axis=-1)
    cos = np.repeat(np.cos(ang), 2, axis=-1)
    sin = np.stack([-np.sin(ang), np.sin(ang)], axis=-1).reshape(seq, HEAD_DIM)
    return jnp.asarray(cos, F32), jnp.asarray(sin, F32)


def kernel(x, c, ctx, c_ctx, w_ada, b_ada, g_norm, w_ffn1_up, w_ffn1_down, w_ffn2_up, w_ffn2_down,
           w_in, q_norm, k_norm, conv_w, conv_b, flt_w1, flt_b1, flt_w2, flt_b2, flt_w3, flt_b3,
           flt_w4, flt_freq, flt_decay, hy_bias, g_out, w_out):
    batch, seq, d = x.shape
    lc = ctx.shape[1]
    depth = w_ada.shape[0]
    assert depth == 1, "context-update path of deeper stacks is not implemented"
    attn_w = d // 2
    kvw = N_KV_HEADS * HEAD_DIM
    tm = _pick_tile(seq, 512, 128)
    tmf = _pick_tile(seq, FFN_ROWS, 128)
    tmc = _pick_tile(batch * lc, 512, 8)
    lat_row = lambda i: i // (seq // tm)
    lat_row_ffn = lambda i: i // (seq // tmf)
    ctx_row = lambda i: batch

    xs = x.reshape(batch * seq, d)
    cs = ctx.reshape(batch * lc, d)
    rows = -(-(batch + 1) // 8) * 8
    c_cols = jnp.zeros((d, rows), F32).at[:, :batch].set(c.T).at[:, batch].set(c_ctx)
    cos, sin = _rope_tables(seq)

    l = 0
    mod3 = _ada_mod(c_cols, batch + 1, w_ada[l], b_ada[l]).reshape(rows, 1, N_MOD * d)

    w1u, w1d = w_ffn1_up[l].astype(BF16), w_ffn1_down[l].astype(BF16)
    w2u, w2d = w_ffn2_up[l].astype(BF16), w_ffn2_down[l].astype(BF16)
    wi = w_in[l].astype(BF16)
    qkv_w = attn_w + 2 * kvw
    qn = q_norm[l].reshape(1, HEAD_DIM)
    kn = k_norm[l].reshape(1, HEAD_DIM)
    wo = w_out[l].astype(BF16)

    x1 = _ffn(xs, mod3, lat_row_ffn, 0, g_norm[l, 0], w1u, w1d, tmf)
    c1 = _ffn(cs, mod3, ctx_row, 0, g_norm[l, 0], w1u, w1d, tmc)

    q, k, v, x0c, vg = _in_proj(x1, mod3, lat_row, g_norm[l, 1], wi, qn, kn,
                                cos, sin, conv_w[l], conv_b[l], seq, tm)
    kc, vc = _ctx_kv(c1, mod3, batch, g_norm[l, 1], wi[:, attn_w:qkv_w], kn, tmc)
    k_all = jnp.concatenate([kc.reshape(batch, lc, kvw), k.reshape(batch, seq, kvw)], axis=1)
    v_all = jnp.concatenate([vc.reshape(batch, lc, kvw), v.reshape(batch, seq, kvw)], axis=1)
    attn = _attention(q, k_all, v_all, batch, seq)

    flt = (flt_w1[l], flt_b1[l], flt_w2[l], flt_b2[l], flt_w3[l], flt_b3[l], flt_w4[l],
           flt_freq[l], flt_decay[l])
    hyn = _hyena(x0c, vg, flt, hy_bias[l], g_out[l, attn_w:], batch, seq)
    x2 = _out_proj(x1, mod3, lat_row, attn, hyn, g_out[l, :attn_w], wo, tm)

    x3 = _ffn(x2, mod3, lat_row_ffn, 2, g_norm[l, 2], w2u, w2d, tmf)
    return x3.reshape(batch, seq, d)
```

```python
import functools
import math

import jax
import jax.numpy as jnp
import numpy as np
from jax import lax
from jax.experimental import pallas as pl
from jax.experimental.pallas import tpu as pltpu

F32 = jnp.float32
BF16 = jnp.bfloat16

HEAD_DIM = 128
N_KV_HEADS = 2
GRID_W = 64
ROPE_THETA = 10000.0
FILTER_BANDS = 16
RMS_EPS = 1e-6
N_MOD = 9

V7X_VMEM_LIMIT_BYTES = 56 * 1024 * 1024
SUBLANES = 8
FFN_ROWS = 1024
DFT_N2 = 128

_HI = lax.Precision.HIGHEST


def _params(*sem):
    return pltpu.CompilerParams(dimension_semantics=sem, vmem_limit_bytes=V7X_VMEM_LIMIT_BYTES)


def _pick_tile(n, cap, mult):
    best = None
    t = mult
    while t <= min(n, cap):
        if n % t == 0:
            best = t
        t += mult
    assert best is not None, (n, cap, mult)
    return best


def _rms(x, eps=RMS_EPS):
    return x * lax.rsqrt(jnp.mean(x * x, axis=-1, keepdims=True) + eps)


def _ada_kernel(ct_ref, w_ref, b_ref, o_ref, sb_ref, *, n_rows):
    lanes = sb_ref.shape[2]

    @pl.when(pl.program_id(0) == 0)
    def _():
        ct = ct_ref[...]
        s = ct * jax.nn.sigmoid(ct)
        for r in range(n_rows):
            sb_ref[r] = jnp.broadcast_to(s[:, r:r + 1], sb_ref.shape[1:])

    d, tn = w_ref.shape
    reps = tn // lanes
    acc = [jnp.zeros((SUBLANES, tn), F32) for _ in range(n_rows)]
    for kb in range(d // SUBLANES):
        rows = slice(kb * SUBLANES, (kb + 1) * SUBLANES)
        w = w_ref[rows, :]
        for r in range(n_rows):
            acc[r] = acc[r] + w * jnp.concatenate([sb_ref[r, rows, :]] * reps, axis=1)
    out = [jnp.sum(a, axis=0, keepdims=True) for a in acc]
    out.append(jnp.zeros((o_ref.shape[0] - n_rows, tn), F32))
    o_ref[...] = jnp.concatenate(out, axis=0) + b_ref[...]


def _ada_mod(c_cols, n_rows, w_ada, b_ada):
    d, rows = c_cols.shape
    n = w_ada.shape[1]
    tn = _pick_tile(n, 2048, 128)
    return pl.pallas_call(
        functools.partial(_ada_kernel, n_rows=n_rows),
        out_shape=jax.ShapeDtypeStruct((rows, n), F32),
        grid=(n // tn,),
        in_specs=[pl.BlockSpec((d, rows), lambda j: (0, 0)),
                  pl.BlockSpec((d, tn), lambda j: (0, j)),
                  pl.BlockSpec((1, tn), lambda j: (0, j))],
        out_specs=pl.BlockSpec((rows, tn), lambda j: (0, j)),
        scratch_shapes=[pltpu.VMEM((n_rows, d, HEAD_DIM), F32)],
        compiler_params=_params("arbitrary"),
        name="ada_mod",
    )(c_cols, w_ada, b_ada.reshape(1, n))


def _ffn_kernel(x_ref, shift_ref, scale_ref, gate_ref, g_ref, wg_ref, wu_ref, wd_ref, o_ref, h_ref, *, tn):
    k = pl.program_id(1)
    last = pl.num_programs(1) - 1
    d = o_ref.shape[1]

    def chunk(h, emit):
        g = jnp.dot(h, wg_ref[...], preferred_element_type=F32)
        u = jnp.dot(h, wu_ref[...], preferred_element_type=F32)
        a = (g * jax.nn.sigmoid(g) * u).astype(BF16)
        for n0 in range(0, d, tn):
            emit(n0, jnp.dot(a, wd_ref[:, n0:n0 + tn], preferred_element_type=F32))

    @pl.when(k == 0)
    def _():
        y = _rms(x_ref[...]) * g_ref[...]
        h = (y * (1.0 + scale_ref[0]) + shift_ref[0]).astype(BF16)
        h_ref[...] = h

        def emit(n0, part):
            o_ref[:, n0:n0 + tn] = part
        chunk(h, emit)

    @pl.when(jnp.logical_and(k > 0, k < last))
    def _():
        def emit(n0, part):
            o_ref[:, n0:n0 + tn] += part
        chunk(h_ref[...], emit)

    @pl.when(jnp.logical_and(k > 0, k == last))
    def _():
        def emit(n0, part):
            cols = slice(n0, n0 + tn)
            o_ref[:, cols] = x_ref[:, cols] + 0.5 * gate_ref[0][:, cols] * (o_ref[:, cols] + part)
        chunk(h_ref[...], emit)


def _ffn(x, mod3, row_of_tile, slot, g, w_up, w_down, tm):
    t, d = x.shape
    f = w_down.shape[0]
    tf = _pick_tile(f, 512, 128)
    nf = f // tf
    assert nf >= 2, "the kernel keeps separate first / last chunk paths"
    mod_spec = lambda c: pl.BlockSpec((1, 1, d), lambda i, k: (row_of_tile(i), 0, 3 * slot + c))
    row_tile = pl.BlockSpec((tm, d), lambda i, k: (i, 0))
    return pl.pallas_call(
        functools.partial(_ffn_kernel, tn=_pick_tile(d, 512, 128)),
        out_shape=jax.ShapeDtypeStruct((t, d), F32),
        grid=(t // tm, nf),
        in_specs=[row_tile, mod_spec(0), mod_spec(1), mod_spec(2),
                  pl.BlockSpec((1, d), lambda i, k: (0, 0)),
                  pl.BlockSpec((d, tf), lambda i, k: (0, k)),
                  pl.BlockSpec((d, tf), lambda i, k: (0, k + nf)),
                  pl.BlockSpec((tf, d), lambda i, k: (k, 0))],
        out_specs=row_tile,
        scratch_shapes=[pltpu.VMEM((tm, d), BF16)],
        compiler_params=_params("parallel", "arbitrary"),
        name="ffn",
    )(x, mod3, mod3, mod3, g.reshape(1, d), w_up, w_up, w_down)


def _rope(x, cos, sin):
    lane = lax.broadcasted_iota(jnp.int32, x.shape, 1)
    partner = jnp.where(lane % 2 == 0, pltpu.roll(x, HEAD_DIM - 1, axis=1), pltpu.roll(x, 1, axis=1))
    return x * cos + partner * sin


HALO = 16


def _in_proj_kernel(x_ref, xp_ref, xn_ref, shift_ref, scale_ref, g_ref, w_ref, qn_ref, kn_ref,
                    cos_ref, sin_ref, cw_ref, cb_ref, q_ref, k_ref, v_ref, x0c_ref, vg_ref, *,
                    q_scale, tiles_per_seq, cw):
    i = pl.program_id(0)
    tm = x_ref.shape[0]
    attn_w = q_ref.shape[1]
    kvw = k_ref.shape[1]
    c = x0c_ref.shape[1]
    first = (i % tiles_per_seq) == 0
    last = (i % tiles_per_seq) == tiles_per_seq - 1

    x_ext = jnp.concatenate([xp_ref[...], x_ref[...], xn_ref[...]], axis=0)
    y = _rms(x_ext) * g_ref[...]
    h_ext = (y * (1.0 + scale_ref[0]) + shift_ref[0]).astype(BF16)
    h = h_ext[HALO:HALO + tm]

    qkv_w = attn_w + 2 * kvw
    u = jnp.dot(h, w_ref[:, :qkv_w], preferred_element_type=F32)

    def normed(col, gain):
        return _rope(_rms(u[:, col:col + HEAD_DIM]) * gain, cos_ref[...], sin_ref[...])

    for hd in range(attn_w // HEAD_DIM):
        col = hd * HEAD_DIM
        q_ref[:, col:col + HEAD_DIM] = (normed(col, qn_ref[...]) * q_scale).astype(BF16)
    for hd in range(kvw // HEAD_DIM):
        col = hd * HEAD_DIM
        k_ref[:, col:col + HEAD_DIM] = normed(attn_w + col, kn_ref[...]).astype(BF16)
    v_ref[...] = u[:, attn_w + kvw:].astype(BF16)

    row = lax.broadcasted_iota(jnp.int32, (tm, cw), 0)
    at_start = jnp.logical_and(first, row == 0)
    at_end = jnp.logical_and(last, row == tm - 1)
    ext = tm + 2 * HALO

    def conv(group, c0):
        col = qkv_w + group * c + c0
        ue = jnp.dot(h_ext, w_ref[:, col:col + cw], preferred_element_type=F32)
        um = jnp.where(at_start, 0.0, pltpu.roll(ue, 1, axis=0)[HALO:HALO + tm])
        up = jnp.where(at_end, 0.0, pltpu.roll(ue, ext - 1, axis=0)[HALO:HALO + tm])
        w = cw_ref[group][:, c0:c0 + cw]
        return um * w[0:1] + ue[HALO:HALO + tm] * w[1:2] + up * w[2:3] + cb_ref[group][:, c0:c0 + cw]

    for c0 in range(0, c, cw):
        x0c_ref[:, c0:c0 + cw] = conv(0, c0)
        vg_ref[:, c0:c0 + cw] = conv(2, c0) * conv(1, c0)


def _in_proj(x, mod3, row_of_tile, g, w_in, qn, kn, cos, sin, conv_w, conv_b, seq, tm):
    t, d = x.shape
    kvw = N_KV_HEADS * HEAD_DIM
    attn_w = d // 2
    c = (w_in.shape[1] - attn_w - 2 * kvw) // 3
    cw = _pick_tile(c, 512, 128)
    pos_tiles = seq // tm
    hb = tm // HALO
    nhb = t // HALO
    w3 = conv_w.reshape(3, 3, c).transpose(1, 0, 2)
    b3 = conv_b.reshape(3, 1, c)
    mod_spec = lambda ch: pl.BlockSpec((1, 1, d), lambda i: (row_of_tile(i), 0, 3 + ch))
    const = lambda shape: pl.BlockSpec(shape, lambda i: (0,) * len(shape), pipeline_mode=pl.Buffered(1))
    kern = functools.partial(_in_proj_kernel, q_scale=HEAD_DIM ** -0.5 * math.log2(math.e),
                             tiles_per_seq=pos_tiles, cw=cw)
    return pl.pallas_call(
        kern,
        out_shape=(jax.ShapeDtypeStruct((t, attn_w), BF16),
                   jax.ShapeDtypeStruct((t, kvw), BF16),
                   jax.ShapeDtypeStruct((t, kvw), BF16),
                   jax.ShapeDtypeStruct((t, c), F32),
                   jax.ShapeDtypeStruct((t, c), F32)),
        grid=(t // tm,),
        in_specs=[pl.BlockSpec((tm, d), lambda i: (i, 0)),
                  pl.BlockSpec((HALO, d), lambda i: (jnp.maximum(i * hb - 1, 0), 0)),
                  pl.BlockSpec((HALO, d), lambda i: (jnp.minimum((i + 1) * hb, nhb - 1), 0)),
                  mod_spec(0), mod_spec(1),
                  const((1, d)), const(w_in.shape),
                  const((1, HEAD_DIM)), const((1, HEAD_DIM)),
                  pl.BlockSpec((tm, HEAD_DIM), lambda i: (i % pos_tiles, 0)),
                  pl.BlockSpec((tm, HEAD_DIM), lambda i: (i % pos_tiles, 0)),
                  const((3, 3, c)), const((3, 1, c))],
        out_specs=(pl.BlockSpec((tm, attn_w), lambda i: (i, 0)),
                   pl.BlockSpec((tm, kvw), lambda i: (i, 0)),
                   pl.BlockSpec((tm, kvw), lambda i: (i, 0)),
                   pl.BlockSpec((tm, c), lambda i: (i, 0)),
                   pl.BlockSpec((tm, c), lambda i: (i, 0))),
        compiler_params=_params("parallel"),
        name="in_proj",
    )(x, x, x, mod3, mod3, g.reshape(1, d), w_in, qn, kn, cos, sin, w3, b3)


def _ctx_kv_kernel(x_ref, shift_ref, scale_ref, g_ref, w_ref, kn_ref, k_ref, v_ref):
    y = _rms(x_ref[...]) * g_ref[...]
    h = (y * (1.0 + scale_ref[0]) + shift_ref[0]).astype(BF16)
    u = jnp.dot(h, w_ref[...], preferred_element_type=F32)
    for hd in range(N_KV_HEADS):
        k_ref[:, hd * HEAD_DIM:(hd + 1) * HEAD_DIM] = (
            _rms(u[:, hd * HEAD_DIM:(hd + 1) * HEAD_DIM]) * kn_ref[...]).astype(BF16)
    v_ref[...] = u[:, N_KV_HEADS * HEAD_DIM:].astype(BF16)


def _ctx_kv(x, mod3, ctx_row, g, w_kv, kn, tm):
    t, d = x.shape
    kvw = N_KV_HEADS * HEAD_DIM
    mod_spec = lambda c: pl.BlockSpec((1, 1, d), lambda i: (ctx_row, 0, 3 + c))
    return pl.pallas_call(
        _ctx_kv_kernel,
        out_shape=(jax.ShapeDtypeStruct((t, kvw), BF16), jax.ShapeDtypeStruct((t, kvw), BF16)),
        grid=(t // tm,),
        in_specs=[pl.BlockSpec((tm, d), lambda i: (i, 0)),
                  mod_spec(0), mod_spec(1),
                  pl.BlockSpec((1, d), lambda i: (0, 0)),
                  pl.BlockSpec((d, 2 * kvw), lambda i: (0, 0)),
                  pl.BlockSpec((1, HEAD_DIM), lambda i: (0, 0))],
        out_specs=(pl.BlockSpec((tm, kvw), lambda i: (i, 0)),
                   pl.BlockSpec((tm, kvw), lambda i: (i, 0))),
        compiler_params=_params("parallel"),
        name="ctx_kv",
    )(x, mod3, mod3, g.reshape(1, d), w_kv, kn)


def _attn_kernel(q_ref, k_ref, v_ref, *refs, group, nq, n_side):
    side_in, o_ref, side_out = refs[:n_side], refs[n_side], refs[n_side + 1:2 * n_side + 1]
    s0_ref, s1_ref, mx0_ref, mx1_ref, ls_ref, acc_ref = refs[2 * n_side + 1:]
    for src, dst in zip(side_in, side_out):
        dst[...] = src[...].astype(BF16)
    i = pl.program_id(2)
    tq = q_ref.shape[0]
    nk, _, tk = s0_ref.shape
    lanes = HEAD_DIM
    bufs = ((s0_ref, mx0_ref), (s1_ref, mx1_ref))

    def scores(j, q, s_ref, mx_ref):
        start = pl.multiple_of(j * tk, tk)
        s = lax.dot_general(q, k_ref[0, pl.ds(start, tk), :], (((1,), (1,)), ((), ())),
                            preferred_element_type=F32)
        s_ref[j] = s
        mx = mx_ref[...]
        for c in range(tk // lanes):
            mx = jnp.maximum(mx, s[:, c * lanes:(c + 1) * lanes])
        mx_ref[...] = mx

    def values(j, s_ref, mx_ref):
        start = pl.multiple_of(j * tk, tk)
        s = s_ref[j]
        m = mx_ref[...]
        ls = ls_ref[...]
        parts = []
        for c in range(tk // lanes):
            pc = jnp.exp2(s[:, c * lanes:(c + 1) * lanes] - m)
            ls = ls + pc
            parts.append(pc.astype(BF16))
        ls_ref[...] = ls
        p = jnp.concatenate(parts, axis=1)
        acc_ref[...] += jnp.dot(p, v_ref[0, pl.ds(start, tk), :], preferred_element_type=F32)

    def begin_scores(mx_ref):
        mx_ref[...] = jnp.full_like(mx_ref, -jnp.inf)
        return jnp.concatenate([q_ref[:, h * HEAD_DIM:(h + 1) * HEAD_DIM] for h in range(group)], axis=0)

    def end_scores(mx_ref):
        mx_ref[...] = jnp.broadcast_to(jnp.max(mx_ref[...], axis=-1, keepdims=True), mx_ref.shape)

    def begin_values():
        ls_ref[...] = jnp.zeros_like(ls_ref)
        acc_ref[...] = jnp.zeros_like(acc_ref)

    def end_values():
        o = acc_ref[...] / jnp.sum(ls_ref[...], axis=-1, keepdims=True)
        for h in range(group):
            o_ref[:, h * HEAD_DIM:(h + 1) * HEAD_DIM] = o[h * tq:(h + 1) * tq].astype(BF16)

    @pl.when(i == 0)
    def _():
        s_ref, mx_ref = bufs[0]
        q = begin_scores(mx_ref)
        lax.fori_loop(0, nk, lambda j, c: (scores(j, q, s_ref, mx_ref), c)[1], 0)
        end_scores(mx_ref)

    for parity in range(2):
        @pl.when(jnp.logical_and(jnp.logical_and(i > 0, i < nq), i % 2 == parity))
        def _():
            (cs_ref, cmx_ref), (ps_ref, pmx_ref) = bufs[parity], bufs[1 - parity]
            q = begin_scores(cmx_ref)
            begin_values()

            def both(j, c):
                scores(j, q, cs_ref, cmx_ref)
                values(j, ps_ref, pmx_ref)
                return c

            lax.fori_loop(0, nk, both, 0, unroll=True)
            end_scores(cmx_ref)
            end_values()

    @pl.when(i == nq)
    def _():
        s_ref, mx_ref = bufs[(nq - 1) % 2]
        begin_values()
        lax.fori_loop(0, nk, lambda j, c: (values(j, s_ref, mx_ref), c)[1], 0)
        end_values()


def _attention(q, k_all, v_all, batch, seq, to_bf16=()):
    t, attn_w = q.shape
    lk = k_all.shape[1]
    group = attn_w // HEAD_DIM // N_KV_HEADS
    gw = group * HEAD_DIM
    tq = _pick_tile(seq, 128, 8)
    tk = _pick_tile(lk, 1408, 128)
    nq = seq // tq
    rows = group * tq
    steps = batch * N_KV_HEADS * (nq + 1)
    side_specs = []
    for w in to_bf16:
        rb = PACKED_ROWS * -(-w.shape[0] // (PACKED_ROWS * steps))
        assert w.shape[0] % rb == 0, (w.shape, rb)
        nblk = w.shape[0] // rb
        side_specs.append(pl.BlockSpec(
            (rb, w.shape[1]),
            lambda b, g, i, nblk=nblk: (jnp.minimum((b * N_KV_HEADS + g) * (nq + 1) + i, nblk - 1), 0)))
    kern = functools.partial(_attn_kernel, group=group, nq=nq, n_side=len(to_bf16))
    outs = pl.pallas_call(
        kern,
        out_shape=(jax.ShapeDtypeStruct((t, attn_w), BF16),
                   *[jax.ShapeDtypeStruct(w.shape, BF16) for w in to_bf16]),
        grid=(batch, N_KV_HEADS, nq + 1),
        in_specs=[pl.BlockSpec((tq, gw), lambda b, g, i: (b * nq + jnp.minimum(i, nq - 1), g)),
                  pl.BlockSpec((1, lk, HEAD_DIM), lambda b, g, i: (b, 0, g)),
                  pl.BlockSpec((1, lk, HEAD_DIM), lambda b, g, i: (b, 0, g)),
                  *side_specs],
        out_specs=(pl.BlockSpec((tq, gw), lambda b, g, i: (b * nq + jnp.maximum(i - 1, 0), g)),
                   *side_specs),
        scratch_shapes=[pltpu.VMEM((lk // tk, rows, tk), F32), pltpu.VMEM((lk // tk, rows, tk), F32),
                        pltpu.VMEM((rows, HEAD_DIM), F32), pltpu.VMEM((rows, HEAD_DIM), F32),
                        pltpu.VMEM((rows, HEAD_DIM), F32), pltpu.VMEM((rows, HEAD_DIM), F32)],
        compiler_params=_params("arbitrary", "arbitrary", "arbitrary"),
        name="attention",
    )(q, k_all, v_all, *to_bf16)
    return outs[0], outs[1:]


def _dft_tables(n1, n2, k1p):
    n = n1 * n2
    k1 = np.arange(k1p)[:, None]
    a = np.arange(n1)[None, :]
    m2 = np.arange(n2)[:, None, None]
    theta = 2.0 * np.pi * ((((n2 * a * k1)[None] + m2 * k1[None]) % n) / n)
    k2 = np.arange(n2)[:, None]
    b = np.arange(n2)[None, :]
    phi = 2.0 * np.pi * ((k2 * b) % n2) / n2
    c, s = np.cos(phi), np.sin(phi)
    fmat = np.block([[c, s], [-s, c]])
    return np.cos(theta), np.sin(theta), fmat


def _mix_forward(cos, sin, n1h):
    n2, k1p, _ = cos.shape
    g = n2 // SUBLANES
    a = np.stack([cos[:, :, :n1h], -sin[:, :, :n1h]], axis=1).reshape(g, SUBLANES, 2, k1p, n1h)
    m = np.zeros((g, 2, k1p, SUBLANES, n1h, SUBLANES))
    for j in range(SUBLANES):
        m[:, :, :, j, :, j] = a[:, j]
    return m.reshape(g, 2 * k1p * SUBLANES, n1h * SUBLANES)


def _mix_inverse(cos, sin, n1h, n1):
    n2, k1p, _ = cos.shape
    g = n2 // SUBLANES
    k1 = np.arange(k1p)
    weight = np.where((k1 == 0) | (k1 == n1 // 2), 1.0, np.where(k1 < n1 // 2, 2.0, 0.0)) / (n1 * n2)
    a = np.stack([cos[:, :, :n1h], -sin[:, :, :n1h]], axis=1) * weight[None, None, :, None]
    a = a.reshape(g, SUBLANES, 2, k1p, n1h)
    m = np.zeros((g, n1h, SUBLANES, 2, k1p, SUBLANES))
    for j in range(SUBLANES):
        m[:, :, j, :, :, j] = a[:, j].transpose(0, 3, 1, 2)
    return m.reshape(g, n1h * SUBLANES, 2 * k1p * SUBLANES)


PACKED_ROWS = 2 * SUBLANES


def _stage_a_kernel(m_ref, x_ref, yr_ref, yi_ref):
    _, n1h, _, c = x_ref.shape
    k1p = yr_ref.shape[1]
    re, im = [], []
    for half in range(2):
        rows = slice(half * SUBLANES, (half + 1) * SUBLANES)
        x = x_ref[0, :, rows, :].reshape(n1h * SUBLANES, c).astype(BF16)
        y = jnp.dot(m_ref[half], x, preferred_element_type=F32)
        re.append(y[:k1p * SUBLANES].reshape(k1p, SUBLANES, c))
        im.append(y[k1p * SUBLANES:].reshape(k1p, SUBLANES, c))
    yr_ref[0] = jnp.concatenate(re, axis=1).astype(BF16)
    yi_ref[0] = jnp.concatenate(im, axis=1).astype(BF16)


def _stage_a(mix, vg4, k1p):
    b, n1h, n2, c = vg4.shape
    return pl.pallas_call(
        _stage_a_kernel,
        out_shape=(jax.ShapeDtypeStruct((b, k1p, n2, c), BF16),) * 2,
        grid=(n2 // PACKED_ROWS, b),
        in_specs=[pl.BlockSpec((2,) + mix.shape[1:], lambda g, bi: (g, 0, 0)),
                  pl.BlockSpec((1, n1h, PACKED_ROWS, c), lambda g, bi: (bi, 0, g, 0))],
        out_specs=(pl.BlockSpec((1, k1p, PACKED_ROWS, c), lambda g, bi: (bi, 0, g, 0)),) * 2,
        compiler_params=_params("parallel", "parallel"),
        name="hyena_dft_a",
    )(mix, vg4)


def _stage_b_kernel(yr_ref, yi_ref, f_ref, ft_ref, kr_ref, ki_ref, ur_ref, ui_ref):
    kb, n2 = yr_ref.shape[1:3]
    for r in range(kb):
        y = jnp.concatenate([yr_ref[0, r], yi_ref[0, r]], axis=0)
        z = jnp.dot(f_ref[...], y, preferred_element_type=F32)
        zr, zi = z[:n2], z[n2:]
        kr = kr_ref[r].astype(F32)
        ki = ki_ref[r].astype(F32)
        p = jnp.concatenate([zr * kr - zi * ki, zr * ki + zi * kr], axis=0).astype(BF16)
        u = jnp.dot(ft_ref[...], p, preferred_element_type=F32)
        ur_ref[0, r] = u[:n2].astype(BF16)
        ui_ref[0, r] = u[n2:].astype(BF16)


def _stage_b(yr, yi, fmat, fmat_t, kfr, kfi, kb):
    b, k1p, n2, c = yr.shape
    blk = pl.BlockSpec((1, kb, n2, c), lambda k, bi: (bi, k, 0, 0))
    flt = pl.BlockSpec((kb, n2, c), lambda k, bi: (k, 0, 0))
    mat = pl.BlockSpec((2 * n2, 2 * n2), lambda k, bi: (0, 0))
    return pl.pallas_call(
        _stage_b_kernel,
        out_shape=(jax.ShapeDtypeStruct((b, k1p, n2, c), BF16),) * 2,
        grid=(k1p // kb, b),
        in_specs=[blk, blk, mat, mat, flt, flt],
        out_specs=(blk, blk),
        compiler_params=_params("parallel", "parallel"),
        name="hyena_dft_b",
    )(yr, yi, fmat, fmat_t, kfr, kfi)


def _stage_c_kernel(m_ref, ur_ref, ui_ref, vg_ref, x0_ref, hb_ref, g_ref, o_ref):
    _, k1p, _, c = ur_ref.shape
    n1h = vg_ref.shape[1]
    ur = ur_ref[0].astype(F32)
    ui = ui_ref[0].astype(F32)
    for half in range(2):
        rows = slice(half * SUBLANES, (half + 1) * SUBLANES)
        u = jnp.concatenate([ur[:, rows, :].reshape(k1p * SUBLANES, c),
                             ui[:, rows, :].reshape(k1p * SUBLANES, c)], axis=0).astype(BF16)
        conv = jnp.dot(m_ref[half], u, preferred_element_type=F32)
        vg = vg_ref[0, :, rows, :].reshape(n1h * SUBLANES, c)
        x0 = x0_ref[0, :, rows, :].reshape(n1h * SUBLANES, c)
        hyo = (conv + hb_ref[...] * vg) * x0
        o_ref[0, :, rows, :] = (_rms(hyo) * g_ref[...]).reshape(n1h, SUBLANES, c)


def _stage_c(mix, ur, ui, vg4, x04, hy_bias, g_hy):
    b, n1h, n2, c = vg4.shape
    k1p = ur.shape[1]
    ublk = pl.BlockSpec((1, k1p, PACKED_ROWS, c), lambda g, bi: (bi, 0, g, 0))
    xblk = pl.BlockSpec((1, n1h, PACKED_ROWS, c), lambda g, bi: (bi, 0, g, 0))
    vec = pl.BlockSpec((1, c), lambda g, bi: (0, 0))
    return pl.pallas_call(
        _stage_c_kernel,
        out_shape=jax.ShapeDtypeStruct((b, n1h, n2, c), F32),
        grid=(n2 // PACKED_ROWS, b),
        in_specs=[pl.BlockSpec((2,) + mix.shape[1:], lambda g, bi: (g, 0, 0)),
                  ublk, ublk, xblk, xblk, vec, vec],
        out_specs=xblk,
        compiler_params=_params("parallel", "parallel"),
        name="hyena_dft_c",
    )(mix, ur, ui, vg4, x04, hy_bias.reshape(1, c), g_hy.reshape(1, c))


def _filter_a_kernel(a_ref, w1t_ref, w1c_ref, w1s_ref, b1_ref, w2_ref, b2_ref, w3_ref, b3_ref,
                     w4_ref, fr_ref, dec_ref, yr_ref, yi_ref, norm_ref, *, seq):
    g = pl.program_id(0)
    k1p = yr_ref.shape[0]
    c = norm_ref.shape[1]
    n1 = a_ref.shape[2]
    n1h = n1 // 2
    n2 = DFT_N2

    def lags(idx, j):
        tprime = idx * n2 + (g * SUBLANES + j)
        fwd = idx < n1h
        return tprime, fwd, jnp.where(fwd, tprime, 2 * seq - tprime).astype(F32)

    def stacked(idx, axis):
        parts = [lags(idx, j) for j in range(SUBLANES)]
        return [jnp.concatenate([p[k] for p in parts], axis=axis) for k in range(3)]

    _, _, pos_l = stacked(lax.broadcasted_iota(jnp.int32, (1, n1), 1), 1)
    tprime_r, fwd_r, pos_r = stacked(lax.broadcasted_iota(jnp.int32, (n1, 1), 0), 0)

    band = lax.broadcasted_iota(jnp.int32, (FILTER_BANDS, 1), 0).astype(F32)
    f = 1e-4 + band * ((FILTER_BANDS - 1 - 1e-4) / (FILTER_BANDS - 1))
    ang = f * ((2.0 * math.pi * pos_l) / seq)
    fr = fr_ref[...]
    dot = functools.partial(jnp.dot, preferred_element_type=F32, precision=_HI)
    h = w1t_ref[...] * (pos_l / (seq - 1.0)) + dot(w1c_ref[...], jnp.cos(ang)) + dot(w1s_ref[...], -jnp.sin(ang))
    h = jnp.sin(fr * (h + b1_ref[...]))
    h = jnp.sin(fr * (dot(w2_ref[...], h) + b2_ref[...]))
    h = jnp.sin(fr * (dot(w3_ref[...], h) + b3_ref[...]))
    h = lax.dot_general(h, w4_ref[...], (((0,), (0,)), ((), ())), preferred_element_type=F32)
    t_r = pos_r / (seq - 1.0)
    dec = jnp.abs(dec_ref[...])
    kf = h[:, :c] * jnp.exp(-t_r * dec[0:1, :])
    kb = h[:, c:] * jnp.exp(-t_r * dec[1:2, :])
    kk = jnp.where(fwd_r, kf, jnp.where(tprime_r == seq, 0.0, kb))

    @pl.when(g == 0)
    def _():
        norm_ref[...] = jnp.zeros_like(norm_ref)

    norm_ref[...] += jnp.sum(jnp.abs(kk), axis=0, keepdims=True)
    for j in range(SUBLANES):
        y = jnp.dot(a_ref[j], kk[j * n1:(j + 1) * n1].astype(BF16), preferred_element_type=F32)
        yr_ref[:, j * c:(j + 1) * c] = y[:k1p].astype(BF16)
        yi_ref[:, j * c:(j + 1) * c] = y[k1p:].astype(BF16)


def _filter_a(a_full, flt, seq, c):
    w1, b1, w2, b2, w3, b3, w4, freq, decay = flt
    hid = w2.shape[0]
    n2, k1p2, n1 = a_full.shape
    k1p = k1p2 // 2
    full = lambda shape: pl.BlockSpec(shape, lambda g: (0,) * len(shape))
    kern = functools.partial(_filter_a_kernel, seq=seq)
    return pl.pallas_call(
        kern,
        out_shape=(jax.ShapeDtypeStruct((k1p, n2 * c), BF16),
                   jax.ShapeDtypeStruct((k1p, n2 * c), BF16),
                   jax.ShapeDtypeStruct((1, c), F32)),
        grid=(n2 // SUBLANES,),
        in_specs=[pl.BlockSpec((SUBLANES, k1p2, n1), lambda g: (g, 0, 0)),
                  full((hid, 1)), full((hid, FILTER_BANDS)), full((hid, FILTER_BANDS)), full((hid, 1)),
                  full((hid, hid)), full((hid, 1)), full((hid, hid)), full((hid, 1)),
                  full((hid, 2 * c)), full((hid, 1)), full((2, c))],
        out_specs=(pl.BlockSpec((k1p, SUBLANES * c), lambda g: (0, g)),
                   pl.BlockSpec((k1p, SUBLANES * c), lambda g: (0, g)),
                   pl.BlockSpec((1, c), lambda g: (0, 0))),
        compiler_params=_params("arbitrary"),
        name="hyena_filter_a",
    )(a_full, w1[0:1].T, w1[1:1 + FILTER_BANDS].T, w1[1 + FILTER_BANDS:].T, b1.reshape(hid, 1),
      w2.T, b2.reshape(hid, 1), w3.T, b3.reshape(hid, 1), w4, freq.reshape(hid, 1), decay)


def _filter_b_kernel(yr_ref, yi_ref, f_ref, norm_ref, kr_ref, ki_ref):
    kb, n2 = yr_ref.shape[:2]
    for r in range(kb):
        y = jnp.concatenate([yr_ref[r], yi_ref[r]], axis=0)
        z = jnp.dot(f_ref[...], y, preferred_element_type=F32) / norm_ref[...]
        kr_ref[r] = z[:n2].astype(BF16)
        ki_ref[r] = z[n2:].astype(BF16)


def _filter_b(yr, yi, fmat, norm, kb):
    k1p, n2, c = yr.shape
    blk = pl.BlockSpec((kb, n2, c), lambda k: (k, 0, 0))
    return pl.pallas_call(
        _filter_b_kernel,
        out_shape=(jax.ShapeDtypeStruct((k1p, n2, c), BF16),) * 2,
        grid=(k1p // kb,),
        in_specs=[blk, blk,
                  pl.BlockSpec((2 * n2, 2 * n2), lambda k: (0, 0)),
                  pl.BlockSpec((1, c), lambda k: (0, 0))],
        out_specs=(blk, blk),
        compiler_params=_params("parallel"),
        name="hyena_filter_b",
    )(yr, yi, fmat, norm)


def _hyena(x0c, vg, flt, hy_bias, g_hy, batch, seq):
    c = vg.shape[1]
    n2 = DFT_N2
    n1h = seq // n2
    n1 = 2 * n1h
    k1p = min(n1, -(-(n1 // 2 + 1) // SUBLANES) * SUBLANES)
    kb = _pick_tile(k1p, 4, 1)
    cos, sin, fmat_np = _dft_tables(n1, n2, k1p)
    a_full = jnp.asarray(np.concatenate([cos, -sin], axis=1), BF16)
    mix_a = jnp.asarray(_mix_forward(cos, sin, n1h), BF16)
    mix_c = jnp.asarray(_mix_inverse(cos, sin, n1h, n1), BF16)
    fmat = jnp.asarray(fmat_np, BF16)
    fmat_t = jnp.asarray(fmat_np.T, BF16)

    fyr, fyi, norm = _filter_a(a_full, flt, seq, c)
    kfr, kfi = _filter_b(fyr.reshape(k1p, n2, c), fyi.reshape(k1p, n2, c), fmat, norm, kb)

    vg4 = vg.reshape(batch, n1h, n2, c)
    x04 = x0c.reshape(batch, n1h, n2, c)
    yr, yi = _stage_a(mix_a, vg4, k1p)
    ur, ui = _stage_b(yr, yi, fmat, fmat_t, kfr, kfi, kb)
    y = _stage_c(mix_c, ur, ui, vg4, x04, hy_bias, g_hy)
    return y.reshape(batch * seq, c)


def _out_proj_kernel(x_ref, gate_ref, attn_ref, hy_ref, ga_ref, wa_ref, wh_ref, o_ref):
    ya = (_rms(attn_ref[...].astype(F32)) * ga_ref[...]).astype(BF16)
    y = jnp.dot(ya, wa_ref[...], preferred_element_type=F32)
    y += jnp.dot(hy_ref[...].astype(BF16), wh_ref[...], preferred_element_type=F32)
    o_ref[...] = x_ref[...] + gate_ref[0] * y


def _out_proj(x, mod3, row_of_tile, attn, hyn, g_attn, w_out, tm):
    t, d = x.shape
    aw = attn.shape[1]
    hw = hyn.shape[1]
    const = lambda shape, idx: pl.BlockSpec(shape, lambda i: idx, pipeline_mode=pl.Buffered(1))
    return pl.pallas_call(
        _out_proj_kernel,
        out_shape=jax.ShapeDtypeStruct((t, d), F32),
        grid=(t // tm,),
        in_specs=[pl.BlockSpec((tm, d), lambda i: (i, 0)),
                  pl.BlockSpec((1, 1, d), lambda i: (row_of_tile(i), 0, 5)),
                  pl.BlockSpec((tm, aw), lambda i: (i, 0)),
                  pl.BlockSpec((tm, hw), lambda i: (i, 0)),
                  const((1, aw), (0, 0)), const((aw, d), (0, 0)), const((hw, d), (1, 0))],
        out_specs=pl.BlockSpec((tm, d), lambda i: (i, 0)),
        compiler_params=_params("parallel"),
        name="out_proj",
    )(x, mod3, attn, hyn, g_attn.reshape(1, aw), w_out, w_out)


def _rope_tables(seq):
    half = HEAD_DIM // 2
    pos = np.arange(seq)
    inv = ROPE_THETA ** (-np.arange(0, half, 2, dtype=np.float64) / half)
    ang = np.concatenate([(pos // GRID_W)[:, None] * inv, (pos % GRID_W)[:, None] * inv], axis=-1)
    cos = np.repeat(np.cos(ang), 2, axis=-1)
    sin = np.stack([-np.sin(ang), np.sin(ang)], axis=-1).reshape(seq, HEAD_DIM)
    return jnp.asarray(cos, F32), jnp.asarray(sin, F32)


def kernel(x, c, ctx, c_ctx, w_ada, b_ada, g_norm, w_ffn1_up, w_ffn1_down, w_ffn2_up, w_ffn2_down,
           w_in, q_norm, k_norm, conv_w, conv_b, flt_w1, flt_b1, flt_w2, flt_b2, flt_w3, flt_b3,
           flt_w4, flt_freq, flt_decay, hy_bias, g_out, w_out):
    batch, seq, d = x.shape
    lc = ctx.shape[1]
    depth = w_ada.shape[0]
    assert depth == 1, "context-update path of deeper stacks is not implemented"
    attn_w = d // 2
    kvw = N_KV_HEADS * HEAD_DIM
    tm = _pick_tile(seq, 512, 128)
    tmf = _pick_tile(seq, FFN_ROWS, 128)
    tmc = _pick_tile(batch * lc, 512, 8)
    lat_row = lambda i: i // (seq // tm)
    lat_row_ffn = lambda i: i // (seq // tmf)
    ctx_row = lambda i: batch

    xs = x.reshape(batch * seq, d)
    cs = ctx.reshape(batch * lc, d)
    rows = -(-(batch + 1) // 8) * 8
    c_cols = jnp.zeros((d, rows), F32).at[:, :batch].set(c.T).at[:, batch].set(c_ctx)
    cos, sin = _rope_tables(seq)

    l = 0
    mod3 = _ada_mod(c_cols, batch + 1, w_ada[l], b_ada[l]).reshape(rows, 1, N_MOD * d)

    w1u, w1d = w_ffn1_up[l].astype(BF16), w_ffn1_down[l].astype(BF16)
    wi = w_in[l].astype(BF16)
    qkv_w = attn_w + 2 * kvw
    qn = q_norm[l].reshape(1, HEAD_DIM)
    kn = k_norm[l].reshape(1, HEAD_DIM)
    wo = w_out[l].astype(BF16)

    x1 = _ffn(xs, mod3, lat_row_ffn, 0, g_norm[l, 0], w1u, w1d, tmf)
    c1 = _ffn(cs, mod3, ctx_row, 0, g_norm[l, 0], w1u, w1d, tmc)

    q, k, v, x0c, vg = _in_proj(x1, mod3, lat_row, g_norm[l, 1], wi, qn, kn,
                                cos, sin, conv_w[l], conv_b[l], seq, tm)
    kc, vc = _ctx_kv(c1, mod3, batch, g_norm[l, 1], wi[:, attn_w:qkv_w], kn, tmc)
    k_all = jnp.concatenate([kc.reshape(batch, lc, kvw), k.reshape(batch, seq, kvw)], axis=1)
    v_all = jnp.concatenate([vc.reshape(batch, lc, kvw), v.reshape(batch, seq, kvw)], axis=1)
    attn, (w2u, w2d) = _attention(q, k_all, v_all, batch, seq, to_bf16=(w_ffn2_up[l], w_ffn2_down[l]))

    flt = (flt_w1[l], flt_b1[l], flt_w2[l], flt_b2[l], flt_w3[l], flt_b3[l], flt_w4[l],
           flt_freq[l], flt_decay[l])
    hyn = _hyena(x0c, vg, flt, hy_bias[l], g_out[l, attn_w:], batch, seq)
    x2 = _out_proj(x1, mod3, lat_row, attn, hyn, g_out[l, :attn_w], wo, tm)

    x3 = _ffn(x2, mod3, lat_row_ffn, 2, g_norm[l, 2], w2u, w2d, tmf)
    return x3.reshape(batch, seq, d)
```

```python
import functools
import math

import jax
import jax.numpy as jnp
import numpy as np
from jax import lax
from jax.experimental import pallas as pl
from jax.experimental.pallas import tpu as pltpu

F32 = jnp.float32
BF16 = jnp.bfloat16

HEAD_DIM = 128
N_KV_HEADS = 2
GRID_W = 64
ROPE_THETA = 10000.0
FILTER_BANDS = 16
RMS_EPS = 1e-6
N_MOD = 9

V7X_VMEM_LIMIT_BYTES = 56 * 1024 * 1024
SUBLANES = 8
PACKED_ROWS = 2 * SUBLANES
FFN_ROWS = 1024
DFT_N2 = 128

_HI = lax.Precision.HIGHEST


def _params(*sem):
    return pltpu.CompilerParams(dimension_semantics=sem, vmem_limit_bytes=V7X_VMEM_LIMIT_BYTES)


def _pick_tile(n, cap, mult):
    best = None
    t = mult
    while t <= min(n, cap):
        if n % t == 0:
            best = t
        t += mult
    assert best is not None, (n, cap, mult)
    return best


def _rms(x, eps=RMS_EPS):
    return x * lax.rsqrt(jnp.mean(x * x, axis=-1, keepdims=True) + eps)


def _side_cast_specs(mats, n_steps, step_of):
    specs = []
    for w in mats:
        rb = PACKED_ROWS * -(-w.shape[0] // (PACKED_ROWS * n_steps))
        assert w.shape[0] % rb == 0, (w.shape, rb)
        nblk = w.shape[0] // rb
        specs.append(pl.BlockSpec((rb, w.shape[1]),
                                  lambda *idx, nblk=nblk: (jnp.minimum(step_of(*idx), nblk - 1), 0)))
    return specs


def _side_cast(side_in, side_out):
    for src, dst in zip(side_in, side_out):
        dst[...] = src[...].astype(BF16)


def _ada_kernel(ct_ref, w_ref, b_ref, o_ref, sb_ref, *, n_rows):
    lanes = sb_ref.shape[2]

    @pl.when(pl.program_id(0) == 0)
    def _():
        ct = ct_ref[...]
        s = ct * jax.nn.sigmoid(ct)
        for r in range(n_rows):
            sb_ref[r] = jnp.broadcast_to(s[:, r:r + 1], sb_ref.shape[1:])

    d, tn = w_ref.shape
    reps = tn // lanes
    acc = [jnp.zeros((SUBLANES, tn), F32) for _ in range(n_rows)]
    for kb in range(d // SUBLANES):
        rows = slice(kb * SUBLANES, (kb + 1) * SUBLANES)
        w = w_ref[rows, :]
        for r in range(n_rows):
            acc[r] = acc[r] + w * jnp.concatenate([sb_ref[r, rows, :]] * reps, axis=1)
    out = [jnp.sum(a, axis=0, keepdims=True) for a in acc]
    out.append(jnp.zeros((o_ref.shape[0] - n_rows, tn), F32))
    o_ref[...] = jnp.concatenate(out, axis=0) + b_ref[...]


def _ada_mod(c_cols, n_rows, w_ada, b_ada):
    d, rows = c_cols.shape
    n = w_ada.shape[1]
    tn = _pick_tile(n, 2048, 128)
    return pl.pallas_call(
        functools.partial(_ada_kernel, n_rows=n_rows),
        out_shape=jax.ShapeDtypeStruct((rows, n), F32),
        grid=(n // tn,),
        in_specs=[pl.BlockSpec((d, rows), lambda j: (0, 0)),
                  pl.BlockSpec((d, tn), lambda j: (0, j)),
                  pl.BlockSpec((1, tn), lambda j: (0, j))],
        out_specs=pl.BlockSpec((rows, tn), lambda j: (0, j)),
        scratch_shapes=[pltpu.VMEM((n_rows, d, HEAD_DIM), F32)],
        compiler_params=_params("arbitrary"),
        name="ada_mod",
    )(c_cols, w_ada, b_ada.reshape(1, n))


def _ffn_kernel(x_ref, shift_ref, scale_ref, gate_ref, g_ref, wg_ref, wu_ref, wd_ref, *refs, tn, n_side):
    side_in, o_ref, side_out, h_ref = refs[:n_side], refs[n_side], refs[n_side + 1:-1], refs[-1]
    _side_cast(side_in, side_out)
    k = pl.program_id(1)
    last = pl.num_programs(1) - 1
    d = o_ref.shape[1]

    def chunk(h, emit):
        g = jnp.dot(h, wg_ref[...], preferred_element_type=F32)
        u = jnp.dot(h, wu_ref[...], preferred_element_type=F32)
        a = (g * jax.nn.sigmoid(g) * u).astype(BF16)
        for n0 in range(0, d, tn):
            emit(n0, jnp.dot(a, wd_ref[:, n0:n0 + tn], preferred_element_type=F32))

    @pl.when(k == 0)
    def _():
        y = _rms(x_ref[...]) * g_ref[...]
        h = (y * (1.0 + scale_ref[0]) + shift_ref[0]).astype(BF16)
        h_ref[...] = h

        def emit(n0, part):
            o_ref[:, n0:n0 + tn] = part
        chunk(h, emit)

    @pl.when(jnp.logical_and(k > 0, k < last))
    def _():
        def emit(n0, part):
            o_ref[:, n0:n0 + tn] += part
        chunk(h_ref[...], emit)

    @pl.when(jnp.logical_and(k > 0, k == last))
    def _():
        def emit(n0, part):
            cols = slice(n0, n0 + tn)
            o_ref[:, cols] = x_ref[:, cols] + 0.5 * gate_ref[0][:, cols] * (o_ref[:, cols] + part)
        chunk(h_ref[...], emit)


def _ffn(x, mod3, row_of_tile, slot, g, w_up, w_down, tm, to_bf16=()):
    t, d = x.shape
    f = w_down.shape[0]
    tf = _pick_tile(f, 512, 128)
    nf = f // tf
    assert nf >= 2, "the kernel keeps separate first / last chunk paths"
    mod_spec = lambda c: pl.BlockSpec((1, 1, d), lambda i, k: (row_of_tile(i), 0, 3 * slot + c))
    row_tile = pl.BlockSpec((tm, d), lambda i, k: (i, 0))
    side_specs = _side_cast_specs(to_bf16, (t // tm) * nf, lambda i, k: i * nf + k)
    outs = pl.pallas_call(
        functools.partial(_ffn_kernel, tn=_pick_tile(d, 512, 128), n_side=len(to_bf16)),
        out_shape=(jax.ShapeDtypeStruct((t, d), F32), *[jax.ShapeDtypeStruct(w.shape, BF16) for w in to_bf16]),
        grid=(t // tm, nf),
        in_specs=[row_tile, mod_spec(0), mod_spec(1), mod_spec(2),
                  pl.BlockSpec((1, d), lambda i, k: (0, 0)),
                  pl.BlockSpec((d, tf), lambda i, k: (0, k)),
                  pl.BlockSpec((d, tf), lambda i, k: (0, k + nf)),
                  pl.BlockSpec((tf, d), lambda i, k: (k, 0)),
                  *side_specs],
        out_specs=(row_tile, *side_specs),
        scratch_shapes=[pltpu.VMEM((tm, d), BF16)],
        compiler_params=_params("arbitrary", "arbitrary"),
        name="ffn",
    )(x, mod3, mod3, mod3, g.reshape(1, d), w_up, w_up, w_down, *to_bf16)
    return outs[0], outs[1:]


def _rope(x, cos, sin):
    lane = lax.broadcasted_iota(jnp.int32, x.shape, 1)
    partner = jnp.where(lane % 2 == 0, pltpu.roll(x, HEAD_DIM - 1, axis=1), pltpu.roll(x, 1, axis=1))
    return x * cos + partner * sin


HALO = 16


def _in_proj_kernel(x_ref, xp_ref, xn_ref, shift_ref, scale_ref, g_ref, w_ref, qn_ref, kn_ref,
                    cos_ref, sin_ref, cw_ref, cb_ref, q_ref, k_ref, v_ref, x0c_ref, vg_ref, *,
                    q_scale, tiles_per_seq, cw):
    i = pl.program_id(0)
    tm = x_ref.shape[0]
    attn_w = q_ref.shape[1]
    kvw = k_ref.shape[1]
    c = x0c_ref.shape[1]
    first = (i % tiles_per_seq) == 0
    last = (i % tiles_per_seq) == tiles_per_seq - 1

    x_ext = jnp.concatenate([xp_ref[...], x_ref[...], xn_ref[...]], axis=0)
    y = _rms(x_ext) * g_ref[...]
    h_ext = (y * (1.0 + scale_ref[0]) + shift_ref[0]).astype(BF16)
    h = h_ext[HALO:HALO + tm]

    qkv_w = attn_w + 2 * kvw
    u = jnp.dot(h, w_ref[:, :qkv_w], preferred_element_type=F32)

    def normed(col, gain):
        return _rope(_rms(u[:, col:col + HEAD_DIM]) * gain, cos_ref[...], sin_ref[...])

    for hd in range(attn_w // HEAD_DIM):
        col = hd * HEAD_DIM
        q_ref[:, col:col + HEAD_DIM] = (normed(col, qn_ref[...]) * q_scale).astype(BF16)
    for hd in range(kvw // HEAD_DIM):
        col = hd * HEAD_DIM
        k_ref[:, col:col + HEAD_DIM] = normed(attn_w + col, kn_ref[...]).astype(BF16)
    v_ref[...] = u[:, attn_w + kvw:].astype(BF16)

    row = lax.broadcasted_iota(jnp.int32, (tm, cw), 0)
    at_start = jnp.logical_and(first, row == 0)
    at_end = jnp.logical_and(last, row == tm - 1)
    ext = tm + 2 * HALO

    def conv(group, c0):
        col = qkv_w + group * c + c0
        ue = jnp.dot(h_ext, w_ref[:, col:col + cw], preferred_element_type=F32)
        um = jnp.where(at_start, 0.0, pltpu.roll(ue, 1, axis=0)[HALO:HALO + tm])
        up = jnp.where(at_end, 0.0, pltpu.roll(ue, ext - 1, axis=0)[HALO:HALO + tm])
        w = cw_ref[group][:, c0:c0 + cw]
        return um * w[0:1] + ue[HALO:HALO + tm] * w[1:2] + up * w[2:3] + cb_ref[group][:, c0:c0 + cw]

    for c0 in range(0, c, cw):
        x0c_ref[:, c0:c0 + cw] = conv(0, c0).astype(BF16)
        vg_ref[:, c0:c0 + cw] = (conv(2, c0) * conv(1, c0)).astype(BF16)


def _in_proj(x, mod3, row_of_tile, g, w_in, qn, kn, cos, sin, conv_w, conv_b, seq, tm):
    t, d = x.shape
    kvw = N_KV_HEADS * HEAD_DIM
    attn_w = d // 2
    c = (w_in.shape[1] - attn_w - 2 * kvw) // 3
    cw = _pick_tile(c, 512, 128)
    pos_tiles = seq // tm
    hb = tm // HALO
    nhb = t // HALO
    w3 = conv_w.reshape(3, 3, c).transpose(1, 0, 2)
    b3 = conv_b.reshape(3, 1, c)
    mod_spec = lambda ch: pl.BlockSpec((1, 1, d), lambda i: (row_of_tile(i), 0, 3 + ch))
    const = lambda shape: pl.BlockSpec(shape, lambda i: (0,) * len(shape), pipeline_mode=pl.Buffered(1))
    kern = functools.partial(_in_proj_kernel, q_scale=HEAD_DIM ** -0.5 * math.log2(math.e),
                             tiles_per_seq=pos_tiles, cw=cw)
    return pl.pallas_call(
        kern,
        out_shape=(jax.ShapeDtypeStruct((t, attn_w), BF16),
                   jax.ShapeDtypeStruct((t, kvw), BF16),
                   jax.ShapeDtypeStruct((t, kvw), BF16),
                   jax.ShapeDtypeStruct((t, c), BF16),
                   jax.ShapeDtypeStruct((t, c), BF16)),
        grid=(t // tm,),
        in_specs=[pl.BlockSpec((tm, d), lambda i: (i, 0)),
                  pl.BlockSpec((HALO, d), lambda i: (jnp.maximum(i * hb - 1, 0), 0)),
                  pl.BlockSpec((HALO, d), lambda i: (jnp.minimum((i + 1) * hb, nhb - 1), 0)),
                  mod_spec(0), mod_spec(1),
                  const((1, d)), const(w_in.shape),
                  const((1, HEAD_DIM)), const((1, HEAD_DIM)),
                  pl.BlockSpec((tm, HEAD_DIM), lambda i: (i % pos_tiles, 0)),
                  pl.BlockSpec((tm, HEAD_DIM), lambda i: (i % pos_tiles, 0)),
                  const((3, 3, c)), const((3, 1, c))],
        out_specs=(pl.BlockSpec((tm, attn_w), lambda i: (i, 0)),
                   pl.BlockSpec((tm, kvw), lambda i: (i, 0)),
                   pl.BlockSpec((tm, kvw), lambda i: (i, 0)),
                   pl.BlockSpec((tm, c), lambda i: (i, 0)),
                   pl.BlockSpec((tm, c), lambda i: (i, 0))),
        compiler_params=_params("parallel"),
        name="in_proj",
    )(x, x, x, mod3, mod3, g.reshape(1, d), w_in, qn, kn, cos, sin, w3, b3)


def _ctx_kv_kernel(x_ref, shift_ref, scale_ref, g_ref, w_ref, kn_ref, k_ref, v_ref):
    y = _rms(x_ref[...]) * g_ref[...]
    h = (y * (1.0 + scale_ref[0]) + shift_ref[0]).astype(BF16)
    u = jnp.dot(h, w_ref[...], preferred_element_type=F32)
    for hd in range(N_KV_HEADS):
        k_ref[:, hd * HEAD_DIM:(hd + 1) * HEAD_DIM] = (
            _rms(u[:, hd * HEAD_DIM:(hd + 1) * HEAD_DIM]) * kn_ref[...]).astype(BF16)
    v_ref[...] = u[:, N_KV_HEADS * HEAD_DIM:].astype(BF16)


def _ctx_kv(x, mod3, ctx_row, g, w_kv, kn, tm):
    t, d = x.shape
    kvw = N_KV_HEADS * HEAD_DIM
    mod_spec = lambda c: pl.BlockSpec((1, 1, d), lambda i: (ctx_row, 0, 3 + c))
    return pl.pallas_call(
        _ctx_kv_kernel,
        out_shape=(jax.ShapeDtypeStruct((t, kvw), BF16), jax.ShapeDtypeStruct((t, kvw), BF16)),
        grid=(t // tm,),
        in_specs=[pl.BlockSpec((tm, d), lambda i: (i, 0)),
                  mod_spec(0), mod_spec(1),
                  pl.BlockSpec((1, d), lambda i: (0, 0)),
                  pl.BlockSpec((d, 2 * kvw), lambda i: (0, 0)),
                  pl.BlockSpec((1, HEAD_DIM), lambda i: (0, 0))],
        out_specs=(pl.BlockSpec((tm, kvw), lambda i: (i, 0)),
                   pl.BlockSpec((tm, kvw), lambda i: (i, 0))),
        compiler_params=_params("parallel"),
        name="ctx_kv",
    )(x, mod3, mod3, g.reshape(1, d), w_kv, kn)


def _attn_kernel(q_ref, k_ref, v_ref, *refs, group, nq, n_side):
    side_in, o_ref, side_out = refs[:n_side], refs[n_side], refs[n_side + 1:2 * n_side + 1]
    s0_ref, s1_ref, mx0_ref, mx1_ref, ls_ref, acc_ref = refs[2 * n_side + 1:]
    _side_cast(side_in, side_out)
    i = pl.program_id(2)
    tq = q_ref.shape[0]
    nk, _, tk = s0_ref.shape
    lanes = HEAD_DIM
    bufs = ((s0_ref, mx0_ref), (s1_ref, mx1_ref))

    def scores(j, q, s_ref, mx_ref):
        start = pl.multiple_of(j * tk, tk)
        s = lax.dot_general(q, k_ref[0, pl.ds(start, tk), :], (((1,), (1,)), ((), ())),
                            preferred_element_type=F32)
        s_ref[j] = s
        mx = mx_ref[...]
        for c in range(tk // lanes):
            mx = jnp.maximum(mx, s[:, c * lanes:(c + 1) * lanes])
        mx_ref[...] = mx

    def values(j, s_ref, mx_ref):
        start = pl.multiple_of(j * tk, tk)
        s = s_ref[j]
        m = mx_ref[...]
        ls = ls_ref[...]
        parts = []
        for c in range(tk // lanes):
            pc = jnp.exp2(s[:, c * lanes:(c + 1) * lanes] - m)
            ls = ls + pc
            parts.append(pc.astype(BF16))
        ls_ref[...] = ls
        p = jnp.concatenate(parts, axis=1)
        acc_ref[...] += jnp.dot(p, v_ref[0, pl.ds(start, tk), :], preferred_element_type=F32)

    def begin_scores(mx_ref):
        mx_ref[...] = jnp.full_like(mx_ref, -jnp.inf)
        return jnp.concatenate([q_ref[:, h * HEAD_DIM:(h + 1) * HEAD_DIM] for h in range(group)], axis=0)

    def end_scores(mx_ref):
        mx_ref[...] = jnp.broadcast_to(jnp.max(mx_ref[...], axis=-1, keepdims=True), mx_ref.shape)

    def begin_values():
        ls_ref[...] = jnp.zeros_like(ls_ref)
        acc_ref[...] = jnp.zeros_like(acc_ref)

    def end_values():
        o = acc_ref[...] / jnp.sum(ls_ref[...], axis=-1, keepdims=True)
        for h in range(group):
            o_ref[:, h * HEAD_DIM:(h + 1) * HEAD_DIM] = o[h * tq:(h + 1) * tq].astype(BF16)

    @pl.when(i == 0)
    def _():
        s_ref, mx_ref = bufs[0]
        q = begin_scores(mx_ref)
        lax.fori_loop(0, nk, lambda j, c: (scores(j, q, s_ref, mx_ref), c)[1], 0)
        end_scores(mx_ref)

    for parity in range(2):
        @pl.when(jnp.logical_and(jnp.logical_and(i > 0, i < nq), i % 2 == parity))
        def _():
            (cs_ref, cmx_ref), (ps_ref, pmx_ref) = bufs[parity], bufs[1 - parity]
            q = begin_scores(cmx_ref)
            begin_values()

            def both(j, c):
                scores(j, q, cs_ref, cmx_ref)
                values(j, ps_ref, pmx_ref)
                return c

            lax.fori_loop(0, nk, both, 0, unroll=True)
            end_scores(cmx_ref)
            end_values()

    @pl.when(i == nq)
    def _():
        s_ref, mx_ref = bufs[(nq - 1) % 2]
        begin_values()
        lax.fori_loop(0, nk, lambda j, c: (values(j, s_ref, mx_ref), c)[1], 0)
        end_values()


def _attention(q, k_all, v_all, batch, seq, to_bf16=()):
    t, attn_w = q.shape
    lk = k_all.shape[1]
    group = attn_w // HEAD_DIM // N_KV_HEADS
    gw = group * HEAD_DIM
    tq = _pick_tile(seq, 128, 8)
    tk = _pick_tile(lk, 1408, 128)
    nq = seq // tq
    rows = group * tq
    side_specs = _side_cast_specs(to_bf16, batch * N_KV_HEADS * (nq + 1),
                                  lambda b, g, i: (b * N_KV_HEADS + g) * (nq + 1) + i)
    kern = functools.partial(_attn_kernel, group=group, nq=nq, n_side=len(to_bf16))
    outs = pl.pallas_call(
        kern,
        out_shape=(jax.ShapeDtypeStruct((t, attn_w), BF16),
                   *[jax.ShapeDtypeStruct(w.shape, BF16) for w in to_bf16]),
        grid=(batch, N_KV_HEADS, nq + 1),
        in_specs=[pl.BlockSpec((tq, gw), lambda b, g, i: (b * nq + jnp.minimum(i, nq - 1), g)),
                  pl.BlockSpec((1, lk, HEAD_DIM), lambda b, g, i: (b, 0, g)),
                  pl.BlockSpec((1, lk, HEAD_DIM), lambda b, g, i: (b, 0, g)),
                  *side_specs],
        out_specs=(pl.BlockSpec((tq, gw), lambda b, g, i: (b * nq + jnp.maximum(i - 1, 0), g)),
                   *side_specs),
        scratch_shapes=[pltpu.VMEM((lk // tk, rows, tk), F32), pltpu.VMEM((lk // tk, rows, tk), F32),
                        pltpu.VMEM((rows, HEAD_DIM), F32), pltpu.VMEM((rows, HEAD_DIM), F32),
                        pltpu.VMEM((rows, HEAD_DIM), F32), pltpu.VMEM((rows, HEAD_DIM), F32)],
        compiler_params=_params("arbitrary", "arbitrary", "arbitrary"),
        name="attention",
    )(q, k_all, v_all, *to_bf16)
    return outs[0], outs[1:]


def _dft_tables(n1, n2, k1p):
    n = n1 * n2
    k1 = np.arange(k1p)[:, None]
    a = np.arange(n1)[None, :]
    m2 = np.arange(n2)[:, None, None]
    theta = 2.0 * np.pi * ((((n2 * a * k1)[None] + m2 * k1[None]) % n) / n)
    k2 = np.arange(n2)[:, None]
    b = np.arange(n2)[None, :]
    phi = 2.0 * np.pi * ((k2 * b) % n2) / n2
    c, s = np.cos(phi), np.sin(phi)
    fmat = np.block([[c, s], [-s, c]])
    return np.cos(theta), np.sin(theta), fmat


def _mix_forward(cos, sin, n1h):
    n2, k1p, _ = cos.shape
    g = n2 // SUBLANES
    a = np.stack([cos[:, :, :n1h], -sin[:, :, :n1h]], axis=1).reshape(g, SUBLANES, 2, k1p, n1h)
    m = np.zeros((g, 2, k1p, SUBLANES, n1h, SUBLANES))
    for j in range(SUBLANES):
        m[:, :, :, j, :, j] = a[:, j]
    return m.reshape(g, 2 * k1p * SUBLANES, n1h * SUBLANES)


def _mix_inverse(cos, sin, n1h, n1):
    n2, k1p, _ = cos.shape
    g = n2 // SUBLANES
    k1 = np.arange(k1p)
    weight = np.where((k1 == 0) | (k1 == n1 // 2), 1.0, np.where(k1 < n1 // 2, 2.0, 0.0)) / (n1 * n2)
    a = np.stack([cos[:, :, :n1h], -sin[:, :, :n1h]], axis=1) * weight[None, None, :, None]
    a = a.reshape(g, SUBLANES, 2, k1p, n1h)
    m = np.zeros((g, n1h, SUBLANES, 2, k1p, SUBLANES))
    for j in range(SUBLANES):
        m[:, :, j, :, :, j] = a[:, j].transpose(0, 3, 1, 2)
    return m.reshape(g, n1h * SUBLANES, 2 * k1p * SUBLANES)


def _stage_a_kernel(m_ref, x_ref, yr_ref, yi_ref):
    _, n1h, _, c = x_ref.shape
    k1p = yr_ref.shape[1]
    re, im = [], []
    xs = x_ref[0].astype(F32)
    for half in range(2):
        rows = slice(half * SUBLANES, (half + 1) * SUBLANES)
        x = xs[:, rows, :].reshape(n1h * SUBLANES, c).astype(BF16)
        y = jnp.dot(m_ref[half], x, preferred_element_type=F32)
        re.append(y[:k1p * SUBLANES].reshape(k1p, SUBLANES, c))
        im.append(y[k1p * SUBLANES:].reshape(k1p, SUBLANES, c))
    yr_ref[0] = jnp.concatenate(re, axis=1).astype(BF16)
    yi_ref[0] = jnp.concatenate(im, axis=1).astype(BF16)


def _stage_a(mix, vg4, k1p):
    b, n1h, n2, c = vg4.shape
    return pl.pallas_call(
        _stage_a_kernel,
        out_shape=(jax.ShapeDtypeStruct((b, k1p, n2, c), BF16),) * 2,
        grid=(n2 // PACKED_ROWS, b),
        in_specs=[pl.BlockSpec((2,) + mix.shape[1:], lambda g, bi: (g, 0, 0)),
                  pl.BlockSpec((1, n1h, PACKED_ROWS, c), lambda g, bi: (bi, 0, g, 0))],
        out_specs=(pl.BlockSpec((1, k1p, PACKED_ROWS, c), lambda g, bi: (bi, 0, g, 0)),) * 2,
        compiler_params=_params("parallel", "parallel"),
        name="hyena_dft_a",
    )(mix, vg4)


def _stage_b_kernel(yr_ref, yi_ref, f_ref, ft_ref, kr_ref, ki_ref, ur_ref, ui_ref):
    kb, n2 = yr_ref.shape[1:3]
    for r in range(kb):
        y = jnp.concatenate([yr_ref[0, r], yi_ref[0, r]], axis=0)
        z = jnp.dot(f_ref[...], y, preferred_element_type=F32)
        zr, zi = z[:n2], z[n2:]
        kr = kr_ref[r].astype(F32)
        ki = ki_ref[r].astype(F32)
        p = jnp.concatenate([zr * kr - zi * ki, zr * ki + zi * kr], axis=0).astype(BF16)
        u = jnp.dot(ft_ref[...], p, preferred_element_type=F32)
        ur_ref[0, r] = u[:n2].astype(BF16)
        ui_ref[0, r] = u[n2:].astype(BF16)


def _stage_b(yr, yi, fmat, fmat_t, kfr, kfi, kb):
    b, k1p, n2, c = yr.shape
    blk = pl.BlockSpec((1, kb, n2, c), lambda k, bi: (bi, k, 0, 0))
    flt = pl.BlockSpec((kb, n2, c), lambda k, bi: (k, 0, 0))
    mat = pl.BlockSpec((2 * n2, 2 * n2), lambda k, bi: (0, 0))
    return pl.pallas_call(
        _stage_b_kernel,
        out_shape=(jax.ShapeDtypeStruct((b, k1p, n2, c), BF16),) * 2,
        grid=(k1p // kb, b),
        in_specs=[blk, blk, mat, mat, flt, flt],
        out_specs=(blk, blk),
        compiler_params=_params("parallel", "parallel"),
        name="hyena_dft_b",
    )(yr, yi, fmat, fmat_t, kfr, kfi)


def _stage_c_kernel(m_ref, ur_ref, ui_ref, vg_ref, x0_ref, hb_ref, g_ref, o_ref):
    _, k1p, _, c = ur_ref.shape
    n1h = vg_ref.shape[1]
    ur = ur_ref[0].astype(F32)
    ui = ui_ref[0].astype(F32)
    vgs = vg_ref[0].astype(F32)
    x0s = x0_ref[0].astype(F32)
    for half in range(2):
        rows = slice(half * SUBLANES, (half + 1) * SUBLANES)
        u = jnp.concatenate([ur[:, rows, :].reshape(k1p * SUBLANES, c),
                             ui[:, rows, :].reshape(k1p * SUBLANES, c)], axis=0).astype(BF16)
        conv = jnp.dot(m_ref[half], u, preferred_element_type=F32)
        vg = vgs[:, rows, :].reshape(n1h * SUBLANES, c)
        x0 = x0s[:, rows, :].reshape(n1h * SUBLANES, c)
        hyo = (conv + hb_ref[...] * vg) * x0
        o_ref[0, :, rows, :] = (_rms(hyo) * g_ref[...]).reshape(n1h, SUBLANES, c)


def _stage_c(mix, ur, ui, vg4, x04, hy_bias, g_hy):
    b, n1h, n2, c = vg4.shape
    k1p = ur.shape[1]
    ublk = pl.BlockSpec((1, k1p, PACKED_ROWS, c), lambda g, bi: (bi, 0, g, 0))
    xblk = pl.BlockSpec((1, n1h, PACKED_ROWS, c), lambda g, bi: (bi, 0, g, 0))
    vec = pl.BlockSpec((1, c), lambda g, bi: (0, 0))
    return pl.pallas_call(
        _stage_c_kernel,
        out_shape=jax.ShapeDtypeStruct((b, n1h, n2, c), F32),
        grid=(n2 // PACKED_ROWS, b),
        in_specs=[pl.BlockSpec((2,) + mix.shape[1:], lambda g, bi: (g, 0, 0)),
                  ublk, ublk, xblk, xblk, vec, vec],
        out_specs=xblk,
        compiler_params=_params("parallel", "parallel"),
        name="hyena_dft_c",
    )(mix, ur, ui, vg4, x04, hy_bias.reshape(1, c), g_hy.reshape(1, c))


def _filter_a_kernel(a_ref, w1t_ref, w1c_ref, w1s_ref, b1_ref, w2_ref, b2_ref, w3_ref, b3_ref,
                     w4_ref, fr_ref, dec_ref, yr_ref, yi_ref, norm_ref, *, seq):
    g = pl.program_id(0)
    k1p = yr_ref.shape[0]
    c = norm_ref.shape[1]
    n1 = a_ref.shape[2]
    n1h = n1 // 2
    n2 = DFT_N2

    def lags(idx, j):
        tprime = idx * n2 + (g * SUBLANES + j)
        fwd = idx < n1h
        return tprime, fwd, jnp.where(fwd, tprime, 2 * seq - tprime).astype(F32)

    def stacked(idx, axis):
        parts = [lags(idx, j) for j in range(SUBLANES)]
        return [jnp.concatenate([p[k] for p in parts], axis=axis) for k in range(3)]

    _, _, pos_l = stacked(lax.broadcasted_iota(jnp.int32, (1, n1), 1), 1)
    tprime_r, fwd_r, pos_r = stacked(lax.broadcasted_iota(jnp.int32, (n1, 1), 0), 0)

    band = lax.broadcasted_iota(jnp.int32, (FILTER_BANDS, 1), 0).astype(F32)
    f = 1e-4 + band * ((FILTER_BANDS - 1 - 1e-4) / (FILTER_BANDS - 1))
    ang = f * ((2.0 * math.pi * pos_l) / seq)
    fr = fr_ref[...]
    dot = functools.partial(jnp.dot, preferred_element_type=F32, precision=_HI)
    h = w1t_ref[...] * (pos_l / (seq - 1.0)) + dot(w1c_ref[...], jnp.cos(ang)) + dot(w1s_ref[...], -jnp.sin(ang))
    h = jnp.sin(fr * (h + b1_ref[...]))
    h = jnp.sin(fr * (dot(w2_ref[...], h) + b2_ref[...]))
    h = jnp.sin(fr * (dot(w3_ref[...], h) + b3_ref[...]))
    h = lax.dot_general(h, w4_ref[...], (((0,), (0,)), ((), ())), preferred_element_type=F32)
    t_r = pos_r / (seq - 1.0)
    dec = jnp.abs(dec_ref[...])
    kf = h[:, :c] * jnp.exp(-t_r * dec[0:1, :])
    kb = h[:, c:] * jnp.exp(-t_r * dec[1:2, :])
    kk = jnp.where(fwd_r, kf, jnp.where(tprime_r == seq, 0.0, kb))

    @pl.when(g == 0)
    def _():
        norm_ref[...] = jnp.zeros_like(norm_ref)

    norm_ref[...] += jnp.sum(jnp.abs(kk), axis=0, keepdims=True)
    for j in range(SUBLANES):
        y = jnp.dot(a_ref[j], kk[j * n1:(j + 1) * n1].astype(BF16), preferred_element_type=F32)
        yr_ref[:, j * c:(j + 1) * c] = y[:k1p].astype(BF16)
        yi_ref[:, j * c:(j + 1) * c] = y[k1p:].astype(BF16)


def _filter_a(a_full, flt, seq, c):
    w1, b1, w2, b2, w3, b3, w4, freq, decay = flt
    hid = w2.shape[0]
    n2, k1p2, n1 = a_full.shape
    k1p = k1p2 // 2
    full = lambda shape: pl.BlockSpec(shape, lambda g: (0,) * len(shape))
    kern = functools.partial(_filter_a_kernel, seq=seq)
    return pl.pallas_call(
        kern,
        out_shape=(jax.ShapeDtypeStruct((k1p, n2 * c), BF16),
                   jax.ShapeDtypeStruct((k1p, n2 * c), BF16),
                   jax.ShapeDtypeStruct((1, c), F32)),
        grid=(n2 // SUBLANES,),
        in_specs=[pl.BlockSpec((SUBLANES, k1p2, n1), lambda g: (g, 0, 0)),
                  full((hid, 1)), full((hid, FILTER_BANDS)), full((hid, FILTER_BANDS)), full((hid, 1)),
                  full((hid, hid)), full((hid, 1)), full((hid, hid)), full((hid, 1)),
                  full((hid, 2 * c)), full((hid, 1)), full((2, c))],
        out_specs=(pl.BlockSpec((k1p, SUBLANES * c), lambda g: (0, g)),
                   pl.BlockSpec((k1p, SUBLANES * c), lambda g: (0, g)),
                   pl.BlockSpec((1, c), lambda g: (0, 0))),
        compiler_params=_params("arbitrary"),
        name="hyena_filter_a",
    )(a_full, w1[0:1].T, w1[1:1 + FILTER_BANDS].T, w1[1 + FILTER_BANDS:].T, b1.reshape(hid, 1),
      w2.T, b2.reshape(hid, 1), w3.T, b3.reshape(hid, 1), w4, freq.reshape(hid, 1), decay)


def _filter_b_kernel(yr_ref, yi_ref, f_ref, norm_ref, kr_ref, ki_ref):
    kb, n2 = yr_ref.shape[:2]
    for r in range(kb):
        y = jnp.concatenate([yr_ref[r], yi_ref[r]], axis=0)
        z = jnp.dot(f_ref[...], y, preferred_element_type=F32) / norm_ref[...]
        kr_ref[r] = z[:n2].astype(BF16)
        ki_ref[r] = z[n2:].astype(BF16)


def _filter_b(yr, yi, fmat, norm, kb):
    k1p, n2, c = yr.shape
    blk = pl.BlockSpec((kb, n2, c), lambda k: (k, 0, 0))
    return pl.pallas_call(
        _filter_b_kernel,
        out_shape=(jax.ShapeDtypeStruct((k1p, n2, c), BF16),) * 2,
        grid=(k1p // kb,),
        in_specs=[blk, blk,
                  pl.BlockSpec((2 * n2, 2 * n2), lambda k: (0, 0)),
                  pl.BlockSpec((1, c), lambda k: (0, 0))],
        out_specs=(blk, blk),
        compiler_params=_params("parallel"),
        name="hyena_filter_b",
    )(yr, yi, fmat, norm)


def _hyena(x0c, vg, flt, hy_bias, g_hy, batch, seq):
    c = vg.shape[1]
    n2 = DFT_N2
    n1h = seq // n2
    n1 = 2 * n1h
    k1p = min(n1, -(-(n1 // 2 + 1) // SUBLANES) * SUBLANES)
    kb = _pick_tile(k1p, 8, 1)
    cos, sin, fmat_np = _dft_tables(n1, n2, k1p)
    a_full = jnp.asarray(np.concatenate([cos, -sin], axis=1), BF16)
    mix_a = jnp.asarray(_mix_forward(cos, sin, n1h), BF16)
    mix_c = jnp.asarray(_mix_inverse(cos, sin, n1h, n1), BF16)
    fmat = jnp.asarray(fmat_np, BF16)
    fmat_t = jnp.asarray(fmat_np.T, BF16)

    fyr, fyi, norm = _filter_a(a_full, flt, seq, c)
    kfr, kfi = _filter_b(fyr.reshape(k1p, n2, c), fyi.reshape(k1p, n2, c), fmat, norm, kb)

    vg4 = vg.reshape(batch, n1h, n2, c)
    x04 = x0c.reshape(batch, n1h, n2, c)
    yr, yi = _stage_a(mix_a, vg4, k1p)
    ur, ui = _stage_b(yr, yi, fmat, fmat_t, kfr, kfi, kb)
    y = _stage_c(mix_c, ur, ui, vg4, x04, hy_bias, g_hy)
    return y.reshape(batch * seq, c)


def _out_proj_kernel(x_ref, gate_ref, attn_ref, hy_ref, ga_ref, wa_ref, wh_ref, o_ref):
    ya = (_rms(attn_ref[...].astype(F32)) * ga_ref[...]).astype(BF16)
    y = jnp.dot(ya, wa_ref[...], preferred_element_type=F32)
    y += jnp.dot(hy_ref[...].astype(BF16), wh_ref[...], preferred_element_type=F32)
    o_ref[...] = x_ref[...] + gate_ref[0] * y


def _out_proj(x, mod3, row_of_tile, attn, hyn, g_attn, w_out, tm):
    t, d = x.shape
    aw = attn.shape[1]
    hw = hyn.shape[1]
    const = lambda shape, idx: pl.BlockSpec(shape, lambda i: idx, pipeline_mode=pl.Buffered(1))
    return pl.pallas_call(
        _out_proj_kernel,
        out_shape=jax.ShapeDtypeStruct((t, d), F32),
        grid=(t // tm,),
        in_specs=[pl.BlockSpec((tm, d), lambda i: (i, 0)),
                  pl.BlockSpec((1, 1, d), lambda i: (row_of_tile(i), 0, 5)),
                  pl.BlockSpec((tm, aw), lambda i: (i, 0)),
                  pl.BlockSpec((tm, hw), lambda i: (i, 0)),
                  const((1, aw), (0, 0)), const((aw, d), (0, 0)), const((hw, d), (1, 0))],
        out_specs=pl.BlockSpec((tm, d), lambda i: (i, 0)),
        compiler_params=_params("parallel"),
        name="out_proj",
    )(x, mod3, attn, hyn, g_attn.reshape(1, aw), w_out, w_out)


def _rope_tables(seq):
    half = HEAD_DIM // 2
    pos = np.arange(seq)
    inv = ROPE_THETA ** (-np.arange(0, half, 2, dtype=np.float64) / half)
    ang = np.concatenate([(pos // GRID_W)[:, None] * inv, (pos % GRID_W)[:, None] * inv], axis=-1)
    cos = np.repeat(np.cos(ang), 2, axis=-1)
    sin = np.stack([-np.sin(ang), np.sin(ang)], axis=-1).reshape(seq, HEAD_DIM)
    return jnp.asarray(cos, F32), jnp.asarray(sin, F32)


def kernel(x, c, ctx, c_ctx, w_ada, b_ada, g_norm, w_ffn1_up, w_ffn1_down, w_ffn2_up, w_ffn2_down,
           w_in, q_norm, k_norm, conv_w, conv_b, flt_w1, flt_b1, flt_w2, flt_b2, flt_w3, flt_b3,
           flt_w4, flt_freq, flt_decay, hy_bias, g_out, w_out):
    batch, seq, d = x.shape
    lc = ctx.shape[1]
    depth = w_ada.shape[0]
    assert depth == 1, "context-update path of deeper stacks is not implemented"
    attn_w = d // 2
    kvw = N_KV_HEADS * HEAD_DIM
    tm = _pick_tile(seq, 512, 128)
    tmf = _pick_tile(seq, FFN_ROWS, 128)
    tmc = _pick_tile(batch * lc, 512, 8)
    lat_row = lambda i: i // (seq // tm)
    lat_row_ffn = lambda i: i // (seq // tmf)
    ctx_row = lambda i: batch

    xs = x.reshape(batch * seq, d)
    cs = ctx.reshape(batch * lc, d)
    rows = -(-(batch + 1) // 8) * 8
    c_cols = jnp.zeros((d, rows), F32).at[:, :batch].set(c.T).at[:, batch].set(c_ctx)
    cos, sin = _rope_tables(seq)

    l = 0
    mod3 = _ada_mod(c_cols, batch + 1, w_ada[l], b_ada[l]).reshape(rows, 1, N_MOD * d)

    w1u, w1d = w_ffn1_up[l].astype(BF16), w_ffn1_down[l].astype(BF16)
    qkv_w = attn_w + 2 * kvw
    qn = q_norm[l].reshape(1, HEAD_DIM)
    kn = k_norm[l].reshape(1, HEAD_DIM)

    x1, (wi, wo) = _ffn(xs, mod3, lat_row_ffn, 0, g_norm[l, 0], w1u, w1d, tmf, to_bf16=(w_in[l], w_out[l]))
    c1, _ = _ffn(cs, mod3, ctx_row, 0, g_norm[l, 0], w1u, w1d, tmc)

    q, k, v, x0c, vg = _in_proj(x1, mod3, lat_row, g_norm[l, 1], wi, qn, kn,
                                cos, sin, conv_w[l], conv_b[l], seq, tm)
    kc, vc = _ctx_kv(c1, mod3, batch, g_norm[l, 1], wi[:, attn_w:qkv_w], kn, tmc)
    k_all = jnp.concatenate([kc.reshape(batch, lc, kvw), k.reshape(batch, seq, kvw)], axis=1)
    v_all = jnp.concatenate([vc.reshape(batch, lc, kvw), v.reshape(batch, seq, kvw)], axis=1)
    attn, (w2u, w2d) = _attention(q, k_all, v_all, batch, seq, to_bf16=(w_ffn2_up[l], w_ffn2_down[l]))

    flt = (flt_w1[l], flt_b1[l], flt_w2[l], flt_b2[l], flt_w3[l], flt_b3[l], flt_w4[l],
           flt_freq[l], flt_decay[l])
    hyn = _hyena(x0c, vg, flt, hy_bias[l], g_out[l, attn_w:], batch, seq)
    x2 = _out_proj(x1, mod3, lat_row, attn, hyn, g_out[l, :attn_w], wo, tm)

    x3, _ = _ffn(x2, mod3, lat_row_ffn, 2, g_norm[l, 2], w2u, w2d, tmf)
    return x3.reshape(batch, seq, d)
```

```python
import functools
import math

import jax
import jax.numpy as jnp
import numpy as np
from jax import lax
from jax.experimental import pallas as pl
from jax.experimental.pallas import tpu as pltpu

F32 = jnp.float32
BF16 = jnp.bfloat16

HEAD_DIM = 128
N_KV_HEADS = 2
GRID_W = 64
ROPE_THETA = 10000.0
FILTER_BANDS = 16
RMS_EPS = 1e-6
N_MOD = 9

V7X_VMEM_LIMIT_BYTES = 56 * 1024 * 1024
SUBLANES = 8
PACKED_ROWS = 2 * SUBLANES
ADA_SLAB = 1024
FFN_ROWS = 1024
DFT_N2 = 128

_HI = lax.Precision.HIGHEST


def _params(*sem):
    return pltpu.CompilerParams(dimension_semantics=sem, vmem_limit_bytes=V7X_VMEM_LIMIT_BYTES)


def _pick_tile(n, cap, mult):
    best = None
    t = mult
    while t <= min(n, cap):
        if n % t == 0:
            best = t
        t += mult
    assert best is not None, (n, cap, mult)
    return best


def _rms(x, eps=RMS_EPS):
    return x * lax.rsqrt(jnp.mean(x * x, axis=-1, keepdims=True) + eps)


def _side_cast_specs(mats, n_steps, step_of):
    specs = []
    for w in mats:
        rb = PACKED_ROWS * -(-w.shape[0] // (PACKED_ROWS * n_steps))
        assert w.shape[0] % rb == 0, (w.shape, rb)
        nblk = w.shape[0] // rb
        specs.append(pl.BlockSpec((rb, w.shape[1]),
                                  lambda *idx, nblk=nblk: (jnp.minimum(step_of(*idx), nblk - 1), 0)))
    return specs


def _side_cast(side_in, side_out):
    for src, dst in zip(side_in, side_out):
        dst[...] = src[...].astype(BF16)


def _ada_kernel(ct_ref, w_ref, b_ref, o_ref, sb_ref, *, n_rows):
    lanes = sb_ref.shape[2]

    @pl.when(pl.program_id(0) == 0)
    def _():
        ct = ct_ref[...]
        s = ct * jax.nn.sigmoid(ct)
        for r in range(n_rows):
            sb_ref[r] = jnp.broadcast_to(s[:, r:r + 1], sb_ref.shape[1:])

    d, tn = w_ref.shape
    slab = _pick_tile(tn, ADA_SLAB, lanes)
    reps = slab // lanes
    for c0 in range(0, tn, slab):
        def body(kb, acc):
            rows = pl.ds(pl.multiple_of(kb * SUBLANES, SUBLANES), SUBLANES)
            w = w_ref[rows, c0:c0 + slab]
            return tuple(a + w * jnp.concatenate([sb_ref[r, rows, :]] * reps, axis=1)
                         for r, a in enumerate(acc))

        acc = lax.fori_loop(0, d // SUBLANES, body,
                            tuple(jnp.zeros((SUBLANES, slab), F32) for _ in range(n_rows)), unroll=8)
        out = [jnp.sum(a, axis=0, keepdims=True) for a in acc]
        out.append(jnp.zeros((o_ref.shape[0] - n_rows, slab), F32))
        o_ref[:, c0:c0 + slab] = jnp.concatenate(out, axis=0) + b_ref[:, c0:c0 + slab]


def _ada_mod(c_cols, n_rows, w_ada, b_ada):
    d, rows = c_cols.shape
    n = w_ada.shape[1]
    tn = _pick_tile(n, 2048, 128)
    return pl.pallas_call(
        functools.partial(_ada_kernel, n_rows=n_rows),
        out_shape=jax.ShapeDtypeStruct((rows, n), F32),
        grid=(n // tn,),
        in_specs=[pl.BlockSpec((d, rows), lambda j: (0, 0)),
                  pl.BlockSpec((d, tn), lambda j: (0, j)),
                  pl.BlockSpec((1, tn), lambda j: (0, j))],
        out_specs=pl.BlockSpec((rows, tn), lambda j: (0, j)),
        scratch_shapes=[pltpu.VMEM((n_rows, d, HEAD_DIM), F32)],
        compiler_params=_params("arbitrary"),
        name="ada_mod",
    )(c_cols, w_ada, b_ada.reshape(1, n))


def _ffn_kernel(x_ref, shift_ref, scale_ref, gate_ref, g_ref, wg_ref, wu_ref, wd_ref, *refs, tn, n_side):
    side_in, o_ref, side_out, h_ref = refs[:n_side], refs[n_side], refs[n_side + 1:-1], refs[-1]
    _side_cast(side_in, side_out)
    k = pl.program_id(1)
    last = pl.num_programs(1) - 1
    d = o_ref.shape[1]

    def chunk(h, emit):
        g = jnp.dot(h, wg_ref[...], preferred_element_type=F32)
        u = jnp.dot(h, wu_ref[...], preferred_element_type=F32)
        a = (g * jax.nn.sigmoid(g) * u).astype(BF16)
        for n0 in range(0, d, tn):
            emit(n0, jnp.dot(a, wd_ref[:, n0:n0 + tn], preferred_element_type=F32))

    @pl.when(k == 0)
    def _():
        y = _rms(x_ref[...]) * g_ref[...]
        h = (y * (1.0 + scale_ref[0]) + shift_ref[0]).astype(BF16)
        h_ref[...] = h

        def emit(n0, part):
            o_ref[:, n0:n0 + tn] = part
        chunk(h, emit)

    @pl.when(jnp.logical_and(k > 0, k < last))
    def _():
        def emit(n0, part):
            o_ref[:, n0:n0 + tn] += part
        chunk(h_ref[...], emit)

    @pl.when(jnp.logical_and(k > 0, k == last))
    def _():
        def emit(n0, part):
            cols = slice(n0, n0 + tn)
            o_ref[:, cols] = x_ref[:, cols] + 0.5 * gate_ref[0][:, cols] * (o_ref[:, cols] + part)
        chunk(h_ref[...], emit)


def _ffn(x, mod3, row_of_tile, slot, g, w_up, w_down, tm, to_bf16=()):
    t, d = x.shape
    f = w_down.shape[0]
    tf = _pick_tile(f, 512, 128)
    nf = f // tf
    assert nf >= 2, "the kernel keeps separate first / last chunk paths"
    mod_spec = lambda c: pl.BlockSpec((1, 1, d), lambda i, k: (row_of_tile(i), 0, 3 * slot + c))
    row_tile = pl.BlockSpec((tm, d), lambda i, k: (i, 0))
    side_specs = _side_cast_specs(to_bf16, (t // tm) * nf, lambda i, k: i * nf + k)
    outs = pl.pallas_call(
        functools.partial(_ffn_kernel, tn=_pick_tile(d, 512, 128), n_side=len(to_bf16)),
        out_shape=(jax.ShapeDtypeStruct((t, d), F32), *[jax.ShapeDtypeStruct(w.shape, BF16) for w in to_bf16]),
        grid=(t // tm, nf),
        in_specs=[row_tile, mod_spec(0), mod_spec(1), mod_spec(2),
                  pl.BlockSpec((1, d), lambda i, k: (0, 0)),
                  pl.BlockSpec((d, tf), lambda i, k: (0, k)),
                  pl.BlockSpec((d, tf), lambda i, k: (0, k + nf)),
                  pl.BlockSpec((tf, d), lambda i, k: (k, 0)),
                  *side_specs],
        out_specs=(row_tile, *side_specs),
        scratch_shapes=[pltpu.VMEM((tm, d), BF16)],
        compiler_params=_params("arbitrary", "arbitrary"),
        name="ffn",
    )(x, mod3, mod3, mod3, g.reshape(1, d), w_up, w_up, w_down, *to_bf16)
    return outs[0], outs[1:]


def _rope(x, cos, sin):
    lane = lax.broadcasted_iota(jnp.int32, x.shape, 1)
    partner = jnp.where(lane % 2 == 0, pltpu.roll(x, HEAD_DIM - 1, axis=1), pltpu.roll(x, 1, axis=1))
    return x * cos + partner * sin


HALO = 16


def _in_proj_kernel(x_ref, xp_ref, xn_ref, shift_ref, scale_ref, g_ref, w_ref, qn_ref, kn_ref,
                    cos_ref, sin_ref, cw_ref, cb_ref, q_ref, k_ref, v_ref, x0c_ref, vg_ref, *,
                    q_scale, tiles_per_seq, cw):
    i = pl.program_id(0)
    tm = x_ref.shape[0]
    attn_w = q_ref.shape[1]
    kvw = k_ref.shape[1]
    c = x0c_ref.shape[1]
    first = (i % tiles_per_seq) == 0
    last = (i % tiles_per_seq) == tiles_per_seq - 1

    x_ext = jnp.concatenate([xp_ref[...], x_ref[...], xn_ref[...]], axis=0)
    y = _rms(x_ext) * g_ref[...]
    h_ext = (y * (1.0 + scale_ref[0]) + shift_ref[0]).astype(BF16)
    h = h_ext[HALO:HALO + tm]

    qkv_w = attn_w + 2 * kvw
    u = jnp.dot(h, w_ref[:, :qkv_w], preferred_element_type=F32)

    def normed(col, gain):
        return _rope(_rms(u[:, col:col + HEAD_DIM]) * gain, cos_ref[...], sin_ref[...])

    for hd in range(attn_w // HEAD_DIM):
        col = hd * HEAD_DIM
        q_ref[:, col:col + HEAD_DIM] = (normed(col, qn_ref[...]) * q_scale).astype(BF16)
    for hd in range(kvw // HEAD_DIM):
        col = hd * HEAD_DIM
        k_ref[:, col:col + HEAD_DIM] = normed(attn_w + col, kn_ref[...]).astype(BF16)
    v_ref[...] = u[:, attn_w + kvw:].astype(BF16)

    row = lax.broadcasted_iota(jnp.int32, (tm, cw), 0)
    at_start = jnp.logical_and(first, row == 0)
    at_end = jnp.logical_and(last, row == tm - 1)
    ext = tm + 2 * HALO

    def conv(group, c0):
        col = qkv_w + group * c + c0
        ue = jnp.dot(h_ext, w_ref[:, col:col + cw], preferred_element_type=F32)
        um = jnp.where(at_start, 0.0, pltpu.roll(ue, 1, axis=0)[HALO:HALO + tm])
        up = jnp.where(at_end, 0.0, pltpu.roll(ue, ext - 1, axis=0)[HALO:HALO + tm])
        w = cw_ref[group][:, c0:c0 + cw]
        return um * w[0:1] + ue[HALO:HALO + tm] * w[1:2] + up * w[2:3] + cb_ref[group][:, c0:c0 + cw]

    for c0 in range(0, c, cw):
        x0c_ref[:, c0:c0 + cw] = conv(0, c0).astype(BF16)
        vg_ref[:, c0:c0 + cw] = (conv(2, c0) * conv(1, c0)).astype(BF16)


def _in_proj(x, mod3, row_of_tile, g, w_in, qn, kn, cos, sin, conv_w, conv_b, seq, tm):
    t, d = x.shape
    kvw = N_KV_HEADS * HEAD_DIM
    attn_w = d // 2
    c = (w_in.shape[1] - attn_w - 2 * kvw) // 3
    cw = _pick_tile(c, 512, 128)
    pos_tiles = seq // tm
    hb = tm // HALO
    nhb = t // HALO
    w3 = conv_w.reshape(3, 3, c).transpose(1, 0, 2)
    b3 = conv_b.reshape(3, 1, c)
    mod_spec = lambda ch: pl.BlockSpec((1, 1, d), lambda i: (row_of_tile(i), 0, 3 + ch))
    const = lambda shape: pl.BlockSpec(shape, lambda i: (0,) * len(shape), pipeline_mode=pl.Buffered(1))
    kern = functools.partial(_in_proj_kernel, q_scale=HEAD_DIM ** -0.5 * math.log2(math.e),
                             tiles_per_seq=pos_tiles, cw=cw)
    return pl.pallas_call(
        kern,
        out_shape=(jax.ShapeDtypeStruct((t, attn_w), BF16),
                   jax.ShapeDtypeStruct((t, kvw), BF16),
                   jax.ShapeDtypeStruct((t, kvw), BF16),
                   jax.ShapeDtypeStruct((t, c), BF16),
                   jax.ShapeDtypeStruct((t, c), BF16)),
        grid=(t // tm,),
        in_specs=[pl.BlockSpec((tm, d), lambda i: (i, 0)),
                  pl.BlockSpec((HALO, d), lambda i: (jnp.maximum(i * hb - 1, 0), 0)),
                  pl.BlockSpec((HALO, d), lambda i: (jnp.minimum((i + 1) * hb, nhb - 1), 0)),
                  mod_spec(0), mod_spec(1),
                  const((1, d)), const(w_in.shape),
                  const((1, HEAD_DIM)), const((1, HEAD_DIM)),
                  pl.BlockSpec((tm, HEAD_DIM), lambda i: (i % pos_tiles, 0)),
                  pl.BlockSpec((tm, HEAD_DIM), lambda i: (i % pos_tiles, 0)),
                  const((3, 3, c)), const((3, 1, c))],
        out_specs=(pl.BlockSpec((tm, attn_w), lambda i: (i, 0)),
                   pl.BlockSpec((tm, kvw), lambda i: (i, 0)),
                   pl.BlockSpec((tm, kvw), lambda i: (i, 0)),
                   pl.BlockSpec((tm, c), lambda i: (i, 0)),
                   pl.BlockSpec((tm, c), lambda i: (i, 0))),
        compiler_params=_params("parallel"),
        name="in_proj",
    )(x, x, x, mod3, mod3, g.reshape(1, d), w_in, qn, kn, cos, sin, w3, b3)


def _ctx_kv_kernel(x_ref, shift_ref, scale_ref, g_ref, w_ref, kn_ref, k_ref, v_ref):
    y = _rms(x_ref[...]) * g_ref[...]
    h = (y * (1.0 + scale_ref[0]) + shift_ref[0]).astype(BF16)
    u = jnp.dot(h, w_ref[...], preferred_element_type=F32)
    for hd in range(N_KV_HEADS):
        k_ref[:, hd * HEAD_DIM:(hd + 1) * HEAD_DIM] = (
            _rms(u[:, hd * HEAD_DIM:(hd + 1) * HEAD_DIM]) * kn_ref[...]).astype(BF16)
    v_ref[...] = u[:, N_KV_HEADS * HEAD_DIM:].astype(BF16)


def _ctx_kv(x, mod3, ctx_row, g, w_kv, kn, tm):
    t, d = x.shape
    kvw = N_KV_HEADS * HEAD_DIM
    mod_spec = lambda c: pl.BlockSpec((1, 1, d), lambda i: (ctx_row, 0, 3 + c))
    return pl.pallas_call(
        _ctx_kv_kernel,
        out_shape=(jax.ShapeDtypeStruct((t, kvw), BF16), jax.ShapeDtypeStruct((t, kvw), BF16)),
        grid=(t // tm,),
        in_specs=[pl.BlockSpec((tm, d), lambda i: (i, 0)),
                  mod_spec(0), mod_spec(1),
                  pl.BlockSpec((1, d), lambda i: (0, 0)),
                  pl.BlockSpec((d, 2 * kvw), lambda i: (0, 0)),
                  pl.BlockSpec((1, HEAD_DIM), lambda i: (0, 0))],
        out_specs=(pl.BlockSpec((tm, kvw), lambda i: (i, 0)),
                   pl.BlockSpec((tm, kvw), lambda i: (i, 0))),
        compiler_params=_params("parallel"),
        name="ctx_kv",
    )(x, mod3, mod3, g.reshape(1, d), w_kv, kn)


def _attn_kernel(q_ref, k_ref, v_ref, *refs, group, nq, n_side):
    side_in, o_ref, side_out = refs[:n_side], refs[n_side], refs[n_side + 1:2 * n_side + 1]
    s0_ref, s1_ref, mx0_ref, mx1_ref, ls_ref, acc_ref = refs[2 * n_side + 1:]
    _side_cast(side_in, side_out)
    i = pl.program_id(2)
    tq = q_ref.shape[0]
    nk, _, tk = s0_ref.shape
    lanes = HEAD_DIM
    bufs = ((s0_ref, mx0_ref), (s1_ref, mx1_ref))

    def scores(j, q, s_ref, mx_ref):
        start = pl.multiple_of(j * tk, tk)
        s = lax.dot_general(q, k_ref[0, pl.ds(start, tk), :], (((1,), (1,)), ((), ())),
                            preferred_element_type=F32)
        s_ref[j] = s
        mx = mx_ref[...]
        for c in range(tk // lanes):
            mx = jnp.maximum(mx, s[:, c * lanes:(c + 1) * lanes])
        mx_ref[...] = mx

    def values(j, s_ref, mx_ref):
        start = pl.multiple_of(j * tk, tk)
        s = s_ref[j]
        m = mx_ref[...]
        ls = ls_ref[...]
        parts = []
        for c in range(tk // lanes):
            pc = jnp.exp2(s[:, c * lanes:(c + 1) * lanes] - m)
            ls = ls + pc
            parts.append(pc.astype(BF16))
        ls_ref[...] = ls
        p = jnp.concatenate(parts, axis=1)
        acc_ref[...] += jnp.dot(p, v_ref[0, pl.ds(start, tk), :], preferred_element_type=F32)

    def begin_scores(mx_ref):
        mx_ref[...] = jnp.full_like(mx_ref, -jnp.inf)
        return jnp.concatenate([q_ref[:, h * HEAD_DIM:(h + 1) * HEAD_DIM] for h in range(group)], axis=0)

    def end_scores(mx_ref):
        mx_ref[...] = jnp.broadcast_to(jnp.max(mx_ref[...], axis=-1, keepdims=True), mx_ref.shape)

    def begin_values():
        ls_ref[...] = jnp.zeros_like(ls_ref)
        acc_ref[...] = jnp.zeros_like(acc_ref)

    def end_values():
        o = acc_ref[...] / jnp.sum(ls_ref[...], axis=-1, keepdims=True)
        for h in range(group):
            o_ref[:, h * HEAD_DIM:(h + 1) * HEAD_DIM] = o[h * tq:(h + 1) * tq].astype(BF16)

    @pl.when(i == 0)
    def _():
        s_ref, mx_ref = bufs[0]
        q = begin_scores(mx_ref)
        lax.fori_loop(0, nk, lambda j, c: (scores(j, q, s_ref, mx_ref), c)[1], 0)
        end_scores(mx_ref)

    for parity in range(2):
        @pl.when(jnp.logical_and(jnp.logical_and(i > 0, i < nq), i % 2 == parity))
        def _():
            (cs_ref, cmx_ref), (ps_ref, pmx_ref) = bufs[parity], bufs[1 - parity]
            q = begin_scores(cmx_ref)
            begin_values()

            def both(j, c):
                scores(j, q, cs_ref, cmx_ref)
                values(j, ps_ref, pmx_ref)
                return c

            lax.fori_loop(0, nk, both, 0, unroll=True)
            end_scores(cmx_ref)
            end_values()

    @pl.when(i == nq)
    def _():
        s_ref, mx_ref = bufs[(nq - 1) % 2]
        begin_values()
        lax.fori_loop(0, nk, lambda j, c: (values(j, s_ref, mx_ref), c)[1], 0)
        end_values()


def _attention(q, k_all, v_all, batch, seq, to_bf16=()):
    t, attn_w = q.shape
    lk = k_all.shape[1]
    group = attn_w // HEAD_DIM // N_KV_HEADS
    gw = group * HEAD_DIM
    tq = _pick_tile(seq, 128, 8)
    tk = _pick_tile(lk, 1408, 128)
    nq = seq // tq
    rows = group * tq
    side_specs = _side_cast_specs(to_bf16, batch * N_KV_HEADS * (nq + 1),
                                  lambda b, g, i: (b * N_KV_HEADS + g) * (nq + 1) + i)
    kern = functools.partial(_attn_kernel, group=group, nq=nq, n_side=len(to_bf16))
    outs = pl.pallas_call(
        kern,
        out_shape=(jax.ShapeDtypeStruct((t, attn_w), BF16),
                   *[jax.ShapeDtypeStruct(w.shape, BF16) for w in to_bf16]),
        grid=(batch, N_KV_HEADS, nq + 1),
        in_specs=[pl.BlockSpec((tq, gw), lambda b, g, i: (b * nq + jnp.minimum(i, nq - 1), g)),
                  pl.BlockSpec((1, lk, HEAD_DIM), lambda b, g, i: (b, 0, g)),
                  pl.BlockSpec((1, lk, HEAD_DIM), lambda b, g, i: (b, 0, g)),
                  *side_specs],
        out_specs=(pl.BlockSpec((tq, gw), lambda b, g, i: (b * nq + jnp.maximum(i - 1, 0), g)),
                   *side_specs),
        scratch_shapes=[pltpu.VMEM((lk // tk, rows, tk), F32), pltpu.VMEM((lk // tk, rows, tk), F32),
                        pltpu.VMEM((rows, HEAD_DIM), F32), pltpu.VMEM((rows, HEAD_DIM), F32),
                        pltpu.VMEM((rows, HEAD_DIM), F32), pltpu.VMEM((rows, HEAD_DIM), F32)],
        compiler_params=_params("arbitrary", "arbitrary", "arbitrary"),
        name="attention",
    )(q, k_all, v_all, *to_bf16)
    return outs[0], outs[1:]


def _dft_tables(n1, n2, k1p):
    n = n1 * n2
    k1 = np.arange(k1p)[:, None]
    a = np.arange(n1)[None, :]
    m2 = np.arange(n2)[:, None, None]
    theta = 2.0 * np.pi * ((((n2 * a * k1)[None] + m2 * k1[None]) % n) / n)
    k2 = np.arange(n2)[:, None]
    b = np.arange(n2)[None, :]
    phi = 2.0 * np.pi * ((k2 * b) % n2) / n2
    c, s = np.cos(phi), np.sin(phi)
    fmat = np.block([[c, s], [-s, c]])
    return np.cos(theta), np.sin(theta), fmat


def _mix_forward(cos, sin, n1h):
    n2, k1p, _ = cos.shape
    g = n2 // SUBLANES
    a = np.stack([cos[:, :, :n1h], -sin[:, :, :n1h]], axis=1).reshape(g, SUBLANES, 2, k1p, n1h)
    m = np.zeros((g, 2, k1p, SUBLANES, n1h, SUBLANES))
    for j in range(SUBLANES):
        m[:, :, :, j, :, j] = a[:, j]
    return m.reshape(g, 2 * k1p * SUBLANES, n1h * SUBLANES)


def _mix_inverse(cos, sin, n1h, n1):
    n2, k1p, _ = cos.shape
    g = n2 // SUBLANES
    k1 = np.arange(k1p)
    weight = np.where((k1 == 0) | (k1 == n1 // 2), 1.0, np.where(k1 < n1 // 2, 2.0, 0.0)) / (n1 * n2)
    a = np.stack([cos[:, :, :n1h], -sin[:, :, :n1h]], axis=1) * weight[None, None, :, None]
    a = a.reshape(g, SUBLANES, 2, k1p, n1h)
    m = np.zeros((g, n1h, SUBLANES, 2, k1p, SUBLANES))
    for j in range(SUBLANES):
        m[:, :, j, :, :, j] = a[:, j].transpose(0, 3, 1, 2)
    return m.reshape(g, n1h * SUBLANES, 2 * k1p * SUBLANES)


def _stage_a_kernel(m_ref, x_ref, yr_ref, yi_ref):
    _, n1h, _, c = x_ref.shape
    k1p = yr_ref.shape[1]
    re, im = [], []
    xs = x_ref[0].astype(F32)
    for half in range(2):
        rows = slice(half * SUBLANES, (half + 1) * SUBLANES)
        x = xs[:, rows, :].reshape(n1h * SUBLANES, c).astype(BF16)
        y = jnp.dot(m_ref[half], x, preferred_element_type=F32)
        re.append(y[:k1p * SUBLANES].reshape(k1p, SUBLANES, c))
        im.append(y[k1p * SUBLANES:].reshape(k1p, SUBLANES, c))
    yr_ref[0] = jnp.concatenate(re, axis=1).astype(BF16)
    yi_ref[0] = jnp.concatenate(im, axis=1).astype(BF16)


def _stage_a(mix, vg4, k1p):
    b, n1h, n2, c = vg4.shape
    return pl.pallas_call(
        _stage_a_kernel,
        out_shape=(jax.ShapeDtypeStruct((b, k1p, n2, c), BF16),) * 2,
        grid=(n2 // PACKED_ROWS, b),
        in_specs=[pl.BlockSpec((2,) + mix.shape[1:], lambda g, bi: (g, 0, 0)),
                  pl.BlockSpec((1, n1h, PACKED_ROWS, c), lambda g, bi: (bi, 0, g, 0))],
        out_specs=(pl.BlockSpec((1, k1p, PACKED_ROWS, c), lambda g, bi: (bi, 0, g, 0)),) * 2,
        compiler_params=_params("parallel", "parallel"),
        name="hyena_dft_a",
    )(mix, vg4)


def _stage_b_kernel(yr_ref, yi_ref, f_ref, ft_ref, kr_ref, ki_ref, ur_ref, ui_ref):
    kb, n2 = yr_ref.shape[1:3]
    for r in range(kb):
        y = jnp.concatenate([yr_ref[0, r], yi_ref[0, r]], axis=0)
        z = jnp.dot(f_ref[...], y, preferred_element_type=F32)
        zr, zi = z[:n2], z[n2:]
        kr = kr_ref[r].astype(F32)
        ki = ki_ref[r].astype(F32)
        p = jnp.concatenate([zr * kr - zi * ki, zr * ki + zi * kr], axis=0).astype(BF16)
        u = jnp.dot(ft_ref[...], p, preferred_element_type=F32)
        ur_ref[0, r] = u[:n2].astype(BF16)
        ui_ref[0, r] = u[n2:].astype(BF16)


def _stage_b(yr, yi, fmat, fmat_t, kfr, kfi, kb):
    b, k1p, n2, c = yr.shape
    blk = pl.BlockSpec((1, kb, n2, c), lambda k, bi: (bi, k, 0, 0))
    flt = pl.BlockSpec((kb, n2, c), lambda k, bi: (k, 0, 0))
    mat = pl.BlockSpec((2 * n2, 2 * n2), lambda k, bi: (0, 0))
    return pl.pallas_call(
        _stage_b_kernel,
        out_shape=(jax.ShapeDtypeStruct((b, k1p, n2, c), BF16),) * 2,
        grid=(k1p // kb, b),
        in_specs=[blk, blk, mat, mat, flt, flt],
        out_specs=(blk, blk),
        compiler_params=_params("parallel", "parallel"),
        name="hyena_dft_b",
    )(yr, yi, fmat, fmat_t, kfr, kfi)


def _stage_c_kernel(m_ref, ur_ref, ui_ref, vg_ref, x0_ref, hb_ref, g_ref, o_ref):
    _, k1p, _, c = ur_ref.shape
    n1h = vg_ref.shape[1]
    ur = ur_ref[0].astype(F32)
    ui = ui_ref[0].astype(F32)
    vgs = vg_ref[0].astype(F32)
    x0s = x0_ref[0].astype(F32)
    for half in range(2):
        rows = slice(half * SUBLANES, (half + 1) * SUBLANES)
        u = jnp.concatenate([ur[:, rows, :].reshape(k1p * SUBLANES, c),
                             ui[:, rows, :].reshape(k1p * SUBLANES, c)], axis=0).astype(BF16)
        conv = jnp.dot(m_ref[half], u, preferred_element_type=F32)
        vg = vgs[:, rows, :].reshape(n1h * SUBLANES, c)
        x0 = x0s[:, rows, :].reshape(n1h * SUBLANES, c)
        hyo = (conv + hb_ref[...] * vg) * x0
        o_ref[0, :, rows, :] = (_rms(hyo) * g_ref[...]).reshape(n1h, SUBLANES, c)


def _stage_c(mix, ur, ui, vg4, x04, hy_bias, g_hy):
    b, n1h, n2, c = vg4.shape
    k1p = ur.shape[1]
    ublk = pl.BlockSpec((1, k1p, PACKED_ROWS, c), lambda g, bi: (bi, 0, g, 0))
    xblk = pl.BlockSpec((1, n1h, PACKED_ROWS, c), lambda g, bi: (bi, 0, g, 0))
    vec = pl.BlockSpec((1, c), lambda g, bi: (0, 0))
    return pl.pallas_call(
        _stage_c_kernel,
        out_shape=jax.ShapeDtypeStruct((b, n1h, n2, c), F32),
        grid=(n2 // PACKED_ROWS, b),
        in_specs=[pl.BlockSpec((2,) + mix.shape[1:], lambda g, bi: (g, 0, 0)),
                  ublk, ublk, xblk, xblk, vec, vec],
        out_specs=xblk,
        compiler_params=_params("parallel", "parallel"),
        name="hyena_dft_c",
    )(mix, ur, ui, vg4, x04, hy_bias.reshape(1, c), g_hy.reshape(1, c))


def _filter_a_kernel(a_ref, w1t_ref, w1c_ref, w1s_ref, b1_ref, w2_ref, b2_ref, w3_ref, b3_ref,
                     w4_ref, fr_ref, dec_ref, pr_ref, pi_ref, mr_ref, mi_ref, norm_ref, *, seq):
    g = pl.program_id(0)
    k1p = pr_ref.shape[0]
    c = norm_ref.shape[1]
    n1h = a_ref.shape[2]
    n2 = DFT_N2

    def positions(idx, axis):
        return jnp.concatenate([idx * n2 + (g * SUBLANES + j) for j in range(SUBLANES)], axis=axis)

    pos_l = positions(lax.broadcasted_iota(jnp.int32, (1, n1h), 1), 1).astype(F32)
    ipos_r = positions(lax.broadcasted_iota(jnp.int32, (n1h, 1), 0), 0)
    pos_r = ipos_r.astype(F32)

    band = lax.broadcasted_iota(jnp.int32, (FILTER_BANDS, 1), 0).astype(F32)
    f = 1e-4 + band * ((FILTER_BANDS - 1 - 1e-4) / (FILTER_BANDS - 1))
    ang = f * ((2.0 * math.pi * pos_l) / seq)
    fr = fr_ref[...]
    dot = functools.partial(jnp.dot, preferred_element_type=F32, precision=_HI)
    h = w1t_ref[...] * (pos_l / (seq - 1.0)) + dot(w1c_ref[...], jnp.cos(ang)) + dot(w1s_ref[...], -jnp.sin(ang))
    h = jnp.sin(fr * (h + b1_ref[...]))
    h = jnp.sin(fr * (dot(w2_ref[...], h) + b2_ref[...]))
    h = jnp.sin(fr * (dot(w3_ref[...], h) + b3_ref[...]))
    h = lax.dot_general(h, w4_ref[...], (((0,), (0,)), ((), ())), preferred_element_type=F32)
    t_r = pos_r / (seq - 1.0)
    dec = jnp.abs(dec_ref[...])
    kf = h[:, :c] * jnp.exp(-t_r * dec[0:1, :])
    kb = jnp.where(ipos_r == 0, 0.0, h[:, c:] * jnp.exp(-t_r * dec[1:2, :]))

    @pl.when(g == 0)
    def _():
        norm_ref[...] = jnp.zeros_like(norm_ref)

    norm_ref[...] += jnp.sum(jnp.abs(kf) + jnp.abs(kb), axis=0, keepdims=True)
    ks = (kf + kb).astype(BF16)
    kd = (kf - kb).astype(BF16)
    for j in range(SUBLANES):
        rows = slice(j * n1h, (j + 1) * n1h)
        cols = slice(j * c, (j + 1) * c)
        y = jnp.dot(a_ref[j], ks[rows], preferred_element_type=F32)
        pr_ref[:, cols] = y[:k1p].astype(BF16)
        pi_ref[:, cols] = y[k1p:].astype(BF16)
        y = jnp.dot(a_ref[j], kd[rows], preferred_element_type=F32)
        mr_ref[:, cols] = y[:k1p].astype(BF16)
        mi_ref[:, cols] = y[k1p:].astype(BF16)


def _filter_a(a_half, flt, seq, c):
    w1, b1, w2, b2, w3, b3, w4, freq, decay = flt
    hid = w2.shape[0]
    n2, k1p2, n1h = a_half.shape
    k1p = k1p2 // 2
    full = lambda shape: pl.BlockSpec(shape, lambda g: (0,) * len(shape))
    spectrum = pl.BlockSpec((k1p, SUBLANES * c), lambda g: (0, g))
    kern = functools.partial(_filter_a_kernel, seq=seq)
    return pl.pallas_call(
        kern,
        out_shape=(*[jax.ShapeDtypeStruct((k1p, n2 * c), BF16)] * 4, jax.ShapeDtypeStruct((1, c), F32)),
        grid=(n2 // SUBLANES,),
        in_specs=[pl.BlockSpec((SUBLANES, k1p2, n1h), lambda g: (g, 0, 0)),
                  full((hid, 1)), full((hid, FILTER_BANDS)), full((hid, FILTER_BANDS)), full((hid, 1)),
                  full((hid, hid)), full((hid, 1)), full((hid, hid)), full((hid, 1)),
                  full((hid, 2 * c)), full((hid, 1)), full((2, c))],
        out_specs=(spectrum, spectrum, spectrum, spectrum, pl.BlockSpec((1, c), lambda g: (0, 0))),
        compiler_params=_params("arbitrary"),
        name="hyena_filter_a",
    )(a_half, w1[0:1].T, w1[1:1 + FILTER_BANDS].T, w1[1 + FILTER_BANDS:].T, b1.reshape(hid, 1),
      w2.T, b2.reshape(hid, 1), w3.T, b3.reshape(hid, 1), w4, freq.reshape(hid, 1), decay)


def _filter_b_kernel(pr_ref, pi_ref, mr_ref, mi_ref, f_ref, norm_ref, kr_ref, ki_ref):
    kb, n2 = pr_ref.shape[:2]
    for r in range(kb):
        p = jnp.concatenate([pr_ref[r], pi_ref[r]], axis=0)
        m = jnp.concatenate([mr_ref[r], mi_ref[r]], axis=0)
        kr_ref[r] = (jnp.dot(f_ref[:n2], p, preferred_element_type=F32) / norm_ref[...]).astype(BF16)
        ki_ref[r] = (jnp.dot(f_ref[n2:], m, preferred_element_type=F32) / norm_ref[...]).astype(BF16)


def _filter_b(spectra, fmat, norm, kb):
    k1p, n2, c = spectra[0].shape
    blk = pl.BlockSpec((kb, n2, c), lambda k: (k, 0, 0))
    return pl.pallas_call(
        _filter_b_kernel,
        out_shape=(jax.ShapeDtypeStruct((k1p, n2, c), BF16),) * 2,
        grid=(k1p // kb,),
        in_specs=[blk, blk, blk, blk,
                  pl.BlockSpec((2 * n2, 2 * n2), lambda k: (0, 0)),
                  pl.BlockSpec((1, c), lambda k: (0, 0))],
        out_specs=(blk, blk),
        compiler_params=_params("parallel"),
        name="hyena_filter_b",
    )(*spectra, fmat, norm)


def _hyena(x0c, vg, flt, hy_bias, g_hy, batch, seq):
    c = vg.shape[1]
    n2 = DFT_N2
    n1h = seq // n2
    n1 = 2 * n1h
    k1p = min(n1, -(-(n1 // 2 + 1) // SUBLANES) * SUBLANES)
    kb = _pick_tile(k1p, 8, 1)
    cos, sin, fmat_np = _dft_tables(n1, n2, k1p)
    a_half = jnp.asarray(np.concatenate([cos[:, :, :n1h], -sin[:, :, :n1h]], axis=1), BF16)
    mix_a = jnp.asarray(_mix_forward(cos, sin, n1h), BF16)
    mix_c = jnp.asarray(_mix_inverse(cos, sin, n1h, n1), BF16)
    fmat = jnp.asarray(fmat_np, BF16)
    fmat_t = jnp.asarray(fmat_np.T, BF16)

    *spectra, norm = _filter_a(a_half, flt, seq, c)
    kfr, kfi = _filter_b([s.reshape(k1p, n2, c) for s in spectra], fmat, norm, kb)

    vg4 = vg.reshape(batch, n1h, n2, c)
    x04 = x0c.reshape(batch, n1h, n2, c)
    yr, yi = _stage_a(mix_a, vg4, k1p)
    ur, ui = _stage_b(yr, yi, fmat, fmat_t, kfr, kfi, kb)
    y = _stage_c(mix_c, ur, ui, vg4, x04, hy_bias, g_hy)
    return y.reshape(batch * seq, c)


def _out_proj_kernel(x_ref, gate_ref, attn_ref, hy_ref, ga_ref, wa_ref, wh_ref, o_ref):
    ya = (_rms(attn_ref[...].astype(F32)) * ga_ref[...]).astype(BF16)
    y = jnp.dot(ya, wa_ref[...], preferred_element_type=F32)
    y += jnp.dot(hy_ref[...].astype(BF16), wh_ref[...], preferred_element_type=F32)
    o_ref[...] = x_ref[...] + gate_ref[0] * y


def _out_proj(x, mod3, row_of_tile, attn, hyn, g_attn, w_out, tm):
    t, d = x.shape
    aw = attn.shape[1]
    hw = hyn.shape[1]
    const = lambda shape, idx: pl.BlockSpec(shape, lambda i: idx, pipeline_mode=pl.Buffered(1))
    return pl.pallas_call(
        _out_proj_kernel,
        out_shape=jax.ShapeDtypeStruct((t, d), F32),
        grid=(t // tm,),
        in_specs=[pl.BlockSpec((tm, d), lambda i: (i, 0)),
                  pl.BlockSpec((1, 1, d), lambda i: (row_of_tile(i), 0, 5)),
                  pl.BlockSpec((tm, aw), lambda i: (i, 0)),
                  pl.BlockSpec((tm, hw), lambda i: (i, 0)),
                  const((1, aw), (0, 0)), const((aw, d), (0, 0)), const((hw, d), (1, 0))],
        out_specs=pl.BlockSpec((tm, d), lambda i: (i, 0)),
        compiler_params=_params("parallel"),
        name="out_proj",
    )(x, mod3, attn, hyn, g_attn.reshape(1, aw), w_out, w_out)


def _rope_tables(seq):
    half = HEAD_DIM // 2
    pos = np.arange(seq)
    inv = ROPE_THETA ** (-np.arange(0, half, 2, dtype=np.float64) / half)
    ang = np.concatenate([(pos // GRID_W)[:, None] * inv, (pos % GRID_W)[:, None] * inv], axis=-1)
    cos = np.repeat(np.cos(ang), 2, axis=-1)
    sin = np.stack([-np.sin(ang), np.sin(ang)], axis=-1).reshape(seq, HEAD_DIM)
    return jnp.asarray(cos, F32), jnp.asarray(sin, F32)


def kernel(x, c, ctx, c_ctx, w_ada, b_ada, g_norm, w_ffn1_up, w_ffn1_down, w_ffn2_up, w_ffn2_down,
           w_in, q_norm, k_norm, conv_w, conv_b, flt_w1, flt_b1, flt_w2, flt_b2, flt_w3, flt_b3,
           flt_w4, flt_freq, flt_decay, hy_bias, g_out, w_out):
    batch, seq, d = x.shape
    lc = ctx.shape[1]
    depth = w_ada.shape[0]
    assert depth == 1, "context-update path of deeper stacks is not implemented"
    attn_w = d // 2
    kvw = N_KV_HEADS * HEAD_DIM
    tm = _pick_tile(seq, 512, 128)
    tmf = _pick_tile(seq, FFN_ROWS, 128)
    tmc = _pick_tile(batch * lc, 512, 8)
    lat_row = lambda i: i // (seq // tm)
    lat_row_ffn = lambda i: i // (seq // tmf)
    ctx_row = lambda i: batch

    xs = x.reshape(batch * seq, d)
    cs = ctx.reshape(batch * lc, d)
    rows = -(-(batch + 1) // 8) * 8
    c_cols = jnp.zeros((d, rows), F32).at[:, :batch].set(c.T).at[:, batch].set(c_ctx)
    cos, sin = _rope_tables(seq)

    l = 0
    mod3 = _ada_mod(c_cols, batch + 1, w_ada[l], b_ada[l]).reshape(rows, 1, N_MOD * d)

    w1u, w1d = w_ffn1_up[l].astype(BF16), w_ffn1_down[l].astype(BF16)
    qkv_w = attn_w + 2 * kvw
    qn = q_norm[l].reshape(1, HEAD_DIM)
    kn = k_norm[l].reshape(1, HEAD_DIM)

    x1, (wi, wo) = _ffn(xs, mod3, lat_row_ffn, 0, g_norm[l, 0], w1u, w1d, tmf, to_bf16=(w_in[l], w_out[l]))
    c1, _ = _ffn(cs, mod3, ctx_row, 0, g_norm[l, 0], w1u, w1d, tmc)

    q, k, v, x0c, vg = _in_proj(x1, mod3, lat_row, g_norm[l, 1], wi, qn, kn,
                                cos, sin, conv_w[l], conv_b[l], seq, tm)
    kc, vc = _ctx_kv(c1, mod3, batch, g_norm[l, 1], wi[:, attn_w:qkv_w], kn, tmc)
    k_all = jnp.concatenate([kc.reshape(batch, lc, kvw), k.reshape(batch, seq, kvw)], axis=1)
    v_all = jnp.concatenate([vc.reshape(batch, lc, kvw), v.reshape(batch, seq, kvw)], axis=1)
    attn, (w2u, w2d) = _attention(q, k_all, v_all, batch, seq, to_bf16=(w_ffn2_up[l], w_ffn2_down[l]))

    flt = (flt_w1[l], flt_b1[l], flt_w2[l], flt_b2[l], flt_w3[l], flt_b3[l], flt_w4[l],
           flt_freq[l], flt_decay[l])
    hyn = _hyena(x0c, vg, flt, hy_bias[l], g_out[l, attn_w:], batch, seq)
    x2 = _out_proj(x1, mod3, lat_row, attn, hyn, g_out[l, :attn_w], wo, tm)

    x3, _ = _ffn(x2, mod3, lat_row_ffn, 2, g_norm[l, 2], w2u, w2d, tmf)
    return x3.reshape(batch, seq, d)
```

```python
import functools
import math

import jax
import jax.numpy as jnp
import numpy as np
from jax import lax
from jax.experimental import pallas as pl
from jax.experimental.pallas import tpu as pltpu

F32 = jnp.float32
BF16 = jnp.bfloat16

HEAD_DIM = 128
N_KV_HEADS = 2
GRID_W = 64
ROPE_THETA = 10000.0
FILTER_BANDS = 16
RMS_EPS = 1e-6
N_MOD = 9

V7X_VMEM_LIMIT_BYTES = 56 * 1024 * 1024
SUBLANES = 8
PACKED_ROWS = 2 * SUBLANES
FFN_ROWS = 1024
DFT_N2 = 128

_HI = lax.Precision.HIGHEST


def _params(*sem):
    return pltpu.CompilerParams(dimension_semantics=sem, vmem_limit_bytes=V7X_VMEM_LIMIT_BYTES)


def _pick_tile(n, cap, mult):
    best = None
    t = mult
    while t <= min(n, cap):
        if n % t == 0:
            best = t
        t += mult
    assert best is not None, (n, cap, mult)
    return best


def _rms(x, eps=RMS_EPS):
    return x * lax.rsqrt(jnp.mean(x * x, axis=-1, keepdims=True) + eps)


def _side_cast_specs(mats, n_steps, step_of):
    specs = []
    for w in mats:
        rb = PACKED_ROWS * -(-w.shape[0] // (PACKED_ROWS * n_steps))
        assert w.shape[0] % rb == 0, (w.shape, rb)
        nblk = w.shape[0] // rb
        specs.append(pl.BlockSpec((rb, w.shape[1]),
                                  lambda *idx, nblk=nblk: (jnp.minimum(step_of(*idx), nblk - 1), 0)))
    return specs


def _side_cast(side_in, side_out):
    for src, dst in zip(side_in, side_out):
        dst[...] = src[...].astype(BF16)


def _ada_kernel(ct_ref, w_ref, b_ref, o_ref, sb_ref, *, n_rows):
    lanes = sb_ref.shape[2]

    @pl.when(pl.program_id(0) == 0)
    def _():
        ct = ct_ref[...]
        s = ct * jax.nn.sigmoid(ct)
        for r in range(n_rows):
            sb_ref[r] = jnp.broadcast_to(s[:, r:r + 1], sb_ref.shape[1:])

    d, tn = w_ref.shape
    reps = tn // lanes
    acc = [jnp.zeros((SUBLANES, tn), F32) for _ in range(n_rows)]
    for kb in range(d // SUBLANES):
        rows = slice(kb * SUBLANES, (kb + 1) * SUBLANES)
        w = w_ref[rows, :]
        for r in range(n_rows):
            acc[r] = acc[r] + w * jnp.concatenate([sb_ref[r, rows, :]] * reps, axis=1)
    out = [jnp.sum(a, axis=0, keepdims=True) for a in acc]
    out.append(jnp.zeros((o_ref.shape[0] - n_rows, tn), F32))
    o_ref[...] = jnp.concatenate(out, axis=0) + b_ref[...]


def _ada_mod(c_cols, n_rows, w_ada, b_ada):
    d, rows = c_cols.shape
    n = w_ada.shape[1]
    tn = _pick_tile(n, 2048, 128)
    return pl.pallas_call(
        functools.partial(_ada_kernel, n_rows=n_rows),
        out_shape=jax.ShapeDtypeStruct((rows, n), F32),
        grid=(n // tn,),
        in_specs=[pl.BlockSpec((d, rows), lambda j: (0, 0)),
                  pl.BlockSpec((d, tn), lambda j: (0, j)),
                  pl.BlockSpec((1, tn), lambda j: (0, j))],
        out_specs=pl.BlockSpec((rows, tn), lambda j: (0, j)),
        scratch_shapes=[pltpu.VMEM((n_rows, d, HEAD_DIM), F32)],
        compiler_params=_params("arbitrary"),
        name="ada_mod",
    )(c_cols, w_ada, b_ada.reshape(1, n))


def _ffn_kernel(x_ref, shift_ref, scale_ref, gate_ref, g_ref, wg_ref, wu_ref, wd_ref, *refs, tn, n_side):
    side_in, o_ref, side_out, h_ref = refs[:n_side], refs[n_side], refs[n_side + 1:-1], refs[-1]
    _side_cast(side_in, side_out)
    k = pl.program_id(1)
    last = pl.num_programs(1) - 1
    d = o_ref.shape[1]

    def chunk(h, emit):
        g = jnp.dot(h, wg_ref[...], preferred_element_type=F32)
        u = jnp.dot(h, wu_ref[...], preferred_element_type=F32)
        a = (g * jax.nn.sigmoid(g) * u).astype(BF16)
        for n0 in range(0, d, tn):
            emit(n0, jnp.dot(a, wd_ref[:, n0:n0 + tn], preferred_element_type=F32))

    @pl.when(k == 0)
    def _():
        y = _rms(x_ref[...]) * g_ref[...]
        h = (y * (1.0 + scale_ref[0]) + shift_ref[0]).astype(BF16)
        h_ref[...] = h

        def emit(n0, part):
            o_ref[:, n0:n0 + tn] = part
        chunk(h, emit)

    @pl.when(jnp.logical_and(k > 0, k < last))
    def _():
        def emit(n0, part):
            o_ref[:, n0:n0 + tn] += part
        chunk(h_ref[...], emit)

    @pl.when(jnp.logical_and(k > 0, k == last))
    def _():
        def emit(n0, part):
            cols = slice(n0, n0 + tn)
            o_ref[:, cols] = x_ref[:, cols] + 0.5 * gate_ref[0][:, cols] * (o_ref[:, cols] + part)
        chunk(h_ref[...], emit)


def _ffn(x, mod3, row_of_tile, slot, g, w_up, w_down, tm, to_bf16=()):
    t, d = x.shape
    f = w_down.shape[0]
    tf = _pick_tile(f, 512, 128)
    nf = f // tf
    assert nf >= 2, "the kernel keeps separate first / last chunk paths"
    mod_spec = lambda c: pl.BlockSpec((1, 1, d), lambda i, k: (row_of_tile(i), 0, 3 * slot + c))
    row_tile = pl.BlockSpec((tm, d), lambda i, k: (i, 0))
    side_specs = _side_cast_specs(to_bf16, (t // tm) * nf, lambda i, k: i * nf + k)
    outs = pl.pallas_call(
        functools.partial(_ffn_kernel, tn=_pick_tile(d, 512, 128), n_side=len(to_bf16)),
        out_shape=(jax.ShapeDtypeStruct((t, d), F32), *[jax.ShapeDtypeStruct(w.shape, BF16) for w in to_bf16]),
        grid=(t // tm, nf),
        in_specs=[row_tile, mod_spec(0), mod_spec(1), mod_spec(2),
                  pl.BlockSpec((1, d), lambda i, k: (0, 0)),
                  pl.BlockSpec((d, tf), lambda i, k: (0, k)),
                  pl.BlockSpec((d, tf), lambda i, k: (0, k + nf)),
                  pl.BlockSpec((tf, d), lambda i, k: (k, 0)),
                  *side_specs],
        out_specs=(row_tile, *side_specs),
        scratch_shapes=[pltpu.VMEM((tm, d), BF16)],
        compiler_params=_params("arbitrary", "arbitrary"),
        name="ffn",
    )(x, mod3, mod3, mod3, g.reshape(1, d), w_up, w_up, w_down, *to_bf16)
    return outs[0], outs[1:]


def _rope(x, cos, sin):
    lane = lax.broadcasted_iota(jnp.int32, x.shape, 1)
    partner = jnp.where(lane % 2 == 0, pltpu.roll(x, HEAD_DIM - 1, axis=1), pltpu.roll(x, 1, axis=1))
    return x * cos + partner * sin


HALO = 16


def _in_proj_kernel(x_ref, xp_ref, xn_ref, shift_ref, scale_ref, g_ref, w_ref, qn_ref, kn_ref,
                    cos_ref, sin_ref, cw_ref, cb_ref, q_ref, k_ref, v_ref, x0c_ref, vg_ref, *,
                    q_scale, tiles_per_seq, cw):
    i = pl.program_id(0)
    tm = x_ref.shape[0]
    attn_w = q_ref.shape[1]
    kvw = k_ref.shape[1]
    c = x0c_ref.shape[1]
    first = (i % tiles_per_seq) == 0
    last = (i % tiles_per_seq) == tiles_per_seq - 1

    x_ext = jnp.concatenate([xp_ref[...], x_ref[...], xn_ref[...]], axis=0)
    y = _rms(x_ext) * g_ref[...]
    h_ext = (y * (1.0 + scale_ref[0]) + shift_ref[0]).astype(BF16)
    h = h_ext[HALO:HALO + tm]

    qkv_w = attn_w + 2 * kvw
    u = jnp.dot(h, w_ref[:, :qkv_w], preferred_element_type=F32)

    def normed(col, gain):
        return _rope(_rms(u[:, col:col + HEAD_DIM]) * gain, cos_ref[...], sin_ref[...])

    for hd in range(attn_w // HEAD_DIM):
        col = hd * HEAD_DIM
        q_ref[:, col:col + HEAD_DIM] = (normed(col, qn_ref[...]) * q_scale).astype(BF16)
    for hd in range(kvw // HEAD_DIM):
        col = hd * HEAD_DIM
        k_ref[:, col:col + HEAD_DIM] = normed(attn_w + col, kn_ref[...]).astype(BF16)
    v_ref[...] = u[:, attn_w + kvw:].astype(BF16)

    row = lax.broadcasted_iota(jnp.int32, (tm, cw), 0)
    at_start = jnp.logical_and(first, row == 0)
    at_end = jnp.logical_and(last, row == tm - 1)
    ext = tm + 2 * HALO

    def conv(group, c0):
        col = qkv_w + group * c + c0
        ue = jnp.dot(h_ext, w_ref[:, col:col + cw], preferred_element_type=F32)
        um = jnp.where(at_start, 0.0, pltpu.roll(ue, 1, axis=0)[HALO:HALO + tm])
        up = jnp.where(at_end, 0.0, pltpu.roll(ue, ext - 1, axis=0)[HALO:HALO + tm])
        w = cw_ref[group][:, c0:c0 + cw]
        return um * w[0:1] + ue[HALO:HALO + tm] * w[1:2] + up * w[2:3] + cb_ref[group][:, c0:c0 + cw]

    for c0 in range(0, c, cw):
        x0c_ref[:, c0:c0 + cw] = conv(0, c0).astype(BF16)
        vg_ref[:, c0:c0 + cw] = (conv(2, c0) * conv(1, c0)).astype(BF16)


def _in_proj(x, mod3, row_of_tile, g, w_in, qn, kn, cos, sin, conv_w, conv_b, seq, tm):
    t, d = x.shape
    kvw = N_KV_HEADS * HEAD_DIM
    attn_w = d // 2
    c = (w_in.shape[1] - attn_w - 2 * kvw) // 3
    cw = _pick_tile(c, 512, 128)
    pos_tiles = seq // tm
    hb = tm // HALO
    nhb = t // HALO
    w3 = conv_w.reshape(3, 3, c).transpose(1, 0, 2)
    b3 = conv_b.reshape(3, 1, c)
    mod_spec = lambda ch: pl.BlockSpec((1, 1, d), lambda i: (row_of_tile(i), 0, 3 + ch))
    const = lambda shape: pl.BlockSpec(shape, lambda i: (0,) * len(shape), pipeline_mode=pl.Buffered(1))
    kern = functools.partial(_in_proj_kernel, q_scale=HEAD_DIM ** -0.5 * math.log2(math.e),
                             tiles_per_seq=pos_tiles, cw=cw)
    return pl.pallas_call(
        kern,
        out_shape=(jax.ShapeDtypeStruct((t, attn_w), BF16),
                   jax.ShapeDtypeStruct((t, kvw), BF16),
                   jax.ShapeDtypeStruct((t, kvw), BF16),
                   jax.ShapeDtypeStruct((t, c), BF16),
                   jax.ShapeDtypeStruct((t, c), BF16)),
        grid=(t // tm,),
        in_specs=[pl.BlockSpec((tm, d), lambda i: (i, 0)),
                  pl.BlockSpec((HALO, d), lambda i: (jnp.maximum(i * hb - 1, 0), 0)),
                  pl.BlockSpec((HALO, d), lambda i: (jnp.minimum((i + 1) * hb, nhb - 1), 0)),
                  mod_spec(0), mod_spec(1),
                  const((1, d)), const(w_in.shape),
                  const((1, HEAD_DIM)), const((1, HEAD_DIM)),
                  pl.BlockSpec((tm, HEAD_DIM), lambda i: (i % pos_tiles, 0)),
                  pl.BlockSpec((tm, HEAD_DIM), lambda i: (i % pos_tiles, 0)),
                  const((3, 3, c)), const((3, 1, c))],
        out_specs=(pl.BlockSpec((tm, attn_w), lambda i: (i, 0)),
                   pl.BlockSpec((tm, kvw), lambda i: (i, 0)),
                   pl.BlockSpec((tm, kvw), lambda i: (i, 0)),
                   pl.BlockSpec((tm, c), lambda i: (i, 0)),
                   pl.BlockSpec((tm, c), lambda i: (i, 0))),
        compiler_params=_params("parallel"),
        name="in_proj",
    )(x, x, x, mod3, mod3, g.reshape(1, d), w_in, qn, kn, cos, sin, w3, b3)


def _ctx_kv_kernel(x_ref, shift_ref, scale_ref, g_ref, w_ref, kn_ref, k_ref, v_ref):
    y = _rms(x_ref[...]) * g_ref[...]
    h = (y * (1.0 + scale_ref[0]) + shift_ref[0]).astype(BF16)
    u = jnp.dot(h, w_ref[...], preferred_element_type=F32)
    for hd in range(N_KV_HEADS):
        k_ref[:, hd * HEAD_DIM:(hd + 1) * HEAD_DIM] = (
            _rms(u[:, hd * HEAD_DIM:(hd + 1) * HEAD_DIM]) * kn_ref[...]).astype(BF16)
    v_ref[...] = u[:, N_KV_HEADS * HEAD_DIM:].astype(BF16)


def _ctx_kv(x, mod3, ctx_row, g, w_kv, kn, tm):
    t, d = x.shape
    kvw = N_KV_HEADS * HEAD_DIM
    mod_spec = lambda c: pl.BlockSpec((1, 1, d), lambda i: (ctx_row, 0, 3 + c))
    return pl.pallas_call(
        _ctx_kv_kernel,
        out_shape=(jax.ShapeDtypeStruct((t, kvw), BF16), jax.ShapeDtypeStruct((t, kvw), BF16)),
        grid=(t // tm,),
        in_specs=[pl.BlockSpec((tm, d), lambda i: (i, 0)),
                  mod_spec(0), mod_spec(1),
                  pl.BlockSpec((1, d), lambda i: (0, 0)),
                  pl.BlockSpec((d, 2 * kvw), lambda i: (0, 0)),
                  pl.BlockSpec((1, HEAD_DIM), lambda i: (0, 0))],
        out_specs=(pl.BlockSpec((tm, kvw), lambda i: (i, 0)),
                   pl.BlockSpec((tm, kvw), lambda i: (i, 0))),
        compiler_params=_params("parallel"),
        name="ctx_kv",
    )(x, mod3, mod3, g.reshape(1, d), w_kv, kn)


def _attn_kernel(q_ref, k_ref, v_ref, *refs, group, nq, n_side):
    side_in, o_ref, side_out = refs[:n_side], refs[n_side], refs[n_side + 1:2 * n_side + 1]
    s0_ref, s1_ref, mx0_ref, mx1_ref, ls_ref, acc_ref = refs[2 * n_side + 1:]
    _side_cast(side_in, side_out)
    i = pl.program_id(2)
    tq = q_ref.shape[0]
    nk, _, tk = s0_ref.shape
    lanes = HEAD_DIM
    bufs = ((s0_ref, mx0_ref), (s1_ref, mx1_ref))

    def scores(j, q, s_ref, mx_ref):
        start = pl.multiple_of(j * tk, tk)
        s = lax.dot_general(q, k_ref[0, pl.ds(start, tk), :], (((1,), (1,)), ((), ())),
                            preferred_element_type=F32)
        s_ref[j] = s
        mx = mx_ref[...]
        for c in range(tk // lanes):
            mx = jnp.maximum(mx, s[:, c * lanes:(c + 1) * lanes])
        mx_ref[...] = mx

    def values(j, s_ref, mx_ref):
        start = pl.multiple_of(j * tk, tk)
        s = s_ref[j]
        m = mx_ref[...]
        ls = ls_ref[...]
        parts = []
        for c in range(tk // lanes):
            pc = jnp.exp2(s[:, c * lanes:(c + 1) * lanes] - m)
            ls = ls + pc
            parts.append(pc.astype(BF16))
        ls_ref[...] = ls
        p = jnp.concatenate(parts, axis=1)
        acc_ref[...] += jnp.dot(p, v_ref[0, pl.ds(start, tk), :], preferred_element_type=F32)

    def begin_scores(mx_ref):
        mx_ref[...] = jnp.full_like(mx_ref, -jnp.inf)
        return jnp.concatenate([q_ref[:, h * HEAD_DIM:(h + 1) * HEAD_DIM] for h in range(group)], axis=0)

    def end_scores(mx_ref):
        mx_ref[...] = jnp.broadcast_to(jnp.max(mx_ref[...], axis=-1, keepdims=True), mx_ref.shape)

    def begin_values():
        ls_ref[...] = jnp.zeros_like(ls_ref)
        acc_ref[...] = jnp.zeros_like(acc_ref)

    def end_values():
        o = acc_ref[...] / jnp.sum(ls_ref[...], axis=-1, keepdims=True)
        for h in range(group):
            o_ref[:, h * HEAD_DIM:(h + 1) * HEAD_DIM] = o[h * tq:(h + 1) * tq].astype(BF16)

    @pl.when(i == 0)
    def _():
        s_ref, mx_ref = bufs[0]
        q = begin_scores(mx_ref)
        lax.fori_loop(0, nk, lambda j, c: (scores(j, q, s_ref, mx_ref), c)[1], 0)
        end_scores(mx_ref)

    for parity in range(2):
        @pl.when(jnp.logical_and(jnp.logical_and(i > 0, i < nq), i % 2 == parity))
        def _():
            (cs_ref, cmx_ref), (ps_ref, pmx_ref) = bufs[parity], bufs[1 - parity]
            q = begin_scores(cmx_ref)
            begin_values()

            def both(j, c):
                scores(j, q, cs_ref, cmx_ref)
                values(j, ps_ref, pmx_ref)
                return c

            lax.fori_loop(0, nk, both, 0, unroll=True)
            end_scores(cmx_ref)
            end_values()

    @pl.when(i == nq)
    def _():
        s_ref, mx_ref = bufs[(nq - 1) % 2]
        begin_values()
        lax.fori_loop(0, nk, lambda j, c: (values(j, s_ref, mx_ref), c)[1], 0)
        end_values()


def _attention(q, k_all, v_all, batch, seq, to_bf16=()):
    t, attn_w = q.shape
    lk = k_all.shape[1]
    group = attn_w // HEAD_DIM // N_KV_HEADS
    gw = group * HEAD_DIM
    tq = _pick_tile(seq, 128, 8)
    tk = _pick_tile(lk, 1408, 128)
    nq = seq // tq
    rows = group * tq
    side_specs = _side_cast_specs(to_bf16, batch * N_KV_HEADS * (nq + 1),
                                  lambda b, g, i: (b * N_KV_HEADS + g) * (nq + 1) + i)
    kern = functools.partial(_attn_kernel, group=group, nq=nq, n_side=len(to_bf16))
    outs = pl.pallas_call(
        kern,
        out_shape=(jax.ShapeDtypeStruct((t, attn_w), BF16),
                   *[jax.ShapeDtypeStruct(w.shape, BF16) for w in to_bf16]),
        grid=(batch, N_KV_HEADS, nq + 1),
        in_specs=[pl.BlockSpec((tq, gw), lambda b, g, i: (b * nq + jnp.minimum(i, nq - 1), g)),
                  pl.BlockSpec((1, lk, HEAD_DIM), lambda b, g, i: (b, 0, g)),
                  pl.BlockSpec((1, lk, HEAD_DIM), lambda b, g, i: (b, 0, g)),
                  *side_specs],
        out_specs=(pl.BlockSpec((tq, gw), lambda b, g, i: (b * nq + jnp.maximum(i - 1, 0), g)),
                   *side_specs),
        scratch_shapes=[pltpu.VMEM((lk // tk, rows, tk), F32), pltpu.VMEM((lk // tk, rows, tk), F32),
                        pltpu.VMEM((rows, HEAD_DIM), F32), pltpu.VMEM((rows, HEAD_DIM), F32),
                        pltpu.VMEM((rows, HEAD_DIM), F32), pltpu.VMEM((rows, HEAD_DIM), F32)],
        compiler_params=_params("arbitrary", "arbitrary", "arbitrary"),
        name="attention",
    )(q, k_all, v_all, *to_bf16)
    return outs[0], outs[1:]


def _dft_tables(n1, n2, k1p):
    n = n1 * n2
    k1 = np.arange(k1p)[:, None]
    a = np.arange(n1)[None, :]
    m2 = np.arange(n2)[:, None, None]
    theta = 2.0 * np.pi * ((((n2 * a * k1)[None] + m2 * k1[None]) % n) / n)
    k2 = np.arange(n2)[:, None]
    b = np.arange(n2)[None, :]
    phi = 2.0 * np.pi * ((k2 * b) % n2) / n2
    c, s = np.cos(phi), np.sin(phi)
    fmat = np.block([[c, s], [-s, c]])
    return np.cos(theta), np.sin(theta), fmat


def _mix_forward(cos, sin, n1h):
    n2, k1p, _ = cos.shape
    g = n2 // SUBLANES
    a = np.stack([cos[:, :, :n1h], -sin[:, :, :n1h]], axis=1).reshape(g, SUBLANES, 2, k1p, n1h)
    m = np.zeros((g, 2, k1p, SUBLANES, n1h, SUBLANES))
    for j in range(SUBLANES):
        m[:, :, :, j, :, j] = a[:, j]
    return m.reshape(g, 2 * k1p * SUBLANES, n1h * SUBLANES)


def _mix_inverse(cos, sin, n1h, n1):
    n2, k1p, _ = cos.shape
    g = n2 // SUBLANES
    k1 = np.arange(k1p)
    weight = np.where((k1 == 0) | (k1 == n1 // 2), 1.0, np.where(k1 < n1 // 2, 2.0, 0.0)) / (n1 * n2)
    a = np.stack([cos[:, :, :n1h], -sin[:, :, :n1h]], axis=1) * weight[None, None, :, None]
    a = a.reshape(g, SUBLANES, 2, k1p, n1h)
    m = np.zeros((g, n1h, SUBLANES, 2, k1p, SUBLANES))
    for j in range(SUBLANES):
        m[:, :, j, :, :, j] = a[:, j].transpose(0, 3, 1, 2)
    return m.reshape(g, n1h * SUBLANES, 2 * k1p * SUBLANES)


def _stage_a_kernel(m_ref, x_ref, yr_ref, yi_ref):
    _, n1h, _, c = x_ref.shape
    k1p = yr_ref.shape[1]
    re, im = [], []
    xs = x_ref[0].astype(F32)
    for half in range(2):
        rows = slice(half * SUBLANES, (half + 1) * SUBLANES)
        x = xs[:, rows, :].reshape(n1h * SUBLANES, c).astype(BF16)
        y = jnp.dot(m_ref[half], x, preferred_element_type=F32)
        re.append(y[:k1p * SUBLANES].reshape(k1p, SUBLANES, c))
        im.append(y[k1p * SUBLANES:].reshape(k1p, SUBLANES, c))
    yr_ref[0] = jnp.concatenate(re, axis=1).astype(BF16)
    yi_ref[0] = jnp.concatenate(im, axis=1).astype(BF16)


def _stage_a(mix, vg4, k1p):
    b, n1h, n2, c = vg4.shape
    return pl.pallas_call(
        _stage_a_kernel,
        out_shape=(jax.ShapeDtypeStruct((b, k1p, n2, c), BF16),) * 2,
        grid=(n2 // PACKED_ROWS, b),
        in_specs=[pl.BlockSpec((2,) + mix.shape[1:], lambda g, bi: (g, 0, 0)),
                  pl.BlockSpec((1, n1h, PACKED_ROWS, c), lambda g, bi: (bi, 0, g, 0))],
        out_specs=(pl.BlockSpec((1, k1p, PACKED_ROWS, c), lambda g, bi: (bi, 0, g, 0)),) * 2,
        compiler_params=_params("parallel", "parallel"),
        name="hyena_dft_a",
    )(mix, vg4)


def _stage_b_kernel(yr_ref, yi_ref, f_ref, ft_ref, kr_ref, ki_ref, ur_ref, ui_ref):
    kb, n2 = yr_ref.shape[1:3]
    for r in range(kb):
        y = jnp.concatenate([yr_ref[0, r], yi_ref[0, r]], axis=0)
        z = jnp.dot(f_ref[...], y, preferred_element_type=F32)
        zr, zi = z[:n2], z[n2:]
        kr = kr_ref[r].astype(F32)
        ki = ki_ref[r].astype(F32)
        p = jnp.concatenate([zr * kr - zi * ki, zr * ki + zi * kr], axis=0).astype(BF16)
        u = jnp.dot(ft_ref[...], p, preferred_element_type=F32)
        ur_ref[0, r] = u[:n2].astype(BF16)
        ui_ref[0, r] = u[n2:].astype(BF16)


def _stage_b(yr, yi, fmat, fmat_t, kfr, kfi, kb):
    b, k1p, n2, c = yr.shape
    blk = pl.BlockSpec((1, kb, n2, c), lambda k, bi: (bi, k, 0, 0))
    flt = pl.BlockSpec((kb, n2, c), lambda k, bi: (k, 0, 0))
    mat = pl.BlockSpec((2 * n2, 2 * n2), lambda k, bi: (0, 0))
    return pl.pallas_call(
        _stage_b_kernel,
        out_shape=(jax.ShapeDtypeStruct((b, k1p, n2, c), BF16),) * 2,
        grid=(k1p // kb, b),
        in_specs=[blk, blk, mat, mat, flt, flt],
        out_specs=(blk, blk),
        compiler_params=_params("parallel", "parallel"),
        name="hyena_dft_b",
    )(yr, yi, fmat, fmat_t, kfr, kfi)


def _stage_c_kernel(m_ref, ur_ref, ui_ref, vg_ref, x0_ref, hb_ref, g_ref, o_ref):
    _, k1p, _, c = ur_ref.shape
    n1h = vg_ref.shape[1]
    ur = ur_ref[0].astype(F32)
    ui = ui_ref[0].astype(F32)
    vgs = vg_ref[0].astype(F32)
    x0s = x0_ref[0].astype(F32)
    for half in range(2):
        rows = slice(half * SUBLANES, (half + 1) * SUBLANES)
        u = jnp.concatenate([ur[:, rows, :].reshape(k1p * SUBLANES, c),
                             ui[:, rows, :].reshape(k1p * SUBLANES, c)], axis=0).astype(BF16)
        conv = jnp.dot(m_ref[half], u, preferred_element_type=F32)
        vg = vgs[:, rows, :].reshape(n1h * SUBLANES, c)
        x0 = x0s[:, rows, :].reshape(n1h * SUBLANES, c)
        hyo = (conv + hb_ref[...] * vg) * x0
        o_ref[0, :, rows, :] = (_rms(hyo) * g_ref[...]).reshape(n1h, SUBLANES, c)


def _stage_c(mix, ur, ui, vg4, x04, hy_bias, g_hy):
    b, n1h, n2, c = vg4.shape
    k1p = ur.shape[1]
    ublk = pl.BlockSpec((1, k1p, PACKED_ROWS, c), lambda g, bi: (bi, 0, g, 0))
    xblk = pl.BlockSpec((1, n1h, PACKED_ROWS, c), lambda g, bi: (bi, 0, g, 0))
    vec = pl.BlockSpec((1, c), lambda g, bi: (0, 0))
    return pl.pallas_call(
        _stage_c_kernel,
        out_shape=jax.ShapeDtypeStruct((b, n1h, n2, c), F32),
        grid=(n2 // PACKED_ROWS, b),
        in_specs=[pl.BlockSpec((2,) + mix.shape[1:], lambda g, bi: (g, 0, 0)),
                  ublk, ublk, xblk, xblk, vec, vec],
        out_specs=xblk,
        compiler_params=_params("parallel", "parallel"),
        name="hyena_dft_c",
    )(mix, ur, ui, vg4, x04, hy_bias.reshape(1, c), g_hy.reshape(1, c))


def _filter_a_kernel(a_ref, w1t_ref, w1c_ref, w1s_ref, b1_ref, w2_ref, b2_ref, w3_ref, b3_ref,
                     w4_ref, fr_ref, dec_ref, pr_ref, pi_ref, mr_ref, mi_ref, norm_ref, *, seq):
    g = pl.program_id(0)
    k1p = pr_ref.shape[0]
    c = norm_ref.shape[1]
    n1h = a_ref.shape[2]
    n2 = DFT_N2

    def positions(idx, axis):
        return jnp.concatenate([idx * n2 + (g * SUBLANES + j) for j in range(SUBLANES)], axis=axis)

    pos_l = positions(lax.broadcasted_iota(jnp.int32, (1, n1h), 1), 1).astype(F32)
    ipos_r = positions(lax.broadcasted_iota(jnp.int32, (n1h, 1), 0), 0)
    pos_r = ipos_r.astype(F32)

    band = lax.broadcasted_iota(jnp.int32, (FILTER_BANDS, 1), 0).astype(F32)
    f = 1e-4 + band * ((FILTER_BANDS - 1 - 1e-4) / (FILTER_BANDS - 1))
    ang = f * ((2.0 * math.pi * pos_l) / seq)
    fr = fr_ref[...]
    dot = functools.partial(jnp.dot, preferred_element_type=F32, precision=_HI)
    h = w1t_ref[...] * (pos_l / (seq - 1.0)) + dot(w1c_ref[...], jnp.cos(ang)) + dot(w1s_ref[...], -jnp.sin(ang))
    h = jnp.sin(fr * (h + b1_ref[...]))
    h = jnp.sin(fr * (dot(w2_ref[...], h) + b2_ref[...]))
    h = jnp.sin(fr * (dot(w3_ref[...], h) + b3_ref[...]))
    h = lax.dot_general(h, w4_ref[...], (((0,), (0,)), ((), ())), preferred_element_type=F32)
    t_r = pos_r / (seq - 1.0)
    dec = jnp.abs(dec_ref[...])
    kf = h[:, :c] * jnp.exp(-t_r * dec[0:1, :])
    kb = jnp.where(ipos_r == 0, 0.0, h[:, c:] * jnp.exp(-t_r * dec[1:2, :]))

    @pl.when(g == 0)
    def _():
        norm_ref[...] = jnp.zeros_like(norm_ref)

    norm_ref[...] += jnp.sum(jnp.abs(kf) + jnp.abs(kb), axis=0, keepdims=True)
    ks = (kf + kb).astype(BF16)
    kd = (kf - kb).astype(BF16)
    for j in range(SUBLANES):
        rows = slice(j * n1h, (j + 1) * n1h)
        cols = slice(j * c, (j + 1) * c)
        y = jnp.dot(a_ref[j], ks[rows], preferred_element_type=F32)
        pr_ref[:, cols] = y[:k1p].astype(BF16)
        pi_ref[:, cols] = y[k1p:].astype(BF16)
        y = jnp.dot(a_ref[j], kd[rows], preferred_element_type=F32)
        mr_ref[:, cols] = y[:k1p].astype(BF16)
        mi_ref[:, cols] = y[k1p:].astype(BF16)


def _filter_a(a_half, flt, seq, c):
    w1, b1, w2, b2, w3, b3, w4, freq, decay = flt
    hid = w2.shape[0]
    n2, k1p2, n1h = a_half.shape
    k1p = k1p2 // 2
    full = lambda shape: pl.BlockSpec(shape, lambda g: (0,) * len(shape))
    spectrum = pl.BlockSpec((k1p, SUBLANES * c), lambda g: (0, g))
    kern = functools.partial(_filter_a_kernel, seq=seq)
    return pl.pallas_call(
        kern,
        out_shape=(*[jax.ShapeDtypeStruct((k1p, n2 * c), BF16)] * 4, jax.ShapeDtypeStruct((1, c), F32)),
        grid=(n2 // SUBLANES,),
        in_specs=[pl.BlockSpec((SUBLANES, k1p2, n1h), lambda g: (g, 0, 0)),
                  full((hid, 1)), full((hid, FILTER_BANDS)), full((hid, FILTER_BANDS)), full((hid, 1)),
                  full((hid, hid)), full((hid, 1)), full((hid, hid)), full((hid, 1)),
                  full((hid, 2 * c)), full((hid, 1)), full((2, c))],
        out_specs=(spectrum, spectrum, spectrum, spectrum, pl.BlockSpec((1, c), lambda g: (0, 0))),
        compiler_params=_params("arbitrary"),
        name="hyena_filter_a",
    )(a_half, w1[0:1].T, w1[1:1 + FILTER_BANDS].T, w1[1 + FILTER_BANDS:].T, b1.reshape(hid, 1),
      w2.T, b2.reshape(hid, 1), w3.T, b3.reshape(hid, 1), w4, freq.reshape(hid, 1), decay)


def _filter_b_kernel(pr_ref, pi_ref, mr_ref, mi_ref, f_ref, norm_ref, kr_ref, ki_ref):
    kb, n2 = pr_ref.shape[:2]
    for r in range(kb):
        p = jnp.concatenate([pr_ref[r], pi_ref[r]], axis=0)
        m = jnp.concatenate([mr_ref[r], mi_ref[r]], axis=0)
        kr_ref[r] = (jnp.dot(f_ref[:n2], p, preferred_element_type=F32) / norm_ref[...]).astype(BF16)
        ki_ref[r] = (jnp.dot(f_ref[n2:], m, preferred_element_type=F32) / norm_ref[...]).astype(BF16)


def _filter_b(spectra, fmat, norm, kb):
    k1p, n2, c = spectra[0].shape
    blk = pl.BlockSpec((kb, n2, c), lambda k: (k, 0, 0))
    return pl.pallas_call(
        _filter_b_kernel,
        out_shape=(jax.ShapeDtypeStruct((k1p, n2, c), BF16),) * 2,
        grid=(k1p // kb,),
        in_specs=[blk, blk, blk, blk,
                  pl.BlockSpec((2 * n2, 2 * n2), lambda k: (0, 0)),
                  pl.BlockSpec((1, c), lambda k: (0, 0))],
        out_specs=(blk, blk),
        compiler_params=_params("parallel"),
        name="hyena_filter_b",
    )(*spectra, fmat, norm)


def _hyena(x0c, vg, flt, hy_bias, g_hy, batch, seq):
    c = vg.shape[1]
    n2 = DFT_N2
    n1h = seq // n2
    n1 = 2 * n1h
    k1p = min(n1, -(-(n1 // 2 + 1) // SUBLANES) * SUBLANES)
    kb = _pick_tile(k1p, 8, 1)
    cos, sin, fmat_np = _dft_tables(n1, n2, k1p)
    a_half = jnp.asarray(np.concatenate([cos[:, :, :n1h], -sin[:, :, :n1h]], axis=1), BF16)
    mix_a = jnp.asarray(_mix_forward(cos, sin, n1h), BF16)
    mix_c = jnp.asarray(_mix_inverse(cos, sin, n1h, n1), BF16)
    fmat = jnp.asarray(fmat_np, BF16)
    fmat_t = jnp.asarray(fmat_np.T, BF16)

    *spectra, norm = _filter_a(a_half, flt, seq, c)
    kfr, kfi = _filter_b([s.reshape(k1p, n2, c) for s in spectra], fmat, norm, kb)

    vg4 = vg.reshape(batch, n1h, n2, c)
    x04 = x0c.reshape(batch, n1h, n2, c)
    yr, yi = _stage_a(mix_a, vg4, k1p)
    ur, ui = _stage_b(yr, yi, fmat, fmat_t, kfr, kfi, kb)
    y = _stage_c(mix_c, ur, ui, vg4, x04, hy_bias, g_hy)
    return y.reshape(batch * seq, c)


def _out_proj_kernel(x_ref, gate_ref, attn_ref, hy_ref, ga_ref, wa_ref, wh_ref, o_ref):
    ya = (_rms(attn_ref[...].astype(F32)) * ga_ref[...]).astype(BF16)
    y = jnp.dot(ya, wa_ref[...], preferred_element_type=F32)
    y += jnp.dot(hy_ref[...].astype(BF16), wh_ref[...], preferred_element_type=F32)
    o_ref[...] = x_ref[...] + gate_ref[0] * y


def _out_proj(x, mod3, row_of_tile, attn, hyn, g_attn, w_out, tm):
    t, d = x.shape
    aw = attn.shape[1]
    hw = hyn.shape[1]
    const = lambda shape, idx: pl.BlockSpec(shape, lambda i: idx, pipeline_mode=pl.Buffered(1))
    return pl.pallas_call(
        _out_proj_kernel,
        out_shape=jax.ShapeDtypeStruct((t, d), F32),
        grid=(t // tm,),
        in_specs=[pl.BlockSpec((tm, d), lambda i: (i, 0)),
                  pl.BlockSpec((1, 1, d), lambda i: (row_of_tile(i), 0, 5)),
                  pl.BlockSpec((tm, aw), lambda i: (i, 0)),
                  pl.BlockSpec((tm, hw), lambda i: (i, 0)),
                  const((1, aw), (0, 0)), const((aw, d), (0, 0)), const((hw, d), (1, 0))],
        out_specs=pl.BlockSpec((tm, d), lambda i: (i, 0)),
        compiler_params=_params("parallel"),
        name="out_proj",
    )(x, mod3, attn, hyn, g_attn.reshape(1, aw), w_out, w_out)


def _rope_tables(seq):
    half = HEAD_DIM // 2
    pos = np.arange(seq)
    inv = ROPE_THETA ** (-np.arange(0, half, 2, dtype=np.float64) / half)
    ang = np.concatenate([(pos // GRID_W)[:, None] * inv, (pos % GRID_W)[:, None] * inv], axis=-1)
    cos = np.repeat(np.cos(ang), 2, axis=-1)
    sin = np.stack([-np.sin(ang), np.sin(ang)], axis=-1).reshape(seq, HEAD_DIM)
    return jnp.asarray(cos, F32), jnp.asarray(sin, F32)


def kernel(x, c, ctx, c_ctx, w_ada, b_ada, g_norm, w_ffn1_up, w_ffn1_down, w_ffn2_up, w_ffn2_down,
           w_in, q_norm, k_norm, conv_w, conv_b, flt_w1, flt_b1, flt_w2, flt_b2, flt_w3, flt_b3,
           flt_w4, flt_freq, flt_decay, hy_bias, g_out, w_out):
    batch, seq, d = x.shape
    lc = ctx.shape[1]
    depth = w_ada.shape[0]
    assert depth == 1, "context-update path of deeper stacks is not implemented"
    attn_w = d // 2
    kvw = N_KV_HEADS * HEAD_DIM
    tm = _pick_tile(seq, 512, 128)
    tmf = _pick_tile(seq, FFN_ROWS, 128)
    tmc = _pick_tile(batch * lc, 512, 8)
    lat_row = lambda i: i // (seq // tm)
    lat_row_ffn = lambda i: i // (seq // tmf)
    ctx_row = lambda i: batch

    xs = x.reshape(batch * seq, d)
    cs = ctx.reshape(batch * lc, d)
    rows = -(-(batch + 1) // 8) * 8
    c_cols = jnp.zeros((d, rows), F32).at[:, :batch].set(c.T).at[:, batch].set(c_ctx)
    cos, sin = _rope_tables(seq)

    l = 0
    mod3 = _ada_mod(c_cols, batch + 1, w_ada[l], b_ada[l]).reshape(rows, 1, N_MOD * d)

    w1u, w1d = w_ffn1_up[l].astype(BF16), w_ffn1_down[l].astype(BF16)
    qkv_w = attn_w + 2 * kvw
    qn = q_norm[l].reshape(1, HEAD_DIM)
    kn = k_norm[l].reshape(1, HEAD_DIM)

    x1, (wi, wo) = _ffn(xs, mod3, lat_row_ffn, 0, g_norm[l, 0], w1u, w1d, tmf, to_bf16=(w_in[l], w_out[l]))
    c1, _ = _ffn(cs, mod3, ctx_row, 0, g_norm[l, 0], w1u, w1d, tmc)

    q, k, v, x0c, vg = _in_proj(x1, mod3, lat_row, g_norm[l, 1], wi, qn, kn,
                                cos, sin, conv_w[l], conv_b[l], seq, tm)
    kc, vc = _ctx_kv(c1, mod3, batch, g_norm[l, 1], wi[:, attn_w:qkv_w], kn, tmc)
    k_all = jnp.concatenate([kc.reshape(batch, lc, kvw), k.reshape(batch, seq, kvw)], axis=1)
    v_all = jnp.concatenate([vc.reshape(batch, lc, kvw), v.reshape(batch, seq, kvw)], axis=1)
    attn, (w2u, w2d) = _attention(q, k_all, v_all, batch, seq, to_bf16=(w_ffn2_up[l], w_ffn2_down[l]))

    flt = (flt_w1[l], flt_b1[l], flt_w2[l], flt_b2[l], flt_w3[l], flt_b3[l], flt_w4[l],
           flt_freq[l], flt_decay[l])
    hyn = _hyena(x0c, vg, flt, hy_bias[l], g_out[l, attn_w:], batch, seq)
    x2 = _out_proj(x1, mod3, lat_row, attn, hyn, g_out[l, :attn_w], wo, tm)

    x3, _ = _ffn(x2, mod3, lat_row_ffn, 2, g_norm[l, 2], w2u, w2d, tmf)
    return x3.reshape(batch, seq, d)
```

```python
import functools
import math

import jax
import jax.numpy as jnp
import numpy as np
from jax import lax
from jax.experimental import pallas as pl
from jax.experimental.pallas import tpu as pltpu

F32 = jnp.float32
BF16 = jnp.bfloat16

HEAD_DIM = 128
N_KV_HEADS = 2
GRID_W = 64
ROPE_THETA = 10000.0
FILTER_BANDS = 16
RMS_EPS = 1e-6
N_MOD = 9

V7X_VMEM_LIMIT_BYTES = 56 * 1024 * 1024
SUBLANES = 8
PACKED_ROWS = 2 * SUBLANES
FFN_ROWS = 1024
DFT_N2 = 128

_HI = lax.Precision.HIGHEST


def _params(*sem):
    return pltpu.CompilerParams(dimension_semantics=sem, vmem_limit_bytes=V7X_VMEM_LIMIT_BYTES)


def _pick_tile(n, cap, mult):
    best = None
    t = mult
    while t <= min(n, cap):
        if n % t == 0:
            best = t
        t += mult
    assert best is not None, (n, cap, mult)
    return best


def _rms(x, eps=RMS_EPS):
    return x * lax.rsqrt(jnp.mean(x * x, axis=-1, keepdims=True) + eps)


def _side_cast_specs(mats, n_steps, step_of):
    specs = []
    for w in mats:
        rb = PACKED_ROWS * -(-w.shape[0] // (PACKED_ROWS * n_steps))
        assert w.shape[0] % rb == 0, (w.shape, rb)
        nblk = w.shape[0] // rb
        specs.append(pl.BlockSpec((rb, w.shape[1]),
                                  lambda *idx, nblk=nblk: (jnp.minimum(step_of(*idx), nblk - 1), 0)))
    return specs


def _side_cast(side_in, side_out):
    for src, dst in zip(side_in, side_out):
        dst[...] = src[...].astype(BF16)


def _ada_kernel(ct_ref, w_ref, b_ref, o_ref, sb_ref, *, n_rows):
    lanes = sb_ref.shape[2]

    @pl.when(pl.program_id(0) == 0)
    def _():
        ct = ct_ref[...]
        s = ct * jax.nn.sigmoid(ct)
        for r in range(n_rows):
            sb_ref[r] = jnp.broadcast_to(s[:, r:r + 1], sb_ref.shape[1:])

    d, tn = w_ref.shape
    reps = tn // lanes
    acc = [jnp.zeros((SUBLANES, tn), F32) for _ in range(n_rows)]
    for kb in range(d // SUBLANES):
        rows = slice(kb * SUBLANES, (kb + 1) * SUBLANES)
        w = w_ref[rows, :]
        for r in range(n_rows):
            acc[r] = acc[r] + w * jnp.concatenate([sb_ref[r, rows, :]] * reps, axis=1)
    out = [jnp.sum(a, axis=0, keepdims=True) for a in acc]
    out.append(jnp.zeros((o_ref.shape[0] - n_rows, tn), F32))
    o_ref[...] = jnp.concatenate(out, axis=0) + b_ref[...]


def _ada_mod(c_cols, n_rows, w_ada, b_ada):
    d, rows = c_cols.shape
    n = w_ada.shape[1]
    tn = _pick_tile(n, 2048, 128)
    return pl.pallas_call(
        functools.partial(_ada_kernel, n_rows=n_rows),
        out_shape=jax.ShapeDtypeStruct((rows, n), F32),
        grid=(n // tn,),
        in_specs=[pl.BlockSpec((d, rows), lambda j: (0, 0)),
                  pl.BlockSpec((d, tn), lambda j: (0, j)),
                  pl.BlockSpec((1, tn), lambda j: (0, j))],
        out_specs=pl.BlockSpec((rows, tn), lambda j: (0, j)),
        scratch_shapes=[pltpu.VMEM((n_rows, d, HEAD_DIM), F32)],
        compiler_params=_params("arbitrary"),
        name="ada_mod",
    )(c_cols, w_ada, b_ada.reshape(1, n))


def _ffn_kernel(x_ref, shift_ref, scale_ref, gate_ref, g_ref, wg_ref, wu_ref, wd_ref, *refs, tn, n_side):
    side_in, o_ref, side_out, h_ref = refs[:n_side], refs[n_side], refs[n_side + 1:-1], refs[-1]
    _side_cast(side_in, side_out)
    k = pl.program_id(1)
    last = pl.num_programs(1) - 1
    d = o_ref.shape[1]

    def chunk(h, emit):
        g = jnp.dot(h, wg_ref[...], preferred_element_type=F32)
        u = jnp.dot(h, wu_ref[...], preferred_element_type=F32)
        a = (g * jax.nn.sigmoid(g) * u).astype(BF16)
        for n0 in range(0, d, tn):
            emit(n0, jnp.dot(a, wd_ref[:, n0:n0 + tn], preferred_element_type=F32))

    @pl.when(k == 0)
    def _():
        y = _rms(x_ref[...]) * g_ref[...]
        h = (y * (1.0 + scale_ref[0]) + shift_ref[0]).astype(BF16)
        h_ref[...] = h

        def emit(n0, part):
            o_ref[:, n0:n0 + tn] = part
        chunk(h, emit)

    @pl.when(jnp.logical_and(k > 0, k < last))
    def _():
        def emit(n0, part):
            o_ref[:, n0:n0 + tn] += part
        chunk(h_ref[...], emit)

    @pl.when(jnp.logical_and(k > 0, k == last))
    def _():
        def emit(n0, part):
            cols = slice(n0, n0 + tn)
            o_ref[:, cols] = x_ref[:, cols] + 0.5 * gate_ref[0][:, cols] * (o_ref[:, cols] + part)
        chunk(h_ref[...], emit)


def _ffn(x, mod3, row_of_tile, slot, g, w_up, w_down, tm, to_bf16=()):
    t, d = x.shape
    f = w_down.shape[0]
    tf = _pick_tile(f, 512, 128)
    nf = f // tf
    assert nf >= 2, "the kernel keeps separate first / last chunk paths"
    mod_spec = lambda c: pl.BlockSpec((1, 1, d), lambda i, k: (row_of_tile(i), 0, 3 * slot + c))
    row_tile = pl.BlockSpec((tm, d), lambda i, k: (i, 0))
    side_specs = _side_cast_specs(to_bf16, (t // tm) * nf, lambda i, k: i * nf + k)
    outs = pl.pallas_call(
        functools.partial(_ffn_kernel, tn=_pick_tile(d, 512, 128), n_side=len(to_bf16)),
        out_shape=(jax.ShapeDtypeStruct((t, d), F32), *[jax.ShapeDtypeStruct(w.shape, BF16) for w in to_bf16]),
        grid=(t // tm, nf),
        in_specs=[row_tile, mod_spec(0), mod_spec(1), mod_spec(2),
                  pl.BlockSpec((1, d), lambda i, k: (0, 0)),
                  pl.BlockSpec((d, tf), lambda i, k: (0, k)),
                  pl.BlockSpec((d, tf), lambda i, k: (0, k + nf)),
                  pl.BlockSpec((tf, d), lambda i, k: (k, 0)),
                  *side_specs],
        out_specs=(row_tile, *side_specs),
        scratch_shapes=[pltpu.VMEM((tm, d), BF16)],
        compiler_params=_params("arbitrary", "arbitrary"),
        name="ffn",
    )(x, mod3, mod3, mod3, g.reshape(1, d), w_up, w_up, w_down, *to_bf16)
    return outs[0], outs[1:]


def _rope(x, cos, sin):
    lane = lax.broadcasted_iota(jnp.int32, x.shape, 1)
    partner = jnp.where(lane % 2 == 0, pltpu.roll(x, HEAD_DIM - 1, axis=1), pltpu.roll(x, 1, axis=1))
    return x * cos + partner * sin


HALO = 16


def _in_proj_kernel(x_ref, xp_ref, xn_ref, shift_ref, scale_ref, g_ref, w_ref, qn_ref, kn_ref,
                    cos_ref, sin_ref, cw_ref, cb_ref, q_ref, k_ref, v_ref, x0c_ref, vg_ref, *,
                    q_scale, tiles_per_seq, cw):
    i = pl.program_id(0)
    tm = x_ref.shape[0]
    attn_w = q_ref.shape[1]
    kvw = k_ref.shape[1]
    c = x0c_ref.shape[1]
    first = (i % tiles_per_seq) == 0
    last = (i % tiles_per_seq) == tiles_per_seq - 1

    x_ext = jnp.concatenate([xp_ref[...], x_ref[...], xn_ref[...]], axis=0)
    y = _rms(x_ext) * g_ref[...]
    h_ext = (y * (1.0 + scale_ref[0]) + shift_ref[0]).astype(BF16)
    h = h_ext[HALO:HALO + tm]

    qkv_w = attn_w + 2 * kvw
    u = jnp.dot(h, w_ref[:, :qkv_w], preferred_element_type=F32)

    def normed(col, gain):
        return _rope(_rms(u[:, col:col + HEAD_DIM]) * gain, cos_ref[...], sin_ref[...])

    for hd in range(attn_w // HEAD_DIM):
        col = hd * HEAD_DIM
        q_ref[:, col:col + HEAD_DIM] = (normed(col, qn_ref[...]) * q_scale).astype(BF16)
    for hd in range(kvw // HEAD_DIM):
        col = hd * HEAD_DIM
        k_ref[:, col:col + HEAD_DIM] = normed(attn_w + col, kn_ref[...]).astype(BF16)
    v_ref[...] = u[:, attn_w + kvw:].astype(BF16)

    row = lax.broadcasted_iota(jnp.int32, (tm, cw), 0)
    at_start = jnp.logical_and(first, row == 0)
    at_end = jnp.logical_and(last, row == tm - 1)
    ext = tm + 2 * HALO

    def conv(group, c0):
        col = qkv_w + group * c + c0
        ue = jnp.dot(h_ext, w_ref[:, col:col + cw], preferred_element_type=F32)
        um = jnp.where(at_start, 0.0, pltpu.roll(ue, 1, axis=0)[HALO:HALO + tm])
        up = jnp.where(at_end, 0.0, pltpu.roll(ue, ext - 1, axis=0)[HALO:HALO + tm])
        w = cw_ref[group][:, c0:c0 + cw]
        return um * w[0:1] + ue[HALO:HALO + tm] * w[1:2] + up * w[2:3] + cb_ref[group][:, c0:c0 + cw]

    for c0 in range(0, c, cw):
        x0c_ref[:, c0:c0 + cw] = conv(0, c0).astype(BF16)
        vg_ref[:, c0:c0 + cw] = (conv(2, c0) * conv(1, c0)).astype(BF16)


def _in_proj(x, mod3, row_of_tile, g, w_in, qn, kn, cos, sin, conv_w, conv_b, seq, tm):
    t, d = x.shape
    kvw = N_KV_HEADS * HEAD_DIM
    attn_w = d // 2
    c = (w_in.shape[1] - attn_w - 2 * kvw) // 3
    cw = _pick_tile(c, 512, 128)
    pos_tiles = seq // tm
    hb = tm // HALO
    nhb = t // HALO
    w3 = conv_w.reshape(3, 3, c).transpose(1, 0, 2)
    b3 = conv_b.reshape(3, 1, c)
    mod_spec = lambda ch: pl.BlockSpec((1, 1, d), lambda i: (row_of_tile(i), 0, 3 + ch))
    const = lambda shape: pl.BlockSpec(shape, lambda i: (0,) * len(shape), pipeline_mode=pl.Buffered(1))
    kern = functools.partial(_in_proj_kernel, q_scale=HEAD_DIM ** -0.5 * math.log2(math.e),
                             tiles_per_seq=pos_tiles, cw=cw)
    return pl.pallas_call(
        kern,
        out_shape=(jax.ShapeDtypeStruct((t, attn_w), BF16),
                   jax.ShapeDtypeStruct((t, kvw), BF16),
                   jax.ShapeDtypeStruct((t, kvw), BF16),
                   jax.ShapeDtypeStruct((t, c), BF16),
                   jax.ShapeDtypeStruct((t, c), BF16)),
        grid=(t // tm,),
        in_specs=[pl.BlockSpec((tm, d), lambda i: (i, 0)),
                  pl.BlockSpec((HALO, d), lambda i: (jnp.maximum(i * hb - 1, 0), 0)),
                  pl.BlockSpec((HALO, d), lambda i: (jnp.minimum((i + 1) * hb, nhb - 1), 0)),
                  mod_spec(0), mod_spec(1),
                  const((1, d)), const(w_in.shape),
                  const((1, HEAD_DIM)), const((1, HEAD_DIM)),
                  pl.BlockSpec((tm, HEAD_DIM), lambda i: (i % pos_tiles, 0)),
                  pl.BlockSpec((tm, HEAD_DIM), lambda i: (i % pos_tiles, 0)),
                  const((3, 3, c)), const((3, 1, c))],
        out_specs=(pl.BlockSpec((tm, attn_w), lambda i: (i, 0)),
                   pl.BlockSpec((tm, kvw), lambda i: (i, 0)),
                   pl.BlockSpec((tm, kvw), lambda i: (i, 0)),
                   pl.BlockSpec((tm, c), lambda i: (i, 0)),
                   pl.BlockSpec((tm, c), lambda i: (i, 0))),
        compiler_params=_params("parallel"),
        name="in_proj",
    )(x, x, x, mod3, mod3, g.reshape(1, d), w_in, qn, kn, cos, sin, w3, b3)


def _ctx_kv_kernel(x_ref, shift_ref, scale_ref, g_ref, w_ref, kn_ref, k_ref, v_ref):
    y = _rms(x_ref[...]) * g_ref[...]
    h = (y * (1.0 + scale_ref[0]) + shift_ref[0]).astype(BF16)
    u = jnp.dot(h, w_ref[...], preferred_element_type=F32)
    for hd in range(N_KV_HEADS):
        k_ref[:, hd * HEAD_DIM:(hd + 1) * HEAD_DIM] = (
            _rms(u[:, hd * HEAD_DIM:(hd + 1) * HEAD_DIM]) * kn_ref[...]).astype(BF16)
    v_ref[...] = u[:, N_KV_HEADS * HEAD_DIM:].astype(BF16)


def _ctx_kv(x, mod3, ctx_row, g, w_kv, kn, tm):
    t, d = x.shape
    kvw = N_KV_HEADS * HEAD_DIM
    mod_spec = lambda c: pl.BlockSpec((1, 1, d), lambda i: (ctx_row, 0, 3 + c))
    return pl.pallas_call(
        _ctx_kv_kernel,
        out_shape=(jax.ShapeDtypeStruct((t, kvw), BF16), jax.ShapeDtypeStruct((t, kvw), BF16)),
        grid=(t // tm,),
        in_specs=[pl.BlockSpec((tm, d), lambda i: (i, 0)),
                  mod_spec(0), mod_spec(1),
                  pl.BlockSpec((1, d), lambda i: (0, 0)),
                  pl.BlockSpec((d, 2 * kvw), lambda i: (0, 0)),
                  pl.BlockSpec((1, HEAD_DIM), lambda i: (0, 0))],
        out_specs=(pl.BlockSpec((tm, kvw), lambda i: (i, 0)),
                   pl.BlockSpec((tm, kvw), lambda i: (i, 0))),
        compiler_params=_params("parallel"),
        name="ctx_kv",
    )(x, mod3, mod3, g.reshape(1, d), w_kv, kn)


def _attn_kernel(q_ref, k_ref, v_ref, *refs, group, nq, n_side):
    side_in, o_ref, side_out = refs[:n_side], refs[n_side], refs[n_side + 1:2 * n_side + 1]
    s0_ref, s1_ref, mx0_ref, mx1_ref, ls_ref, acc_ref = refs[2 * n_side + 1:]
    _side_cast(side_in, side_out)
    i = pl.program_id(2)
    tq = q_ref.shape[0]
    nk, _, tk = s0_ref.shape
    lanes = HEAD_DIM
    bufs = ((s0_ref, mx0_ref), (s1_ref, mx1_ref))

    def scores(j, q, s_ref, mx_ref):
        start = pl.multiple_of(j * tk, tk)
        s = lax.dot_general(q, k_ref[0, pl.ds(start, tk), :], (((1,), (1,)), ((), ())),
                            preferred_element_type=F32)
        s_ref[j] = s
        mx = mx_ref[...]
        for c in range(tk // lanes):
            mx = jnp.maximum(mx, s[:, c * lanes:(c + 1) * lanes])
        mx_ref[...] = mx

    def values(j, s_ref, mx_ref):
        start = pl.multiple_of(j * tk, tk)
        s = s_ref[j]
        m = mx_ref[...]
        ls = ls_ref[...]
        parts = []
        for c in range(tk // lanes):
            pc = jnp.exp2(s[:, c * lanes:(c + 1) * lanes] - m)
            ls = ls + pc
            parts.append(pc.astype(BF16))
        ls_ref[...] = ls
        p = jnp.concatenate(parts, axis=1)
        acc_ref[...] += jnp.dot(p, v_ref[0, pl.ds(start, tk), :], preferred_element_type=F32)

    def begin_scores(mx_ref):
        mx_ref[...] = jnp.full_like(mx_ref, -jnp.inf)
        return jnp.concatenate([q_ref[:, h * HEAD_DIM:(h + 1) * HEAD_DIM] for h in range(group)], axis=0)

    def end_scores(mx_ref):
        mx_ref[...] = jnp.broadcast_to(jnp.max(mx_ref[...], axis=-1, keepdims=True), mx_ref.shape)

    def begin_values():
        ls_ref[...] = jnp.zeros_like(ls_ref)
        acc_ref[...] = jnp.zeros_like(acc_ref)

    def end_values():
        o = acc_ref[...] / jnp.sum(ls_ref[...], axis=-1, keepdims=True)
        for h in range(group):
            o_ref[:, h * HEAD_DIM:(h + 1) * HEAD_DIM] = o[h * tq:(h + 1) * tq].astype(BF16)

    @pl.when(i == 0)
    def _():
        s_ref, mx_ref = bufs[0]
        q = begin_scores(mx_ref)
        lax.fori_loop(0, nk, lambda j, c: (scores(j, q, s_ref, mx_ref), c)[1], 0)
        end_scores(mx_ref)

    for parity in range(2):
        @pl.when(jnp.logical_and(jnp.logical_and(i > 0, i < nq), i % 2 == parity))
        def _():
            (cs_ref, cmx_ref), (ps_ref, pmx_ref) = bufs[parity], bufs[1 - parity]
            q = begin_scores(cmx_ref)
            begin_values()

            def both(j, c):
                scores(j, q, cs_ref, cmx_ref)
                values(j, ps_ref, pmx_ref)
                return c

            lax.fori_loop(0, nk, both, 0, unroll=True)
            end_scores(cmx_ref)
            end_values()

    @pl.when(i == nq)
    def _():
        s_ref, mx_ref = bufs[(nq - 1) % 2]
        begin_values()
        lax.fori_loop(0, nk, lambda j, c: (values(j, s_ref, mx_ref), c)[1], 0)
        end_values()


def _attention(q, k_all, v_all, batch, seq, to_bf16=()):
    t, attn_w = q.shape
    lk = k_all.shape[1]
    group = attn_w // HEAD_DIM // N_KV_HEADS
    gw = group * HEAD_DIM
    tq = _pick_tile(seq, 128, 8)
    tk = _pick_tile(lk, 1408, 128)
    nq = seq // tq
    rows = group * tq
    side_specs = _side_cast_specs(to_bf16, batch * N_KV_HEADS * (nq + 1),
                                  lambda b, g, i: (b * N_KV_HEADS + g) * (nq + 1) + i)
    kern = functools.partial(_attn_kernel, group=group, nq=nq, n_side=len(to_bf16))
    outs = pl.pallas_call(
        kern,
        out_shape=(jax.ShapeDtypeStruct((t, attn_w), BF16),
                   *[jax.ShapeDtypeStruct(w.shape, BF16) for w in to_bf16]),
        grid=(batch, N_KV_HEADS, nq + 1),
        in_specs=[pl.BlockSpec((tq, gw), lambda b, g, i: (b * nq + jnp.minimum(i, nq - 1), g)),
                  pl.BlockSpec((1, lk, HEAD_DIM), lambda b, g, i: (b, 0, g)),
                  pl.BlockSpec((1, lk, HEAD_DIM), lambda b, g, i: (b, 0, g)),
                  *side_specs],
        out_specs=(pl.BlockSpec((tq, gw), lambda b, g, i: (b * nq + jnp.maximum(i - 1, 0), g)),
                   *side_specs),
        scratch_shapes=[pltpu.VMEM((lk // tk, rows, tk), F32), pltpu.VMEM((lk // tk, rows, tk), F32),
                        pltpu.VMEM((rows, HEAD_DIM), F32), pltpu.VMEM((rows, HEAD_DIM), F32),
                        pltpu.VMEM((rows, HEAD_DIM), F32), pltpu.VMEM((rows, HEAD_DIM), F32)],
        compiler_params=_params("arbitrary", "arbitrary", "arbitrary"),
        name="attention",
    )(q, k_all, v_all, *to_bf16)
    return outs[0], outs[1:]


def _dft_tables(n1, n2, k1p):
    n = n1 * n2
    k1 = np.arange(k1p)[:, None]
    a = np.arange(n1)[None, :]
    m2 = np.arange(n2)[:, None, None]
    theta = 2.0 * np.pi * ((((n2 * a * k1)[None] + m2 * k1[None]) % n) / n)
    k2 = np.arange(n2)[:, None]
    b = np.arange(n2)[None, :]
    phi = 2.0 * np.pi * ((k2 * b) % n2) / n2
    c, s = np.cos(phi), np.sin(phi)
    fmat = np.block([[c, s], [-s, c]])
    return np.cos(theta), np.sin(theta), fmat


def _mix_forward(cos, sin, n1h):
    n2, k1p, _ = cos.shape
    g = n2 // SUBLANES
    a = np.stack([cos[:, :, :n1h], -sin[:, :, :n1h]], axis=1).reshape(g, SUBLANES, 2, k1p, n1h)
    m = np.zeros((g, 2, k1p, SUBLANES, n1h, SUBLANES))
    for j in range(SUBLANES):
        m[:, :, :, j, :, j] = a[:, j]
    return m.reshape(g, 2 * k1p * SUBLANES, n1h * SUBLANES)


def _mix_inverse(cos, sin, n1h, n1):
    n2, k1p, _ = cos.shape
    g = n2 // SUBLANES
    k1 = np.arange(k1p)
    weight = np.where((k1 == 0) | (k1 == n1 // 2), 1.0, np.where(k1 < n1 // 2, 2.0, 0.0)) / (n1 * n2)
    a = np.stack([cos[:, :, :n1h], -sin[:, :, :n1h]], axis=1) * weight[None, None, :, None]
    a = a.reshape(g, SUBLANES, 2, k1p, n1h)
    m = np.zeros((g, n1h, SUBLANES, 2, k1p, SUBLANES))
    for j in range(SUBLANES):
        m[:, :, j, :, :, j] = a[:, j].transpose(0, 3, 1, 2)
    return m.reshape(g, n1h * SUBLANES, 2 * k1p * SUBLANES)


def _stage_a_kernel(m_ref, x_ref, yr_ref, yi_ref):
    _, n1h, _, c = x_ref.shape
    k1p = yr_ref.shape[1]
    re, im = [], []
    xs = x_ref[0].astype(F32)
    for half in range(2):
        rows = slice(half * SUBLANES, (half + 1) * SUBLANES)
        x = xs[:, rows, :].reshape(n1h * SUBLANES, c).astype(BF16)
        y = jnp.dot(m_ref[half], x, preferred_element_type=F32)
        re.append(y[:k1p * SUBLANES].reshape(k1p, SUBLANES, c))
        im.append(y[k1p * SUBLANES:].reshape(k1p, SUBLANES, c))
    yr_ref[0] = jnp.concatenate(re, axis=1).astype(BF16)
    yi_ref[0] = jnp.concatenate(im, axis=1).astype(BF16)


def _stage_a(mix, vg4, k1p):
    b, n1h, n2, c = vg4.shape
    return pl.pallas_call(
        _stage_a_kernel,
        out_shape=(jax.ShapeDtypeStruct((b, k1p, n2, c), BF16),) * 2,
        grid=(n2 // PACKED_ROWS, b),
        in_specs=[pl.BlockSpec((2,) + mix.shape[1:], lambda g, bi: (g, 0, 0)),
                  pl.BlockSpec((1, n1h, PACKED_ROWS, c), lambda g, bi: (bi, 0, g, 0))],
        out_specs=(pl.BlockSpec((1, k1p, PACKED_ROWS, c), lambda g, bi: (bi, 0, g, 0)),) * 2,
        compiler_params=_params("parallel", "parallel"),
        name="hyena_dft_a",
    )(mix, vg4)


def _stage_b_kernel(yr_ref, yi_ref, f_ref, ft_ref, kr_ref, ki_ref, ur_ref, ui_ref):
    kb, n2 = yr_ref.shape[1:3]
    for r in range(kb):
        y = jnp.concatenate([yr_ref[0, r], yi_ref[0, r]], axis=0)
        z = jnp.dot(f_ref[...], y, preferred_element_type=F32)
        zr, zi = z[:n2], z[n2:]
        kr = kr_ref[r].astype(F32)
        ki = ki_ref[r].astype(F32)
        p = jnp.concatenate([zr * kr - zi * ki, zr * ki + zi * kr], axis=0).astype(BF16)
        u = jnp.dot(ft_ref[...], p, preferred_element_type=F32)
        ur_ref[0, r] = u[:n2].astype(BF16)
        ui_ref[0, r] = u[n2:].astype(BF16)


def _stage_b(yr, yi, fmat, fmat_t, kfr, kfi, kb):
    b, k1p, n2, c = yr.shape
    blk = pl.BlockSpec((1, kb, n2, c), lambda k, bi: (bi, k, 0, 0))
    flt = pl.BlockSpec((kb, n2, c), lambda k, bi: (k, 0, 0))
    mat = pl.BlockSpec((2 * n2, 2 * n2), lambda k, bi: (0, 0))
    return pl.pallas_call(
        _stage_b_kernel,
        out_shape=(jax.ShapeDtypeStruct((b, k1p, n2, c), BF16),) * 2,
        grid=(k1p // kb, b),
        in_specs=[blk, blk, mat, mat, flt, flt],
        out_specs=(blk, blk),
        compiler_params=_params("parallel", "parallel"),
        name="hyena_dft_b",
    )(yr, yi, fmat, fmat_t, kfr, kfi)


def _stage_c_kernel(m_ref, ur_ref, ui_ref, vg_ref, x0_ref, hb_ref, g_ref, o_ref):
    _, k1p, _, c = ur_ref.shape
    n1h = vg_ref.shape[1]
    ur = ur_ref[0].astype(F32)
    ui = ui_ref[0].astype(F32)
    vgs = vg_ref[0].astype(F32)
    x0s = x0_ref[0].astype(F32)
    for half in range(2):
        rows = slice(half * SUBLANES, (half + 1) * SUBLANES)
        u = jnp.concatenate([ur[:, rows, :].reshape(k1p * SUBLANES, c),
                             ui[:, rows, :].reshape(k1p * SUBLANES, c)], axis=0).astype(BF16)
        conv = jnp.dot(m_ref[half], u, preferred_element_type=F32)
        vg = vgs[:, rows, :].reshape(n1h * SUBLANES, c)
        x0 = x0s[:, rows, :].reshape(n1h * SUBLANES, c)
        hyo = (conv + hb_ref[...] * vg) * x0
        o_ref[0, :, rows, :] = (_rms(hyo) * g_ref[...]).reshape(n1h, SUBLANES, c)


def _stage_c(mix, ur, ui, vg4, x04, hy_bias, g_hy):
    b, n1h, n2, c = vg4.shape
    k1p = ur.shape[1]
    ublk = pl.BlockSpec((1, k1p, PACKED_ROWS, c), lambda g, bi: (bi, 0, g, 0))
    xblk = pl.BlockSpec((1, n1h, PACKED_ROWS, c), lambda g, bi: (bi, 0, g, 0))
    vec = pl.BlockSpec((1, c), lambda g, bi: (0, 0))
    return pl.pallas_call(
        _stage_c_kernel,
        out_shape=jax.ShapeDtypeStruct((b, n1h, n2, c), F32),
        grid=(n2 // PACKED_ROWS, b),
        in_specs=[pl.BlockSpec((2,) + mix.shape[1:], lambda g, bi: (g, 0, 0)),
                  ublk, ublk, xblk, xblk, vec, vec],
        out_specs=xblk,
        compiler_params=_params("parallel", "parallel"),
        name="hyena_dft_c",
    )(mix, ur, ui, vg4, x04, hy_bias.reshape(1, c), g_hy.reshape(1, c))


def _filter_a_kernel(a_ref, w1t_ref, w1c_ref, w1s_ref, b1_ref, w2_ref, b2_ref, w3_ref, b3_ref,
                     w4_ref, fr_ref, dec_ref, *refs, seq, n_side):
    side_in, (yr_ref, yi_ref, norm_ref), side_out = refs[:n_side], refs[n_side:n_side + 3], refs[n_side + 3:]
    _side_cast(side_in, side_out)
    g = pl.program_id(0)
    k1p = yr_ref.shape[0]
    c = norm_ref.shape[1]
    n1 = a_ref.shape[2]
    n1h = n1 // 2
    n2 = DFT_N2

    def lags(idx, j):
        tprime = idx * n2 + (g * SUBLANES + j)
        fwd = idx < n1h
        return tprime, fwd, jnp.where(fwd, tprime, 2 * seq - tprime).astype(F32)

    def stacked(idx, axis):
        parts = [lags(idx, j) for j in range(SUBLANES)]
        return [jnp.concatenate([p[k] for p in parts], axis=axis) for k in range(3)]

    _, _, pos_l = stacked(lax.broadcasted_iota(jnp.int32, (1, n1), 1), 1)
    tprime_r, fwd_r, pos_r = stacked(lax.broadcasted_iota(jnp.int32, (n1, 1), 0), 0)

    band = lax.broadcasted_iota(jnp.int32, (FILTER_BANDS, 1), 0).astype(F32)
    f = 1e-4 + band * ((FILTER_BANDS - 1 - 1e-4) / (FILTER_BANDS - 1))
    ang = f * ((2.0 * math.pi * pos_l) / seq)
    fr = fr_ref[...]
    dot = functools.partial(jnp.dot, preferred_element_type=F32, precision=_HI)
    h = w1t_ref[...] * (pos_l / (seq - 1.0)) + dot(w1c_ref[...], jnp.cos(ang)) + dot(w1s_ref[...], -jnp.sin(ang))
    h = jnp.sin(fr * (h + b1_ref[...]))
    h = jnp.sin(fr * (dot(w2_ref[...], h) + b2_ref[...]))
    h = jnp.sin(fr * (dot(w3_ref[...], h) + b3_ref[...]))
    h = lax.dot_general(h, w4_ref[...], (((0,), (0,)), ((), ())), preferred_element_type=F32)
    t_r = pos_r / (seq - 1.0)
    dec = jnp.abs(dec_ref[...])
    kf = h[:, :c] * jnp.exp(-t_r * dec[0:1, :])
    kb = h[:, c:] * jnp.exp(-t_r * dec[1:2, :])
    kk = jnp.where(fwd_r, kf, jnp.where(tprime_r == seq, 0.0, kb))

    @pl.when(g == 0)
    def _():
        norm_ref[...] = jnp.zeros_like(norm_ref)

    norm_ref[...] += jnp.sum(jnp.abs(kk), axis=0, keepdims=True)
    for j in range(SUBLANES):
        y = jnp.dot(a_ref[j], kk[j * n1:(j + 1) * n1].astype(BF16), preferred_element_type=F32)
        yr_ref[:, j * c:(j + 1) * c] = y[:k1p].astype(BF16)
        yi_ref[:, j * c:(j + 1) * c] = y[k1p:].astype(BF16)


def _filter_a(a_full, flt, seq, c, to_bf16=()):
    w1, b1, w2, b2, w3, b3, w4, freq, decay = flt
    hid = w2.shape[0]
    n2, k1p2, n1 = a_full.shape
    k1p = k1p2 // 2
    full = lambda shape: pl.BlockSpec(shape, lambda g: (0,) * len(shape))
    side_specs = _side_cast_specs(to_bf16, n2 // SUBLANES, lambda g: g)
    kern = functools.partial(_filter_a_kernel, seq=seq, n_side=len(to_bf16))
    outs = pl.pallas_call(
        kern,
        out_shape=(jax.ShapeDtypeStruct((k1p, n2 * c), BF16),
                   jax.ShapeDtypeStruct((k1p, n2 * c), BF16),
                   jax.ShapeDtypeStruct((1, c), F32),
                   *[jax.ShapeDtypeStruct(w.shape, BF16) for w in to_bf16]),
        grid=(n2 // SUBLANES,),
        in_specs=[pl.BlockSpec((SUBLANES, k1p2, n1), lambda g: (g, 0, 0)),
                  full((hid, 1)), full((hid, FILTER_BANDS)), full((hid, FILTER_BANDS)), full((hid, 1)),
                  full((hid, hid)), full((hid, 1)), full((hid, hid)), full((hid, 1)),
                  full((hid, 2 * c)), full((hid, 1)), full((2, c)),
                  *side_specs],
        out_specs=(pl.BlockSpec((k1p, SUBLANES * c), lambda g: (0, g)),
                   pl.BlockSpec((k1p, SUBLANES * c), lambda g: (0, g)),
                   pl.BlockSpec((1, c), lambda g: (0, 0)),
                   *side_specs),
        compiler_params=_params("arbitrary"),
        name="hyena_filter_a",
    )(a_full, w1[0:1].T, w1[1:1 + FILTER_BANDS].T, w1[1 + FILTER_BANDS:].T, b1.reshape(hid, 1),
      w2.T, b2.reshape(hid, 1), w3.T, b3.reshape(hid, 1), w4, freq.reshape(hid, 1), decay, *to_bf16)
    return outs[0], outs[1], outs[2], outs[3:]


def _filter_b_kernel(yr_ref, yi_ref, f_ref, norm_ref, kr_ref, ki_ref):
    kb, n2 = yr_ref.shape[:2]
    for r in range(kb):
        y = jnp.concatenate([yr_ref[r], yi_ref[r]], axis=0)
        z = jnp.dot(f_ref[...], y, preferred_element_type=F32) / norm_ref[...]
        kr_ref[r] = z[:n2].astype(BF16)
        ki_ref[r] = z[n2:].astype(BF16)


def _filter_b(yr, yi, fmat, norm, kb):
    k1p, n2, c = yr.shape
    blk = pl.BlockSpec((kb, n2, c), lambda k: (k, 0, 0))
    return pl.pallas_call(
        _filter_b_kernel,
        out_shape=(jax.ShapeDtypeStruct((k1p, n2, c), BF16),) * 2,
        grid=(k1p // kb,),
        in_specs=[blk, blk,
                  pl.BlockSpec((2 * n2, 2 * n2), lambda k: (0, 0)),
                  pl.BlockSpec((1, c), lambda k: (0, 0))],
        out_specs=(blk, blk),
        compiler_params=_params("parallel"),
        name="hyena_filter_b",
    )(yr, yi, fmat, norm)


def _dft_plan(seq):
    n2 = DFT_N2
    n1h = seq // n2
    n1 = 2 * n1h
    k1p = min(n1, -(-(n1 // 2 + 1) // SUBLANES) * SUBLANES)
    return n1h, n1, n2, k1p, _pick_tile(k1p, 8, 1)


def _hyena_filter(flt, seq, c, to_bf16=()):
    n1h, n1, n2, k1p, kb = _dft_plan(seq)
    cos, sin, fmat_np = _dft_tables(n1, n2, k1p)
    a_full = jnp.asarray(np.concatenate([cos, -sin], axis=1), BF16)
    fyr, fyi, norm, sides = _filter_a(a_full, flt, seq, c, to_bf16)
    kfr, kfi = _filter_b(fyr.reshape(k1p, n2, c), fyi.reshape(k1p, n2, c), jnp.asarray(fmat_np, BF16), norm, kb)
    return kfr, kfi, sides


def _hyena(x0c, vg, kfr, kfi, hy_bias, g_hy, batch, seq):
    c = vg.shape[1]
    n1h, n1, n2, k1p, kb = _dft_plan(seq)
    cos, sin, fmat_np = _dft_tables(n1, n2, k1p)
    mix_a = jnp.asarray(_mix_forward(cos, sin, n1h), BF16)
    mix_c = jnp.asarray(_mix_inverse(cos, sin, n1h, n1), BF16)
    fmat = jnp.asarray(fmat_np, BF16)
    fmat_t = jnp.asarray(fmat_np.T, BF16)

    vg4 = vg.reshape(batch, n1h, n2, c)
    x04 = x0c.reshape(batch, n1h, n2, c)
    yr, yi = _stage_a(mix_a, vg4, k1p)
    ur, ui = _stage_b(yr, yi, fmat, fmat_t, kfr, kfi, kb)
    y = _stage_c(mix_c, ur, ui, vg4, x04, hy_bias, g_hy)
    return y.reshape(batch * seq, c)


def _out_proj_kernel(x_ref, gate_ref, attn_ref, hy_ref, ga_ref, wa_ref, wh_ref, o_ref):
    ya = (_rms(attn_ref[...].astype(F32)) * ga_ref[...]).astype(BF16)
    y = jnp.dot(ya, wa_ref[...], preferred_element_type=F32)
    y += jnp.dot(hy_ref[...].astype(BF16), wh_ref[...], preferred_element_type=F32)
    o_ref[...] = x_ref[...] + gate_ref[0] * y


def _out_proj(x, mod3, row_of_tile, attn, hyn, g_attn, w_out, tm):
    t, d = x.shape
    aw = attn.shape[1]
    hw = hyn.shape[1]
    const = lambda shape, idx: pl.BlockSpec(shape, lambda i: idx, pipeline_mode=pl.Buffered(1))
    return pl.pallas_call(
        _out_proj_kernel,
        out_shape=jax.ShapeDtypeStruct((t, d), F32),
        grid=(t // tm,),
        in_specs=[pl.BlockSpec((tm, d), lambda i: (i, 0)),
                  pl.BlockSpec((1, 1, d), lambda i: (row_of_tile(i), 0, 5)),
                  pl.BlockSpec((tm, aw), lambda i: (i, 0)),
                  pl.BlockSpec((tm, hw), lambda i: (i, 0)),
                  const((1, aw), (0, 0)), const((aw, d), (0, 0)), const((hw, d), (1, 0))],
        out_specs=pl.BlockSpec((tm, d), lambda i: (i, 0)),
        compiler_params=_params("parallel"),
        name="out_proj",
    )(x, mod3, attn, hyn, g_attn.reshape(1, aw), w_out, w_out)


def _rope_tables(seq):
    half = HEAD_DIM // 2
    pos = np.arange(seq)
    inv = ROPE_THETA ** (-np.arange(0, half, 2, dtype=np.float64) / half)
    ang = np.concatenate([(pos // GRID_W)[:, None] * inv, (pos % GRID_W)[:, None] * inv], axis=-1)
    cos = np.repeat(np.cos(ang), 2, axis=-1)
    sin = np.stack([-np.sin(ang), np.sin(ang)], axis=-1).reshape(seq, HEAD_DIM)
    return jnp.asarray(cos, F32), jnp.asarray(sin, F32)


def kernel(x, c, ctx, c_ctx, w_ada, b_ada, g_norm, w_ffn1_up, w_ffn1_down, w_ffn2_up, w_ffn2_down,
           w_in, q_norm, k_norm, conv_w, conv_b, flt_w1, flt_b1, flt_w2, flt_b2, flt_w3, flt_b3,
           flt_w4, flt_freq, flt_decay, hy_bias, g_out, w_out):
    batch, seq, d = x.shape
    lc = ctx.shape[1]
    depth = w_ada.shape[0]
    assert depth == 1, "context-update path of deeper stacks is not implemented"
    attn_w = d // 2
    kvw = N_KV_HEADS * HEAD_DIM
    tm = _pick_tile(seq, 512, 128)
    tmf = _pick_tile(seq, FFN_ROWS, 128)
    tmc = _pick_tile(batch * lc, 512, 8)
    lat_row = lambda i: i // (seq // tm)
    lat_row_ffn = lambda i: i // (seq // tmf)
    ctx_row = lambda i: batch

    xs = x.reshape(batch * seq, d)
    cs = ctx.reshape(batch * lc, d)
    rows = -(-(batch + 1) // 8) * 8
    c_cols = jnp.zeros((d, rows), F32).at[:, :batch].set(c.T).at[:, batch].set(c_ctx)
    cos, sin = _rope_tables(seq)

    l = 0
    mod3 = _ada_mod(c_cols, batch + 1, w_ada[l], b_ada[l]).reshape(rows, 1, N_MOD * d)

    flt = (flt_w1[l], flt_b1[l], flt_w2[l], flt_b2[l], flt_w3[l], flt_b3[l], flt_w4[l],
           flt_freq[l], flt_decay[l])
    kfr, kfi, (w1u, w1d) = _hyena_filter(flt, seq, d - attn_w, to_bf16=(w_ffn1_up[l], w_ffn1_down[l]))
    qkv_w = attn_w + 2 * kvw
    qn = q_norm[l].reshape(1, HEAD_DIM)
    kn = k_norm[l].reshape(1, HEAD_DIM)

    x1, (wi, wo) = _ffn(xs, mod3, lat_row_ffn, 0, g_norm[l, 0], w1u, w1d, tmf, to_bf16=(w_in[l], w_out[l]))
    c1, _ = _ffn(cs, mod3, ctx_row, 0, g_norm[l, 0], w1u, w1d, tmc)

    q, k, v, x0c, vg = _in_proj(x1, mod3, lat_row, g_norm[l, 1], wi, qn, kn,
                                cos, sin, conv_w[l], conv_b[l], seq, tm)
    kc, vc = _ctx_kv(c1, mod3, batch, g_norm[l, 1], wi[:, attn_w:qkv_w], kn, tmc)
    k_all = jnp.concatenate([kc.reshape(batch, lc, kvw), k.reshape(batch, seq, kvw)], axis=1)
    v_all = jnp.concatenate([vc.reshape(batch, lc, kvw), v.reshape(batch, seq, kvw)], axis=1)
    attn, (w2u, w2d) = _attention(q, k_all, v_all, batch, seq, to_bf16=(w_ffn2_up[l], w_ffn2_down[l]))

    hyn = _hyena(x0c, vg, kfr, kfi, hy_bias[l], g_out[l, attn_w:], batch, seq)
    x2 = _out_proj(x1, mod3, lat_row, attn, hyn, g_out[l, :attn_w], wo, tm)

    x3, _ = _ffn(x2, mod3, lat_row_ffn, 2, g_norm[l, 2], w2u, w2d, tmf)
    return x3.reshape(batch, seq, d)
```

```python
import functools
import math

import jax
import jax.numpy as jnp
import numpy as np
from jax import lax
from jax.experimental import pallas as pl
from jax.experimental.pallas import tpu as pltpu

F32 = jnp.float32
BF16 = jnp.bfloat16

HEAD_DIM = 128
N_KV_HEADS = 2
GRID_W = 64
ROPE_THETA = 10000.0
FILTER_BANDS = 16
RMS_EPS = 1e-6
N_MOD = 9

V7X_VMEM_LIMIT_BYTES = 56 * 1024 * 1024
SUBLANES = 8
PACKED_ROWS = 2 * SUBLANES
FFN_ROWS = 1024
DFT_N2 = 128

_HI = lax.Precision.HIGHEST


def _params(*sem):
    return pltpu.CompilerParams(dimension_semantics=sem, vmem_limit_bytes=V7X_VMEM_LIMIT_BYTES)


def _pick_tile(n, cap, mult):
    best = None
    t = mult
    while t <= min(n, cap):
        if n % t == 0:
            best = t
        t += mult
    assert best is not None, (n, cap, mult)
    return best


def _rms(x, eps=RMS_EPS):
    return x * lax.rsqrt(jnp.mean(x * x, axis=-1, keepdims=True) + eps)


def _side_cast_specs(mats, n_steps, step_of):
    specs = []
    for w in mats:
        rb = PACKED_ROWS * -(-w.shape[0] // (PACKED_ROWS * n_steps))
        assert w.shape[0] % rb == 0, (w.shape, rb)
        nblk = w.shape[0] // rb
        specs.append(pl.BlockSpec((rb, w.shape[1]),
                                  lambda *idx, nblk=nblk: (jnp.minimum(step_of(*idx), nblk - 1), 0)))
    return specs


def _side_cast(side_in, side_out):
    for src, dst in zip(side_in, side_out):
        dst[...] = src[...].astype(BF16)


def _ada_kernel(ct_ref, w_ref, b_ref, o_ref, sb_ref, *, n_rows):
    lanes = sb_ref.shape[2]

    @pl.when(pl.program_id(0) == 0)
    def _():
        ct = ct_ref[...]
        s = ct * jax.nn.sigmoid(ct)
        for r in range(n_rows):
            sb_ref[r] = jnp.broadcast_to(s[:, r:r + 1], sb_ref.shape[1:])

    d, tn = w_ref.shape
    reps = tn // lanes
    acc = [jnp.zeros((SUBLANES, tn), F32) for _ in range(n_rows)]
    for kb in range(d // SUBLANES):
        rows = slice(kb * SUBLANES, (kb + 1) * SUBLANES)
        w = w_ref[rows, :]
        for r in range(n_rows):
            acc[r] = acc[r] + w * jnp.concatenate([sb_ref[r, rows, :]] * reps, axis=1)
    out = [jnp.sum(a, axis=0, keepdims=True) for a in acc]
    out.append(jnp.zeros((o_ref.shape[0] - n_rows, tn), F32))
    o_ref[...] = jnp.concatenate(out, axis=0) + b_ref[...]


def _ada_mod(c_cols, n_rows, w_ada, b_ada):
    d, rows = c_cols.shape
    n = w_ada.shape[1]
    tn = _pick_tile(n, 2048, 128)
    return pl.pallas_call(
        functools.partial(_ada_kernel, n_rows=n_rows),
        out_shape=jax.ShapeDtypeStruct((rows, n), F32),
        grid=(n // tn,),
        in_specs=[pl.BlockSpec((d, rows), lambda j: (0, 0)),
                  pl.BlockSpec((d, tn), lambda j: (0, j)),
                  pl.BlockSpec((1, tn), lambda j: (0, j))],
        out_specs=pl.BlockSpec((rows, tn), lambda j: (0, j)),
        scratch_shapes=[pltpu.VMEM((n_rows, d, HEAD_DIM), F32)],
        compiler_params=_params("arbitrary"),
        name="ada_mod",
    )(c_cols, w_ada, b_ada.reshape(1, n))


def _ffn_kernel(x_ref, shift_ref, scale_ref, gate_ref, g_ref, wg_ref, wu_ref, wd_ref, *refs, tn, n_side):
    side_in, o_ref, side_out, h_ref = refs[:n_side], refs[n_side], refs[n_side + 1:-1], refs[-1]
    _side_cast(side_in, side_out)
    k = pl.program_id(1)
    last = pl.num_programs(1) - 1
    d = o_ref.shape[1]

    def chunk(h, emit):
        g = jnp.dot(h, wg_ref[...], preferred_element_type=F32)
        u = jnp.dot(h, wu_ref[...], preferred_element_type=F32)
        a = (g * jax.nn.sigmoid(g) * u).astype(BF16)
        for n0 in range(0, d, tn):
            emit(n0, jnp.dot(a, wd_ref[:, n0:n0 + tn], preferred_element_type=F32))

    @pl.when(k == 0)
    def _():
        y = _rms(x_ref[...]) * g_ref[...]
        h = (y * (1.0 + scale_ref[0]) + shift_ref[0]).astype(BF16)
        h_ref[...] = h

        def emit(n0, part):
            o_ref[:, n0:n0 + tn] = part
        chunk(h, emit)

    @pl.when(jnp.logical_and(k > 0, k < last))
    def _():
        def emit(n0, part):
            o_ref[:, n0:n0 + tn] += part
        chunk(h_ref[...], emit)

    @pl.when(jnp.logical_and(k > 0, k == last))
    def _():
        def emit(n0, part):
            cols = slice(n0, n0 + tn)
            o_ref[:, cols] = x_ref[:, cols] + 0.5 * gate_ref[0][:, cols] * (o_ref[:, cols] + part)
        chunk(h_ref[...], emit)


def _ffn(x, mod3, row_of_tile, slot, g, w_up, w_down, tm, to_bf16=()):
    t, d = x.shape
    f = w_down.shape[0]
    tf = _pick_tile(f, 512, 128)
    nf = f // tf
    assert nf >= 2, "the kernel keeps separate first / last chunk paths"
    mod_spec = lambda c: pl.BlockSpec((1, 1, d), lambda i, k: (row_of_tile(i), 0, 3 * slot + c))
    row_tile = pl.BlockSpec((tm, d), lambda i, k: (i, 0))
    side_specs = _side_cast_specs(to_bf16, (t // tm) * nf, lambda i, k: i * nf + k)
    outs = pl.pallas_call(
        functools.partial(_ffn_kernel, tn=_pick_tile(d, 512, 128), n_side=len(to_bf16)),
        out_shape=(jax.ShapeDtypeStruct((t, d), F32), *[jax.ShapeDtypeStruct(w.shape, BF16) for w in to_bf16]),
        grid=(t // tm, nf),
        in_specs=[row_tile, mod_spec(0), mod_spec(1), mod_spec(2),
                  pl.BlockSpec((1, d), lambda i, k: (0, 0)),
                  pl.BlockSpec((d, tf), lambda i, k: (0, k)),
                  pl.BlockSpec((d, tf), lambda i, k: (0, k + nf)),
                  pl.BlockSpec((tf, d), lambda i, k: (k, 0)),
                  *side_specs],
        out_specs=(row_tile, *side_specs),
        scratch_shapes=[pltpu.VMEM((tm, d), BF16)],
        compiler_params=_params("arbitrary", "arbitrary"),
        name="ffn",
    )(x, mod3, mod3, mod3, g.reshape(1, d), w_up, w_up, w_down, *to_bf16)
    return outs[0], outs[1:]


def _rope(x, cos, sin):
    lane = lax.broadcasted_iota(jnp.int32, x.shape, 1)
    partner = jnp.where(lane % 2 == 0, pltpu.roll(x, HEAD_DIM - 1, axis=1), pltpu.roll(x, 1, axis=1))
    return x * cos + partner * sin


HALO = 16


def _in_proj_kernel(x_ref, xp_ref, xn_ref, shift_ref, scale_ref, g_ref, w_ref, qn_ref, kn_ref,
                    cos_ref, sin_ref, cw_ref, cb_ref, q_ref, k_ref, v_ref, x0c_ref, vg_ref, *,
                    q_scale, tiles_per_seq, cw):
    i = pl.program_id(0)
    tm = x_ref.shape[0]
    attn_w = q_ref.shape[1]
    kvw = k_ref.shape[2]
    c = x0c_ref.shape[1]
    first = (i % tiles_per_seq) == 0
    last = (i % tiles_per_seq) == tiles_per_seq - 1

    x_ext = jnp.concatenate([xp_ref[...], x_ref[...], xn_ref[...]], axis=0)
    y = _rms(x_ext) * g_ref[...]
    h_ext = (y * (1.0 + scale_ref[0]) + shift_ref[0]).astype(BF16)
    h = h_ext[HALO:HALO + tm]

    qkv_w = attn_w + 2 * kvw
    u = jnp.dot(h, w_ref[:, :qkv_w], preferred_element_type=F32)

    def normed(col, gain):
        return _rope(_rms(u[:, col:col + HEAD_DIM]) * gain, cos_ref[...], sin_ref[...])

    for hd in range(attn_w // HEAD_DIM):
        col = hd * HEAD_DIM
        q_ref[:, col:col + HEAD_DIM] = (normed(col, qn_ref[...]) * q_scale).astype(BF16)
    for hd in range(kvw // HEAD_DIM):
        col = hd * HEAD_DIM
        k_ref[0, :, col:col + HEAD_DIM] = normed(attn_w + col, kn_ref[...]).astype(BF16)
    v_ref[0] = u[:, attn_w + kvw:].astype(BF16)

    row = lax.broadcasted_iota(jnp.int32, (tm, cw), 0)
    at_start = jnp.logical_and(first, row == 0)
    at_end = jnp.logical_and(last, row == tm - 1)
    ext = tm + 2 * HALO

    def conv(group, c0):
        col = qkv_w + group * c + c0
        ue = jnp.dot(h_ext, w_ref[:, col:col + cw], preferred_element_type=F32)
        um = jnp.where(at_start, 0.0, pltpu.roll(ue, 1, axis=0)[HALO:HALO + tm])
        up = jnp.where(at_end, 0.0, pltpu.roll(ue, ext - 1, axis=0)[HALO:HALO + tm])
        w = cw_ref[group][:, c0:c0 + cw]
        return um * w[0:1] + ue[HALO:HALO + tm] * w[1:2] + up * w[2:3] + cb_ref[group][:, c0:c0 + cw]

    for c0 in range(0, c, cw):
        x0c_ref[:, c0:c0 + cw] = conv(0, c0).astype(BF16)
        vg_ref[:, c0:c0 + cw] = (conv(2, c0) * conv(1, c0)).astype(BF16)


def _in_proj(x, mod3, row_of_tile, g, w_in, qn, kn, cos, sin, conv_w, conv_b, seq, lc, tm):
    t, d = x.shape
    batch = t // seq
    kvw = N_KV_HEADS * HEAD_DIM
    attn_w = d // 2
    c = (w_in.shape[1] - attn_w - 2 * kvw) // 3
    cw = _pick_tile(c, 512, 128)
    pos_tiles = seq // tm
    hb = tm // HALO
    nhb = t // HALO
    w3 = conv_w.reshape(3, 3, c).transpose(1, 0, 2)
    b3 = conv_b.reshape(3, 1, c)
    mod_spec = lambda ch: pl.BlockSpec((1, 1, d), lambda i: (row_of_tile(i), 0, 3 + ch))
    const = lambda shape: pl.BlockSpec(shape, lambda i: (0,) * len(shape), pipeline_mode=pl.Buffered(1))
    kern = functools.partial(_in_proj_kernel, q_scale=HEAD_DIM ** -0.5 * math.log2(math.e),
                             tiles_per_seq=pos_tiles, cw=cw)
    return pl.pallas_call(
        kern,
        out_shape=(jax.ShapeDtypeStruct((t, attn_w), BF16),
                   jax.ShapeDtypeStruct((batch, seq + lc, kvw), BF16),
                   jax.ShapeDtypeStruct((batch, seq + lc, kvw), BF16),
                   jax.ShapeDtypeStruct((t, c), BF16),
                   jax.ShapeDtypeStruct((t, c), BF16)),
        grid=(t // tm,),
        in_specs=[pl.BlockSpec((tm, d), lambda i: (i, 0)),
                  pl.BlockSpec((HALO, d), lambda i: (jnp.maximum(i * hb - 1, 0), 0)),
                  pl.BlockSpec((HALO, d), lambda i: (jnp.minimum((i + 1) * hb, nhb - 1), 0)),
                  mod_spec(0), mod_spec(1),
                  const((1, d)), const(w_in.shape),
                  const((1, HEAD_DIM)), const((1, HEAD_DIM)),
                  pl.BlockSpec((tm, HEAD_DIM), lambda i: (i % pos_tiles, 0)),
                  pl.BlockSpec((tm, HEAD_DIM), lambda i: (i % pos_tiles, 0)),
                  const((3, 3, c)), const((3, 1, c))],
        out_specs=(pl.BlockSpec((tm, attn_w), lambda i: (i, 0)),
                   pl.BlockSpec((1, tm, kvw), lambda i: (i // pos_tiles, i % pos_tiles, 0)),
                   pl.BlockSpec((1, tm, kvw), lambda i: (i // pos_tiles, i % pos_tiles, 0)),
                   pl.BlockSpec((tm, c), lambda i: (i, 0)),
                   pl.BlockSpec((tm, c), lambda i: (i, 0))),
        compiler_params=_params("parallel"),
        name="in_proj",
    )(x, x, x, mod3, mod3, g.reshape(1, d), w_in, qn, kn, cos, sin, w3, b3)


def _ctx_kv_kernel(x_ref, shift_ref, scale_ref, g_ref, w_ref, kn_ref, k_in_ref, v_in_ref, k_ref, v_ref):
    del k_in_ref, v_in_ref
    y = _rms(x_ref[...]) * g_ref[...]
    h = (y * (1.0 + scale_ref[0]) + shift_ref[0]).astype(BF16)
    u = jnp.dot(h, w_ref[...], preferred_element_type=F32)
    for hd in range(N_KV_HEADS):
        k_ref[0, :, hd * HEAD_DIM:(hd + 1) * HEAD_DIM] = (
            _rms(u[:, hd * HEAD_DIM:(hd + 1) * HEAD_DIM]) * kn_ref[...]).astype(BF16)
    v_ref[0] = u[:, N_KV_HEADS * HEAD_DIM:].astype(BF16)


def _ctx_kv(x, mod3, ctx_row, g, w_kv, kn, k_all, v_all, seq):
    t, d = x.shape
    batch, lk, kvw = k_all.shape
    lc = lk - seq
    assert t == batch * lc and seq % lc == 0, (t, batch, lc, seq)
    mod_spec = lambda c: pl.BlockSpec((1, 1, d), lambda b: (ctx_row, 0, 3 + c))
    rows = pl.BlockSpec((1, lc, kvw), lambda b: (b, seq // lc, 0))
    return pl.pallas_call(
        _ctx_kv_kernel,
        out_shape=(jax.ShapeDtypeStruct(k_all.shape, BF16), jax.ShapeDtypeStruct(v_all.shape, BF16)),
        grid=(batch,),
        in_specs=[pl.BlockSpec((lc, d), lambda b: (b, 0)),
                  mod_spec(0), mod_spec(1),
                  pl.BlockSpec((1, d), lambda b: (0, 0)),
                  pl.BlockSpec((d, 2 * kvw), lambda b: (0, 0)),
                  pl.BlockSpec((1, HEAD_DIM), lambda b: (0, 0)),
                  pl.BlockSpec(memory_space=pl.ANY),
                  pl.BlockSpec(memory_space=pl.ANY)],
        out_specs=(rows, rows),
        input_output_aliases={6: 0, 7: 1},
        compiler_params=_params("parallel"),
        name="ctx_kv",
    )(x, mod3, mod3, g.reshape(1, d), w_kv, kn, k_all, v_all)


def _attn_kernel(q_ref, k_ref, v_ref, *refs, group, nq, n_side):
    side_in, o_ref, side_out = refs[:n_side], refs[n_side], refs[n_side + 1:2 * n_side + 1]
    s0_ref, s1_ref, mx0_ref, mx1_ref, ls_ref, acc_ref = refs[2 * n_side + 1:]
    _side_cast(side_in, side_out)
    i = pl.program_id(2)
    tq = q_ref.shape[0]
    nk, _, tk = s0_ref.shape
    lanes = HEAD_DIM
    bufs = ((s0_ref, mx0_ref), (s1_ref, mx1_ref))

    def scores(j, q, s_ref, mx_ref):
        start = pl.multiple_of(j * tk, tk)
        s = lax.dot_general(q, k_ref[0, pl.ds(start, tk), :], (((1,), (1,)), ((), ())),
                            preferred_element_type=F32)
        s_ref[j] = s
        mx = mx_ref[...]
        for c in range(tk // lanes):
            mx = jnp.maximum(mx, s[:, c * lanes:(c + 1) * lanes])
        mx_ref[...] = mx

    def values(j, s_ref, mx_ref):
        start = pl.multiple_of(j * tk, tk)
        s = s_ref[j]
        m = mx_ref[...]
        ls = ls_ref[...]
        parts = []
        for c in range(tk // lanes):
            pc = jnp.exp2(s[:, c * lanes:(c + 1) * lanes] - m)
            ls = ls + pc
            parts.append(pc.astype(BF16))
        ls_ref[...] = ls
        p = jnp.concatenate(parts, axis=1)
        acc_ref[...] += jnp.dot(p, v_ref[0, pl.ds(start, tk), :], preferred_element_type=F32)

    def begin_scores(mx_ref):
        mx_ref[...] = jnp.full_like(mx_ref, -jnp.inf)
        return jnp.concatenate([q_ref[:, h * HEAD_DIM:(h + 1) * HEAD_DIM] for h in range(group)], axis=0)

    def end_scores(mx_ref):
        mx_ref[...] = jnp.broadcast_to(jnp.max(mx_ref[...], axis=-1, keepdims=True), mx_ref.shape)

    def begin_values():
        ls_ref[...] = jnp.zeros_like(ls_ref)
        acc_ref[...] = jnp.zeros_like(acc_ref)

    def end_values():
        o = acc_ref[...] / jnp.sum(ls_ref[...], axis=-1, keepdims=True)
        for h in range(group):
            o_ref[:, h * HEAD_DIM:(h + 1) * HEAD_DIM] = o[h * tq:(h + 1) * tq].astype(BF16)

    @pl.when(i == 0)
    def _():
        s_ref, mx_ref = bufs[0]
        q = begin_scores(mx_ref)
        lax.fori_loop(0, nk, lambda j, c: (scores(j, q, s_ref, mx_ref), c)[1], 0)
        end_scores(mx_ref)

    for parity in range(2):
        @pl.when(jnp.logical_and(jnp.logical_and(i > 0, i < nq), i % 2 == parity))
        def _():
            (cs_ref, cmx_ref), (ps_ref, pmx_ref) = bufs[parity], bufs[1 - parity]
            q = begin_scores(cmx_ref)
            begin_values()

            def both(j, c):
                scores(j, q, cs_ref, cmx_ref)
                values(j, ps_ref, pmx_ref)
                return c

            lax.fori_loop(0, nk, both, 0, unroll=True)
            end_scores(cmx_ref)
            end_values()

    @pl.when(i == nq)
    def _():
        s_ref, mx_ref = bufs[(nq - 1) % 2]
        begin_values()
        lax.fori_loop(0, nk, lambda j, c: (values(j, s_ref, mx_ref), c)[1], 0)
        end_values()


def _attention(q, k_all, v_all, batch, seq, to_bf16=()):
    t, attn_w = q.shape
    lk = k_all.shape[1]
    group = attn_w // HEAD_DIM // N_KV_HEADS
    gw = group * HEAD_DIM
    tq = _pick_tile(seq, 128, 8)
    tk = _pick_tile(lk, 1408, 128)
    nq = seq // tq
    rows = group * tq
    side_specs = _side_cast_specs(to_bf16, batch * N_KV_HEADS * (nq + 1),
                                  lambda b, g, i: (b * N_KV_HEADS + g) * (nq + 1) + i)
    kern = functools.partial(_attn_kernel, group=group, nq=nq, n_side=len(to_bf16))
    outs = pl.pallas_call(
        kern,
        out_shape=(jax.ShapeDtypeStruct((t, attn_w), BF16),
                   *[jax.ShapeDtypeStruct(w.shape, BF16) for w in to_bf16]),
        grid=(batch, N_KV_HEADS, nq + 1),
        in_specs=[pl.BlockSpec((tq, gw), lambda b, g, i: (b * nq + jnp.minimum(i, nq - 1), g)),
                  pl.BlockSpec((1, lk, HEAD_DIM), lambda b, g, i: (b, 0, g)),
                  pl.BlockSpec((1, lk, HEAD_DIM), lambda b, g, i: (b, 0, g)),
                  *side_specs],
        out_specs=(pl.BlockSpec((tq, gw), lambda b, g, i: (b * nq + jnp.maximum(i - 1, 0), g)),
                   *side_specs),
        scratch_shapes=[pltpu.VMEM((lk // tk, rows, tk), F32), pltpu.VMEM((lk // tk, rows, tk), F32),
                        pltpu.VMEM((rows, HEAD_DIM), F32), pltpu.VMEM((rows, HEAD_DIM), F32),
                        pltpu.VMEM((rows, HEAD_DIM), F32), pltpu.VMEM((rows, HEAD_DIM), F32)],
        compiler_params=_params("arbitrary", "arbitrary", "arbitrary"),
        name="attention",
    )(q, k_all, v_all, *to_bf16)
    return outs[0], outs[1:]


def _dft_tables(n1, n2, k1p):
    n = n1 * n2
    k1 = np.arange(k1p)[:, None]
    a = np.arange(n1)[None, :]
    m2 = np.arange(n2)[:, None, None]
    theta = 2.0 * np.pi * ((((n2 * a * k1)[None] + m2 * k1[None]) % n) / n)
    k2 = np.arange(n2)[:, None]
    b = np.arange(n2)[None, :]
    phi = 2.0 * np.pi * ((k2 * b) % n2) / n2
    c, s = np.cos(phi), np.sin(phi)
    fmat = np.block([[c, s], [-s, c]])
    return np.cos(theta), np.sin(theta), fmat


def _mix_forward(cos, sin, n1h):
    n2, k1p, _ = cos.shape
    g = n2 // SUBLANES
    a = np.stack([cos[:, :, :n1h], -sin[:, :, :n1h]], axis=1).reshape(g, SUBLANES, 2, k1p, n1h)
    m = np.zeros((g, 2, k1p, SUBLANES, n1h, SUBLANES))
    for j in range(SUBLANES):
        m[:, :, :, j, :, j] = a[:, j]
    return m.reshape(g, 2 * k1p * SUBLANES, n1h * SUBLANES)


def _mix_inverse(cos, sin, n1h, n1):
    n2, k1p, _ = cos.shape
    g = n2 // SUBLANES
    k1 = np.arange(k1p)
    weight = np.where((k1 == 0) | (k1 == n1 // 2), 1.0, np.where(k1 < n1 // 2, 2.0, 0.0)) / (n1 * n2)
    a = np.stack([cos[:, :, :n1h], -sin[:, :, :n1h]], axis=1) * weight[None, None, :, None]
    a = a.reshape(g, SUBLANES, 2, k1p, n1h)
    m = np.zeros((g, n1h, SUBLANES, 2, k1p, SUBLANES))
    for j in range(SUBLANES):
        m[:, :, j, :, :, j] = a[:, j].transpose(0, 3, 1, 2)
    return m.reshape(g, n1h * SUBLANES, 2 * k1p * SUBLANES)


def _stage_a_kernel(m_ref, x_ref, yr_ref, yi_ref):
    _, n1h, _, c = x_ref.shape
    k1p = yr_ref.shape[1]
    re, im = [], []
    xs = x_ref[0].astype(F32)
    for half in range(2):
        rows = slice(half * SUBLANES, (half + 1) * SUBLANES)
        x = xs[:, rows, :].reshape(n1h * SUBLANES, c).astype(BF16)
        y = jnp.dot(m_ref[half], x, preferred_element_type=F32)
        re.append(y[:k1p * SUBLANES].reshape(k1p, SUBLANES, c))
        im.append(y[k1p * SUBLANES:].reshape(k1p, SUBLANES, c))
    yr_ref[0] = jnp.concatenate(re, axis=1).astype(BF16)
    yi_ref[0] = jnp.concatenate(im, axis=1).astype(BF16)


def _stage_a(mix, vg4, k1p):
    b, n1h, n2, c = vg4.shape
    return pl.pallas_call(
        _stage_a_kernel,
        out_shape=(jax.ShapeDtypeStruct((b, k1p, n2, c), BF16),) * 2,
        grid=(n2 // PACKED_ROWS, b),
        in_specs=[pl.BlockSpec((2,) + mix.shape[1:], lambda g, bi: (g, 0, 0)),
                  pl.BlockSpec((1, n1h, PACKED_ROWS, c), lambda g, bi: (bi, 0, g, 0))],
        out_specs=(pl.BlockSpec((1, k1p, PACKED_ROWS, c), lambda g, bi: (bi, 0, g, 0)),) * 2,
        compiler_params=_params("parallel", "parallel"),
        name="hyena_dft_a",
    )(mix, vg4)


def _stage_b_kernel(yr_ref, yi_ref, f_ref, ft_ref, kr_ref, ki_ref, ur_ref, ui_ref):
    kb, n2 = yr_ref.shape[1:3]
    for r in range(kb):
        y = jnp.concatenate([yr_ref[0, r], yi_ref[0, r]], axis=0)
        z = jnp.dot(f_ref[...], y, preferred_element_type=F32)
        zr, zi = z[:n2], z[n2:]
        kr = kr_ref[r].astype(F32)
        ki = ki_ref[r].astype(F32)
        p = jnp.concatenate([zr * kr - zi * ki, zr * ki + zi * kr], axis=0).astype(BF16)
        u = jnp.dot(ft_ref[...], p, preferred_element_type=F32)
        ur_ref[0, r] = u[:n2].astype(BF16)
        ui_ref[0, r] = u[n2:].astype(BF16)


def _stage_b(yr, yi, fmat, fmat_t, kfr, kfi, kb):
    b, k1p, n2, c = yr.shape
    blk = pl.BlockSpec((1, kb, n2, c), lambda k, bi: (bi, k, 0, 0))
    flt = pl.BlockSpec((kb, n2, c), lambda k, bi: (k, 0, 0))
    mat = pl.BlockSpec((2 * n2, 2 * n2), lambda k, bi: (0, 0))
    return pl.pallas_call(
        _stage_b_kernel,
        out_shape=(jax.ShapeDtypeStruct((b, k1p, n2, c), BF16),) * 2,
        grid=(k1p // kb, b),
        in_specs=[blk, blk, mat, mat, flt, flt],
        out_specs=(blk, blk),
        compiler_params=_params("parallel", "parallel"),
        name="hyena_dft_b",
    )(yr, yi, fmat, fmat_t, kfr, kfi)


def _stage_c_kernel(m_ref, ur_ref, ui_ref, vg_ref, x0_ref, hb_ref, g_ref, o_ref):
    _, k1p, _, c = ur_ref.shape
    n1h = vg_ref.shape[1]
    ur = ur_ref[0].astype(F32)
    ui = ui_ref[0].astype(F32)
    vgs = vg_ref[0].astype(F32)
    x0s = x0_ref[0].astype(F32)
    for half in range(2):
        rows = slice(half * SUBLANES, (half + 1) * SUBLANES)
        u = jnp.concatenate([ur[:, rows, :].reshape(k1p * SUBLANES, c),
                             ui[:, rows, :].reshape(k1p * SUBLANES, c)], axis=0).astype(BF16)
        conv = jnp.dot(m_ref[half], u, preferred_element_type=F32)
        vg = vgs[:, rows, :].reshape(n1h * SUBLANES, c)
        x0 = x0s[:, rows, :].reshape(n1h * SUBLANES, c)
        hyo = (conv + hb_ref[...] * vg) * x0
        o_ref[0, :, rows, :] = (_rms(hyo) * g_ref[...]).reshape(n1h, SUBLANES, c)


def _stage_c(mix, ur, ui, vg4, x04, hy_bias, g_hy):
    b, n1h, n2, c = vg4.shape
    k1p = ur.shape[1]
    ublk = pl.BlockSpec((1, k1p, PACKED_ROWS, c), lambda g, bi: (bi, 0, g, 0))
    xblk = pl.BlockSpec((1, n1h, PACKED_ROWS, c), lambda g, bi: (bi, 0, g, 0))
    vec = pl.BlockSpec((1, c), lambda g, bi: (0, 0))
    return pl.pallas_call(
        _stage_c_kernel,
        out_shape=jax.ShapeDtypeStruct((b, n1h, n2, c), F32),
        grid=(n2 // PACKED_ROWS, b),
        in_specs=[pl.BlockSpec((2,) + mix.shape[1:], lambda g, bi: (g, 0, 0)),
                  ublk, ublk, xblk, xblk, vec, vec],
        out_specs=xblk,
        compiler_params=_params("parallel", "parallel"),
        name="hyena_dft_c",
    )(mix, ur, ui, vg4, x04, hy_bias.reshape(1, c), g_hy.reshape(1, c))


def _filter_a_kernel(a_ref, w1t_ref, w1c_ref, w1s_ref, b1_ref, w2_ref, b2_ref, w3_ref, b3_ref,
                     w4_ref, fr_ref, dec_ref, *refs, seq, n_side):
    side_in, (yr_ref, yi_ref, norm_ref), side_out = refs[:n_side], refs[n_side:n_side + 3], refs[n_side + 3:]
    _side_cast(side_in, side_out)
    g = pl.program_id(0)
    k1p = yr_ref.shape[0]
    c = norm_ref.shape[1]
    n1 = a_ref.shape[2]
    n1h = n1 // 2
    n2 = DFT_N2

    def lags(idx, j):
        tprime = idx * n2 + (g * SUBLANES + j)
        fwd = idx < n1h
        return tprime, fwd, jnp.where(fwd, tprime, 2 * seq - tprime).astype(F32)

    def stacked(idx, axis):
        parts = [lags(idx, j) for j in range(SUBLANES)]
        return [jnp.concatenate([p[k] for p in parts], axis=axis) for k in range(3)]

    _, _, pos_l = stacked(lax.broadcasted_iota(jnp.int32, (1, n1), 1), 1)
    tprime_r, fwd_r, pos_r = stacked(lax.broadcasted_iota(jnp.int32, (n1, 1), 0), 0)

    band = lax.broadcasted_iota(jnp.int32, (FILTER_BANDS, 1), 0).astype(F32)
    f = 1e-4 + band * ((FILTER_BANDS - 1 - 1e-4) / (FILTER_BANDS - 1))
    ang = f * ((2.0 * math.pi * pos_l) / seq)
    fr = fr_ref[...]
    dot = functools.partial(jnp.dot, preferred_element_type=F32, precision=_HI)
    h = w1t_ref[...] * (pos_l / (seq - 1.0)) + dot(w1c_ref[...], jnp.cos(ang)) + dot(w1s_ref[...], -jnp.sin(ang))
    h = jnp.sin(fr * (h + b1_ref[...]))
    h = jnp.sin(fr * (dot(w2_ref[...], h) + b2_ref[...]))
    h = jnp.sin(fr * (dot(w3_ref[...], h) + b3_ref[...]))
    h = lax.dot_general(h, w4_ref[...], (((0,), (0,)), ((), ())), preferred_element_type=F32)
    t_r = pos_r / (seq - 1.0)
    dec = jnp.abs(dec_ref[...])
    kf = h[:, :c] * jnp.exp(-t_r * dec[0:1, :])
    kb = h[:, c:] * jnp.exp(-t_r * dec[1:2, :])
    kk = jnp.where(fwd_r, kf, jnp.where(tprime_r == seq, 0.0, kb))

    @pl.when(g == 0)
    def _():
        norm_ref[...] = jnp.zeros_like(norm_ref)

    norm_ref[...] += jnp.sum(jnp.abs(kk), axis=0, keepdims=True)
    for j in range(SUBLANES):
        y = jnp.dot(a_ref[j], kk[j * n1:(j + 1) * n1].astype(BF16), preferred_element_type=F32)
        yr_ref[:, j * c:(j + 1) * c] = y[:k1p].astype(BF16)
        yi_ref[:, j * c:(j + 1) * c] = y[k1p:].astype(BF16)


def _filter_a(a_full, flt, seq, c, to_bf16=()):
    w1, b1, w2, b2, w3, b3, w4, freq, decay = flt
    hid = w2.shape[0]
    n2, k1p2, n1 = a_full.shape
    k1p = k1p2 // 2
    full = lambda shape: pl.BlockSpec(shape, lambda g: (0,) * len(shape))
    side_specs = _side_cast_specs(to_bf16, n2 // SUBLANES, lambda g: g)
    kern = functools.partial(_filter_a_kernel, seq=seq, n_side=len(to_bf16))
    outs = pl.pallas_call(
        kern,
        out_shape=(jax.ShapeDtypeStruct((k1p, n2 * c), BF16),
                   jax.ShapeDtypeStruct((k1p, n2 * c), BF16),
                   jax.ShapeDtypeStruct((1, c), F32),
                   *[jax.ShapeDtypeStruct(w.shape, BF16) for w in to_bf16]),
        grid=(n2 // SUBLANES,),
        in_specs=[pl.BlockSpec((SUBLANES, k1p2, n1), lambda g: (g, 0, 0)),
                  full((hid, 1)), full((hid, FILTER_BANDS)), full((hid, FILTER_BANDS)), full((hid, 1)),
                  full((hid, hid)), full((hid, 1)), full((hid, hid)), full((hid, 1)),
                  full((hid, 2 * c)), full((hid, 1)), full((2, c)),
                  *side_specs],
        out_specs=(pl.BlockSpec((k1p, SUBLANES * c), lambda g: (0, g)),
                   pl.BlockSpec((k1p, SUBLANES * c), lambda g: (0, g)),
                   pl.BlockSpec((1, c), lambda g: (0, 0)),
                   *side_specs),
        compiler_params=_params("arbitrary"),
        name="hyena_filter_a",
    )(a_full, w1[0:1].T, w1[1:1 + FILTER_BANDS].T, w1[1 + FILTER_BANDS:].T, b1.reshape(hid, 1),
      w2.T, b2.reshape(hid, 1), w3.T, b3.reshape(hid, 1), w4, freq.reshape(hid, 1), decay, *to_bf16)
    return outs[0], outs[1], outs[2], outs[3:]


def _filter_b_kernel(yr_ref, yi_ref, f_ref, norm_ref, kr_ref, ki_ref):
    kb, n2 = yr_ref.shape[:2]
    for r in range(kb):
        y = jnp.concatenate([yr_ref[r], yi_ref[r]], axis=0)
        z = jnp.dot(f_ref[...], y, preferred_element_type=F32) / norm_ref[...]
        kr_ref[r] = z[:n2].astype(BF16)
        ki_ref[r] = z[n2:].astype(BF16)


def _filter_b(yr, yi, fmat, norm, kb):
    k1p, n2, c = yr.shape
    blk = pl.BlockSpec((kb, n2, c), lambda k: (k, 0, 0))
    return pl.pallas_call(
        _filter_b_kernel,
        out_shape=(jax.ShapeDtypeStruct((k1p, n2, c), BF16),) * 2,
        grid=(k1p // kb,),
        in_specs=[blk, blk,
                  pl.BlockSpec((2 * n2, 2 * n2), lambda k: (0, 0)),
                  pl.BlockSpec((1, c), lambda k: (0, 0))],
        out_specs=(blk, blk),
        compiler_params=_params("parallel"),
        name="hyena_filter_b",
    )(yr, yi, fmat, norm)


def _dft_plan(seq):
    n2 = DFT_N2
    n1h = seq // n2
    n1 = 2 * n1h
    k1p = min(n1, -(-(n1 // 2 + 1) // SUBLANES) * SUBLANES)
    return n1h, n1, n2, k1p, _pick_tile(k1p, 8, 1)


def _hyena_filter(flt, seq, c, to_bf16=()):
    n1h, n1, n2, k1p, kb = _dft_plan(seq)
    cos, sin, fmat_np = _dft_tables(n1, n2, k1p)
    a_full = jnp.asarray(np.concatenate([cos, -sin], axis=1), BF16)
    fyr, fyi, norm, sides = _filter_a(a_full, flt, seq, c, to_bf16)
    kfr, kfi = _filter_b(fyr.reshape(k1p, n2, c), fyi.reshape(k1p, n2, c), jnp.asarray(fmat_np, BF16), norm, kb)
    return kfr, kfi, sides


def _hyena(x0c, vg, kfr, kfi, hy_bias, g_hy, batch, seq):
    c = vg.shape[1]
    n1h, n1, n2, k1p, kb = _dft_plan(seq)
    cos, sin, fmat_np = _dft_tables(n1, n2, k1p)
    mix_a = jnp.asarray(_mix_forward(cos, sin, n1h), BF16)
    mix_c = jnp.asarray(_mix_inverse(cos, sin, n1h, n1), BF16)
    fmat = jnp.asarray(fmat_np, BF16)
    fmat_t = jnp.asarray(fmat_np.T, BF16)

    vg4 = vg.reshape(batch, n1h, n2, c)
    x04 = x0c.reshape(batch, n1h, n2, c)
    yr, yi = _stage_a(mix_a, vg4, k1p)
    ur, ui = _stage_b(yr, yi, fmat, fmat_t, kfr, kfi, kb)
    y = _stage_c(mix_c, ur, ui, vg4, x04, hy_bias, g_hy)
    return y.reshape(batch * seq, c)


def _out_proj_kernel(x_ref, gate_ref, attn_ref, hy_ref, ga_ref, wa_ref, wh_ref, o_ref):
    ya = (_rms(attn_ref[...].astype(F32)) * ga_ref[...]).astype(BF16)
    y = jnp.dot(ya, wa_ref[...], preferred_element_type=F32)
    y += jnp.dot(hy_ref[...].astype(BF16), wh_ref[...], preferred_element_type=F32)
    o_ref[...] = x_ref[...] + gate_ref[0] * y


def _out_proj(x, mod3, row_of_tile, attn, hyn, g_attn, w_out, tm):
    t, d = x.shape
    aw = attn.shape[1]
    hw = hyn.shape[1]
    const = lambda shape, idx: pl.BlockSpec(shape, lambda i: idx, pipeline_mode=pl.Buffered(1))
    return pl.pallas_call(
        _out_proj_kernel,
        out_shape=jax.ShapeDtypeStruct((t, d), F32),
        grid=(t // tm,),
        in_specs=[pl.BlockSpec((tm, d), lambda i: (i, 0)),
                  pl.BlockSpec((1, 1, d), lambda i: (row_of_tile(i), 0, 5)),
                  pl.BlockSpec((tm, aw), lambda i: (i, 0)),
                  pl.BlockSpec((tm, hw), lambda i: (i, 0)),
                  const((1, aw), (0, 0)), const((aw, d), (0, 0)), const((hw, d), (1, 0))],
        out_specs=pl.BlockSpec((tm, d), lambda i: (i, 0)),
        compiler_params=_params("parallel"),
        name="out_proj",
    )(x, mod3, attn, hyn, g_attn.reshape(1, aw), w_out, w_out)


def _rope_tables(seq):
    half = HEAD_DIM // 2
    pos = np.arange(seq)
    inv = ROPE_THETA ** (-np.arange(0, half, 2, dtype=np.float64) / half)
    ang = np.concatenate([(pos // GRID_W)[:, None] * inv, (pos % GRID_W)[:, None] * inv], axis=-1)
    cos = np.repeat(np.cos(ang), 2, axis=-1)
    sin = np.stack([-np.sin(ang), np.sin(ang)], axis=-1).reshape(seq, HEAD_DIM)
    return jnp.asarray(cos, F32), jnp.asarray(sin, F32)


def kernel(x, c, ctx, c_ctx, w_ada, b_ada, g_norm, w_ffn1_up, w_ffn1_down, w_ffn2_up, w_ffn2_down,
           w_in, q_norm, k_norm, conv_w, conv_b, flt_w1, flt_b1, flt_w2, flt_b2, flt_w3, flt_b3,
           flt_w4, flt_freq, flt_decay, hy_bias, g_out, w_out):
    batch, seq, d = x.shape
    lc = ctx.shape[1]
    depth = w_ada.shape[0]
    assert depth == 1, "context-update path of deeper stacks is not implemented"
    attn_w = d // 2
    kvw = N_KV_HEADS * HEAD_DIM
    tm = _pick_tile(seq, 512, 128)
    tmf = _pick_tile(seq, FFN_ROWS, 128)
    tmc = _pick_tile(batch * lc, 512, 8)
    lat_row = lambda i: i // (seq // tm)
    lat_row_ffn = lambda i: i // (seq // tmf)
    ctx_row = lambda i: batch

    xs = x.reshape(batch * seq, d)
    cs = ctx.reshape(batch * lc, d)
    rows = -(-(batch + 1) // 8) * 8
    c_cols = jnp.zeros((d, rows), F32).at[:, :batch].set(c.T).at[:, batch].set(c_ctx)
    cos, sin = _rope_tables(seq)

    l = 0
    mod3 = _ada_mod(c_cols, batch + 1, w_ada[l], b_ada[l]).reshape(rows, 1, N_MOD * d)

    flt = (flt_w1[l], flt_b1[l], flt_w2[l], flt_b2[l], flt_w3[l], flt_b3[l], flt_w4[l],
           flt_freq[l], flt_decay[l])
    kfr, kfi, (w1u, w1d) = _hyena_filter(flt, seq, d - attn_w, to_bf16=(w_ffn1_up[l], w_ffn1_down[l]))
    qkv_w = attn_w + 2 * kvw
    qn = q_norm[l].reshape(1, HEAD_DIM)
    kn = k_norm[l].reshape(1, HEAD_DIM)

    x1, (wi, wo) = _ffn(xs, mod3, lat_row_ffn, 0, g_norm[l, 0], w1u, w1d, tmf, to_bf16=(w_in[l], w_out[l]))
    c1, _ = _ffn(cs, mod3, ctx_row, 0, g_norm[l, 0], w1u, w1d, tmc)

    q, k_all, v_all, x0c, vg = _in_proj(x1, mod3, lat_row, g_norm[l, 1], wi, qn, kn,
                                        cos, sin, conv_w[l], conv_b[l], seq, lc, tm)
    k_all, v_all = _ctx_kv(c1, mod3, batch, g_norm[l, 1], wi[:, attn_w:qkv_w], kn, k_all, v_all, seq)
    attn, (w2u, w2d) = _attention(q, k_all, v_all, batch, seq, to_bf16=(w_ffn2_up[l], w_ffn2_down[l]))

    hyn = _hyena(x0c, vg, kfr, kfi, hy_bias[l], g_out[l, attn_w:], batch, seq)
    x2 = _out_proj(x1, mod3, lat_row, attn, hyn, g_out[l, :attn_w], wo, tm)

    x3, _ = _ffn(x2, mod3, lat_row_ffn, 2, g_norm[l, 2], w2u, w2d, tmf)
    return x3.reshape(batch, seq, d)
```
